```python
import functools
import jax, jax.numpy as jnp
from jax import lax
import numpy as np

D_MODEL = 1024
BATCH = 16
SEQ = 256
DEPTH = 2
DEC_BATCH = 8
DEC_SEQ = 4096
PAST_LEN = 512

GRID_W = 64
POOL_WIDTH = 256
POOL_GROUPS = 4
POOL_GROUP_DIM = POOL_WIDTH // POOL_GROUPS
POOL_WINDOWS = (2, 4, 8, 16)
NA_HEADS = 8
NA_HEAD_DIM = 64
NA_WIDTH = NA_HEADS * NA_HEAD_DIM
NA_WIN_ROWS = 8
NA_WIN_COLS = 16
GLA_WIDTH = D_MODEL - POOL_WIDTH - NA_WIDTH
GLA_HEADS = 4
GLA_DV = GLA_WIDTH // GLA_HEADS
GLA_DK = GLA_DV // 2
GLA_KEY_WIDTH = GLA_HEADS * GLA_DK
GLA_GATE_RANK = 16
GLA_GATE_TAU = 16.0
GLA_CHUNK = 64
ROPE_THETA = 10000.0
D_FF = -(-8 * D_MODEL // (3 * 256)) * 256
NORM_EPS = 1e-6
IN_SIZES = (POOL_WIDTH, NA_WIDTH, NA_WIDTH, NA_WIDTH, GLA_KEY_WIDTH, GLA_KEY_WIDTH, GLA_WIDTH, GLA_WIDTH, GLA_GATE_RANK, GLA_GATE_RANK)
IN_WIDTH = sum(IN_SIZES)
IN_SPLIT_POINTS = tuple(int(s) for s in np.cumsum(IN_SIZES)[:-1])

kernel_name = 'hybrid_pool_natten_gla_diffusion_step'


def rms_norm(x, gain):
    xf = x.astype(jnp.float32)
    y = xf * lax.rsqrt(jnp.mean(xf * xf, axis=-1, keepdims=True) + NORM_EPS)
    return (y * gain.astype(jnp.float32)).astype(x.dtype)


def axial_rope_tables(L):
    t = jnp.arange(L)
    row = (t // GRID_W).astype(jnp.float32)
    col = (t % GRID_W).astype(jnp.float32)
    half = GLA_DK // 2
    inv_freq = ROPE_THETA ** (-jnp.arange(0, half, 2, dtype=jnp.float32) / half)
    ang_r = (row[:, None] * inv_freq)[:, None, :]
    ang_c = (col[:, None] * inv_freq)[:, None, :]
    return (jnp.cos(ang_r), jnp.sin(ang_r), jnp.cos(ang_c), jnp.sin(ang_c))


def rope_rotate(x, cos, sin):
    n = x.shape[-1] // 2
    x1, x2 = x[..., :n], x[..., n:]
    return jnp.concatenate([x1 * cos - x2 * sin, x2 * cos + x1 * sin], axis=-1)


def apply_axial_rope(x, rope):
    cos_r, sin_r, cos_c, sin_c = rope
    half = x.shape[-1] // 2
    xf = x.astype(jnp.float32)
    out = jnp.concatenate([rope_rotate(xf[..., :half], cos_r, sin_r), rope_rotate(xf[..., half:], cos_c, sin_c)], axis=-1)
    return out.astype(x.dtype)


def pool_mixer(u, w_pool, pool_scale):
    B, L, _ = u.shape
    t = jnp.arange(L)
    uf = u.astype(jnp.float32)
    diffs = []
    for g, win in enumerate(POOL_WINDOWS):
        ug = uf[..., g * POOL_GROUP_DIM:(g + 1) * POOL_GROUP_DIM]
        csum = jnp.concatenate([jnp.zeros((B, 1, POOL_GROUP_DIM), jnp.float32), jnp.cumsum(ug, axis=1)], axis=1)
        lo = jnp.clip(t - win // 2, 0, L)
        hi = jnp.clip(t + win // 2, 0, L)
        mean = (csum[:, hi] - csum[:, lo]) / (hi - lo).astype(jnp.float32)[None, :, None]
        diffs.append(mean - ug)
    d = jnp.stack(diffs, axis=2).astype(u.dtype)
    y = jnp.einsum('blgc,gcd->blgd', d, w_pool).reshape(B, L, POOL_WIDTH)
    return y * pool_scale


def na_context_attn(q, k, v):
    B, L, H, dh = q.shape
    s = jnp.einsum('bqhd,bkhd->bhqk', q, k).astype(jnp.float32) * (dh ** -0.5)
    p = jax.nn.softmax(s, axis=-1).astype(v.dtype)
    return jnp.einsum('bhqk,bkhd->bqhd', p, v).reshape(B, L, H * dh)


def na_latent_attn(q, k, v, k_ctx, v_ctx, rel_bias):
    B, L, H, dh = q.shape
    R = L // GRID_W
    WR = min(NA_WIN_ROWS, R)
    WC = NA_WIN_COLS
    qg = q.reshape(B, R, GRID_W, H, dh)
    kg = k.reshape(B, R, GRID_W, H, dh)
    vg = v.reshape(B, R, GRID_W, H, dh)
    rows = jnp.arange(R)
    cols = jnp.arange(GRID_W)
    row_idx = jnp.clip(rows - WR // 2, 0, R - WR)[:, None] + jnp.arange(WR)
    col_idx = jnp.clip(cols - WC // 2, 0, GRID_W - WC)[:, None] + jnp.arange(WC)
    dr = row_idx - rows[:, None] + (NA_WIN_ROWS - 1)
    dc = col_idx - cols[:, None] + (NA_WIN_COLS - 1)
    bias = rel_bias[:, dr[:, None, :, None], dc[None, :, None, :]]
    bias = jnp.moveaxis(bias, 1, 0).astype(jnp.float32)
    scale = dh ** -0.5
    n_loc = WR * WC

    def row_block(args):
        q_r, ridx, b_r = args
        k_band = jnp.take(kg, ridx, axis=1)
        v_band = jnp.take(vg, ridx, axis=1)
        k_win = k_band[:, :, col_idx]
        v_win = v_band[:, :, col_idx]
        s_loc = jnp.einsum('bwhd,biwjhd->bhwij', q_r, k_win).astype(jnp.float32) * scale + b_r
        s_ctx = jnp.einsum('bwhd,bhnd->bhwn', q_r, k_ctx).astype(jnp.float32) * scale
        s = jnp.concatenate([s_loc.reshape(B, H, GRID_W, n_loc), s_ctx], axis=-1)
        p = jax.nn.softmax(s, axis=-1).astype(v.dtype)
        p_loc = p[..., :n_loc].reshape(B, H, GRID_W, WR, WC)
        p_ctx = p[..., n_loc:]
        return (jnp.einsum('bhwij,biwjhd->bwhd', p_loc, v_win)
                + jnp.einsum('bhwn,bhnd->bwhd', p_ctx, v_ctx))

    o = lax.map(row_block, (jnp.moveaxis(qg, 1, 0), row_idx, bias))
    return jnp.moveaxis(o, 0, 1).reshape(B, L, H * dh)


def gla_chunk_scan(q, k, v, log_a, s0):
    B, L, H, DK = q.shape
    DV = v.shape[-1]
    C = GLA_CHUNK
    N = L // C

    def to_chunks(a):
        return a.astype(jnp.float32).reshape(B, N, C, H, a.shape[-1]).transpose(1, 0, 3, 2, 4)

    mask = jnp.tril(jnp.ones((C, C), dtype=bool))[:, :, None]

    def step(S, inp):
        qc, kc, vc, gc = inp
        b = jnp.cumsum(gc, axis=2)
        o_inter = jnp.einsum('bhtd,bhde->bhte', qc * jnp.exp(b), S)
        decay = jnp.exp(jnp.where(mask, b[:, :, :, None, :] - b[:, :, None, :, :], -jnp.inf))
        attn = jnp.einsum('bhtd,bhsd,bhtsd->bhts', qc, kc, decay)
        o_intra = jnp.einsum('bhts,bhse->bhte', attn, vc)
        b_last = b[:, :, -1:, :]
        S = jnp.exp(b_last[:, :, 0, :])[..., None] * S + jnp.einsum('bhsd,bhse->bhde', kc * jnp.exp(b_last - b), vc)
        return S, o_inter + o_intra

    S, o = lax.scan(step, s0.astype(jnp.float32), (to_chunks(q), to_chunks(k), to_chunks(v), to_chunks(log_a)))
    o = o.transpose(1, 0, 3, 2, 4).reshape(B, L, H, DV)
    return o, S


def gla_mixer(q, k, v, g, lr_f, lr_b, w_gate_f, b_gate_f, w_gate_b, b_gate_b, norm_gain, s0_f, s0_b, rope):
    B, L, _ = q.shape
    q = q.reshape(B, L, GLA_HEADS, GLA_DK) * (GLA_DK ** -0.5)
    k = k.reshape(B, L, GLA_HEADS, GLA_DK)
    v = v.reshape(B, L, GLA_HEADS, GLA_DV)
    if rope is not None:
        q = apply_axial_rope(q, rope)
        k = apply_axial_rope(k, rope)
    la_f = (jax.nn.log_sigmoid((lr_f @ w_gate_f + b_gate_f).astype(jnp.float32)) / GLA_GATE_TAU).reshape(B, L, GLA_HEADS, GLA_DK)
    la_b = (jax.nn.log_sigmoid((lr_b @ w_gate_b + b_gate_b).astype(jnp.float32)) / GLA_GATE_TAU).reshape(B, L, GLA_HEADS, GLA_DK)
    flip = functools.partial(jnp.flip, axis=1)
    o_f, s_f = gla_chunk_scan(q, k, v, la_f, s0_f)
    o_b, s_b = gla_chunk_scan(flip(q), flip(k), flip(v), flip(la_b), s0_b)
    o = rms_norm(o_f + flip(o_b), norm_gain).astype(g.dtype)
    o = o * jax.nn.silu(g.reshape(B, L, GLA_HEADS, GLA_DV))
    return o.reshape(B, L, GLA_WIDTH), s_f, s_b


def swiglu(h, w_ffn_in, w_ffn_out):
    hg, hu = jnp.split(h @ w_ffn_in, 2, axis=-1)
    return (jax.nn.silu(hg) * hu) @ w_ffn_out


def trunk_layer(x, cond, lw, na_fn, s0_f, s0_b, rope):
    B, L, _ = x.shape
    mods = (jax.nn.silu(cond) @ lw['w_ada'] + lw['b_ada'])[:, None, :]
    shift1, scale1, gate1, shift2, scale2, gate2 = jnp.split(mods, 6, axis=-1)
    h = rms_norm(x, lw['norm1_gain']) * (1 + scale1) + shift1
    u_pool, q_na, k_na, v_na, q_la, k_la, v_la, g_la, lr_f, lr_b = jnp.split(h @ lw['w_in'], IN_SPLIT_POINTS, axis=-1)
    y_pool = pool_mixer(u_pool, lw['w_pool'], lw['pool_scale'])
    q_na = rms_norm(q_na.reshape(B, L, NA_HEADS, NA_HEAD_DIM), lw['q_norm_gain'])
    k_na = rms_norm(k_na.reshape(B, L, NA_HEADS, NA_HEAD_DIM), lw['k_norm_gain'])
    v_na = v_na.reshape(B, L, NA_HEADS, NA_HEAD_DIM)
    y_na = na_fn(q_na, k_na, v_na)
    y_la, s_f, s_b = gla_mixer(q_la, k_la, v_la, g_la, lr_f, lr_b, lw['w_gate_f'], lw['b_gate_f'],
                               lw['w_gate_b'], lw['b_gate_b'], lw['gla_norm_gain'], s0_f, s0_b, rope)
    x = x + gate1 * (jnp.concatenate([y_pool, y_na, y_la], axis=-1) @ lw['w_out'])
    h2 = rms_norm(x, lw['norm2_gain']) * (1 + scale2) + shift2
    x = x + gate2 * swiglu(h2, lw['w_ffn_in'], lw['w_ffn_out'])
    return x, k_na.transpose(0, 2, 1, 3), v_na.transpose(0, 2, 1, 3), s_f, s_b


def setup_inputs(seed: int = 0) -> dict:
    key = jax.random.key(seed)
    ks = jax.random.split(key, 26)
    D = D_MODEL

    def nrm(k, shape, s):
        return jax.random.normal(k, shape, jnp.float32) * s

    return {
        'x_prompt': nrm(ks[0], (BATCH, SEQ, D), 1.0),
        'x_sample': nrm(ks[1], (DEC_BATCH, DEC_SEQ, D), 1.0),
        'c': nrm(ks[2], (DEC_BATCH, D), 1.0),
        'cache_na_k': nrm(ks[3], (DEC_BATCH, DEPTH, NA_HEADS, PAST_LEN, NA_HEAD_DIM), 1.0),
        'cache_na_v': nrm(ks[4], (DEC_BATCH, DEPTH, NA_HEADS, PAST_LEN, NA_HEAD_DIM), 1.0),
        'state_gla_fwd': nrm(ks[5], (DEC_BATCH, DEPTH, GLA_HEADS, GLA_DK, GLA_DV), 2.0),
        'state_gla_bwd': nrm(ks[6], (DEC_BATCH, DEPTH, GLA_HEADS, GLA_DK, GLA_DV), 2.0),
        'c_ctx': nrm(ks[7], (D,), 1.0),
        'w_ada': nrm(ks[8], (DEPTH, D, 6 * D), 0.5 * D ** -0.5),
        'b_ada': nrm(ks[9], (DEPTH, 6 * D), 0.02),
        'norm1_gain': 1.0 + nrm(ks[10], (DEPTH, D), 0.02),
        'norm2_gain': 1.0 + nrm(ks[11], (DEPTH, D), 0.02),
        'w_in': nrm(ks[12], (DEPTH, D, IN_WIDTH), D ** -0.5),
        'w_pool': nrm(ks[13], (DEPTH, POOL_GROUPS, POOL_GROUP_DIM, POOL_GROUP_DIM), POOL_GROUP_DIM ** -0.5),
        'pool_scale': 1.0 + nrm(ks[14], (DEPTH, POOL_WIDTH), 0.1),
        'q_norm_gain': 1.0 + nrm(ks[15], (DEPTH, NA_HEAD_DIM), 0.02),
        'k_norm_gain': 1.0 + nrm(ks[16], (DEPTH, NA_HEAD_DIM), 0.02),
        'rel_bias': nrm(ks[17], (DEPTH, NA_HEADS, 2 * NA_WIN_ROWS - 1, 2 * NA_WIN_COLS - 1), 0.5),
        'w_gate_f': nrm(ks[18], (DEPTH, GLA_GATE_RANK, GLA_KEY_WIDTH), GLA_GATE_RANK ** -0.5),
        'b_gate_f': nrm(ks[19], (DEPTH, GLA_KEY_WIDTH), 0.1),
        'w_gate_b': nrm(ks[20], (DEPTH, GLA_GATE_RANK, GLA_KEY_WIDTH), GLA_GATE_RANK ** -0.5),
        'b_gate_b': nrm(ks[21], (DEPTH, GLA_KEY_WIDTH), 0.1),
        'gla_norm_gain': 1.0 + nrm(ks[22], (DEPTH, GLA_DV), 0.02),
        'w_out': nrm(ks[23], (DEPTH, D, D), D ** -0.5),
        'w_ffn_in': nrm(ks[24], (DEPTH, D, 2 * D_FF), D ** -0.5),
        'w_ffn_out': nrm(ks[25], (DEPTH, D_FF, D), D_FF ** -0.5),
    }


def reference(x_prompt, x_sample, c, cache_na_k, cache_na_v, state_gla_fwd, state_gla_bwd, c_ctx,
              w_ada, b_ada, norm1_gain, norm2_gain, w_in, w_pool, pool_scale, q_norm_gain, k_norm_gain,
              rel_bias, w_gate_f, b_gate_f, w_gate_b, b_gate_b, gla_norm_gain, w_out, w_ffn_in, w_ffn_out):
    def layer_weights(l):
        return {'w_ada': w_ada[l], 'b_ada': b_ada[l], 'norm1_gain': norm1_gain[l], 'norm2_gain': norm2_gain[l],
                'w_in': w_in[l], 'w_pool': w_pool[l], 'pool_scale': pool_scale[l],
                'q_norm_gain': q_norm_gain[l], 'k_norm_gain': k_norm_gain[l],
                'w_gate_f': w_gate_f[l], 'b_gate_f': b_gate_f[l], 'w_gate_b': w_gate_b[l], 'b_gate_b': b_gate_b[l],
                'gla_norm_gain': gla_norm_gain[l], 'w_out': w_out[l],
                'w_ffn_in': w_ffn_in[l], 'w_ffn_out': w_ffn_out[l]}

    xp = x_prompt
    zero_state = jnp.zeros((xp.shape[0], GLA_HEADS, GLA_DK, GLA_DV), jnp.float32)
    ks, vs, sfs, sbs = [], [], [], []
    for l in range(DEPTH):
        xp, k_l, v_l, sf_l, sb_l = trunk_layer(xp, c_ctx[None, :], layer_weights(l), na_context_attn,
                                               zero_state, zero_state, None)
        ks.append(k_l)
        vs.append(v_l)
        sfs.append(sf_l)
        sbs.append(sb_l)
    new_na_k = jnp.stack(ks, axis=1)
    new_na_v = jnp.stack(vs, axis=1)
    new_gla_fwd = jnp.stack(sfs, axis=1)
    new_gla_bwd = jnp.stack(sbs, axis=1)

    xs = x_sample
    rope = axial_rope_tables(xs.shape[1])
    for l in range(DEPTH):
        na_fn = functools.partial(na_latent_attn, k_ctx=cache_na_k[:, l], v_ctx=cache_na_v[:, l], rel_bias=rel_bias[l])
        xs = trunk_layer(xs, c, layer_weights(l), na_fn, state_gla_fwd[:, l], state_gla_bwd[:, l], rope)[0]

    return (xp, xs, new_na_k, new_na_v, new_gla_fwd, new_gla_bwd)
```

```python
import functools

import numpy as np
import jax
import jax.numpy as jnp
from jax import lax
from jax.experimental import pallas as pl
from jax.experimental.pallas import tpu as pltpu

F32 = jnp.float32
BF16 = jnp.bfloat16

D_MODEL = 1024
DEPTH = 2
GRID_W = 64
POOL_WIDTH = 256
POOL_GROUP_DIM = 64
POOL_WINDOWS = (2, 4, 8, 16)
POOL_HALO = max(POOL_WINDOWS) // 2
NA_HEADS = 8
NA_HEAD_DIM = 64
NA_WIDTH = NA_HEADS * NA_HEAD_DIM
NA_WIN_ROWS = 8
NA_WIN_COLS = 16
GLA_HEADS = 4
GLA_DV = 64
GLA_DK = 32
GLA_WIDTH = GLA_HEADS * GLA_DV
GLA_KEY_WIDTH = GLA_HEADS * GLA_DK
GLA_GATE_RANK = 16
GLA_GATE_TAU = 16.0
GLA_BLOCK = 16
ROPE_THETA = 10000.0
D_FF = 2816
NORM_EPS = 1e-6
N_MODS = 6
COND_ROWS = 16
MASK_VALUE = -1e30

OFF_POOL = 0
OFF_NA_Q = OFF_POOL + POOL_WIDTH
OFF_NA_K = OFF_NA_Q + NA_WIDTH
OFF_NA_V = OFF_NA_K + NA_WIDTH
OFF_LA_Q = OFF_NA_V + NA_WIDTH
OFF_LA_K = OFF_LA_Q + GLA_KEY_WIDTH
OFF_LA_V = OFF_LA_K + GLA_KEY_WIDTH
OFF_LA_G = OFF_LA_V + GLA_WIDTH
OFF_LR = OFF_LA_G + GLA_WIDTH
IN_WIDTH = OFF_LR + 2 * GLA_GATE_RANK

LANES = 128
ROW_TILE = 256
FF_CHUNK = 256
ADA_COL_TILE = 512
VMEM_LIMIT = 56 * 1024 * 1024


def _dot(a, b):
    return jnp.dot(a, b, preferred_element_type=F32)


def _dot_nt(a, b):
    return lax.dot_general(a, b, (((1,), (1,)), ((), ())), preferred_element_type=F32)


def _dot_tn(a, b):
    return lax.dot_general(a, b, (((0,), (0,)), ((), ())), preferred_element_type=F32)


def _silu(x):
    return x * (1.0 / (1.0 + jnp.exp(-x)))


def _split3(x):
    hi = x.astype(BF16)
    r1 = x - hi.astype(F32)
    mid = r1.astype(BF16)
    lo = (r1 - mid.astype(F32)).astype(BF16)
    return hi, mid, lo


def _half_lane_rms(x, eps):
    lane = lax.broadcasted_iota(jnp.int32, (1, LANES), 1)
    low = lane < NA_HEAD_DIM
    cols = []
    for j in range(x.shape[-1] // LANES):
        blk = x[:, j * LANES:(j + 1) * LANES]
        sq = blk * blk
        s_lo = jnp.sum(jnp.where(low, sq, 0.0), axis=-1, keepdims=True)
        s_hi = jnp.sum(jnp.where(low, 0.0, sq), axis=-1, keepdims=True)
        r_lo = lax.rsqrt(s_lo * (1.0 / NA_HEAD_DIM) + eps)
        r_hi = lax.rsqrt(s_hi * (1.0 / NA_HEAD_DIM) + eps)
        cols.append(blk * jnp.where(low, r_lo, r_hi))
    return jnp.concatenate(cols, axis=-1)


def _row_rms(x, eps):
    return x * lax.rsqrt(jnp.mean(x * x, axis=-1, keepdims=True) + eps)


def _ada_kernel(c_ref, w_ref, b_ref, o_ref):
    s_hi, s_mid, _ = _split3(_silu(c_ref[...]))
    w = w_ref[0]
    w_hi, w_mid, _ = _split3(w)
    acc = _dot(s_hi, w_hi) + _dot(s_mid, w_hi) + _dot(s_hi, w_mid)
    o_ref[0] = acc + b_ref[0]


def _ada(conds, w_ada, b_ada):
    n_out = w_ada.shape[-1]
    return pl.pallas_call(
        _ada_kernel,
        grid=(DEPTH, n_out // ADA_COL_TILE),
        in_specs=[
            pl.BlockSpec((COND_ROWS, D_MODEL), lambda l, j: (0, 0)),
            pl.BlockSpec((1, D_MODEL, ADA_COL_TILE), lambda l, j: (l, 0, j)),
            pl.BlockSpec((1, 1, ADA_COL_TILE), lambda l, j: (l, 0, j)),
        ],
        out_specs=pl.BlockSpec((1, COND_ROWS, ADA_COL_TILE), lambda l, j: (l, 0, j)),
        out_shape=jax.ShapeDtypeStruct((DEPTH, COND_ROWS, n_out), F32),
        name="ada",
    )(conds, w_ada, b_ada.reshape(DEPTH, 1, n_out))


def _inproj_kernel(*refs, rope, kv_f32):
    x_ref, m_ref, g1_ref, w_ref, qg_ref, kg_ref, wg_ref, bg_ref = refs[:8]
    pos = 8
    if rope:
        cos_ref, sin_ref = refs[pos:pos + 2]
        pos += 2
    u_ref, q_ref, k_ref, v_ref, lq_ref, lk_ref, lv_ref, lg_ref, la_ref = refs[pos:pos + 9]
    pos += 9
    if kv_f32:
        kf_ref, vf_ref = refs[pos:pos + 2]

    x = x_ref[0]
    shift1 = m_ref[0, 0:1, :]
    scale1 = m_ref[0, 1:2, :]
    h = _row_rms(x, NORM_EPS) * g1_ref[...] * (1.0 + scale1) + shift1
    hb = h.astype(BF16)

    def proj(off, width):
        return _dot(hb, w_ref[:, off:off + width])

    u_ref[0] = proj(OFF_POOL, POOL_WIDTH)

    qn = _half_lane_rms(proj(OFF_NA_Q, NA_WIDTH), NORM_EPS) * qg_ref[...]
    q_ref[0] = (qn * (NA_HEAD_DIM ** -0.5)).astype(BF16)
    kn = _half_lane_rms(proj(OFF_NA_K, NA_WIDTH), NORM_EPS) * kg_ref[...]
    k_ref[0] = kn.astype(BF16)
    vn = proj(OFF_NA_V, NA_WIDTH)
    v_ref[0] = vn.astype(BF16)
    if kv_f32:
        kf_ref[0] = kn
        vf_ref[0] = vn

    lq = proj(OFF_LA_Q, GLA_KEY_WIDTH) * (GLA_DK ** -0.5)
    lk = proj(OFF_LA_K, GLA_KEY_WIDTH)
    if rope:
        lane = lax.broadcasted_iota(jnp.int32, (1, LANES), 1)
        first = (lane % 16) < 8
        cos = cos_ref[...]
        sin = sin_ref[...]

        def rot(t):
            partner = jnp.where(first, pltpu.roll(t, LANES - 8, axis=1), pltpu.roll(t, 8, axis=1))
            return t * cos + partner * sin

        lq = rot(lq)
        lk = rot(lk)
    lq_ref[0] = lq
    lk_ref[0] = lk
    lv_ref[0] = proj(OFF_LA_V, GLA_WIDTH)
    lg_ref[0] = proj(OFF_LA_G, GLA_WIDTH)

    lr = proj(OFF_LR, 2 * GLA_GATE_RANK)
    z = _dot(lr.astype(BF16), wg_ref[...]) + bg_ref[...]
    log_sig = jnp.minimum(z, 0.0) - jnp.log1p(jnp.exp(-jnp.abs(z)))
    la_ref[0] = log_sig * (1.0 / GLA_GATE_TAU)


def _in_proj(x, mods_l, row_of_batch, lw, rope_tabs, kv_f32):
    B, L, _ = x.shape
    tm = min(ROW_TILE, L)
    rope = rope_tabs is not None
    const = lambda b, i: (0, 0)
    tile = lambda b, i: (b, i, 0)

    in_specs = [
        pl.BlockSpec((1, tm, D_MODEL), tile),
        pl.BlockSpec((1, N_MODS, D_MODEL), lambda b, i: (row_of_batch(b), 0, 0)),
        pl.BlockSpec((1, D_MODEL), const),
        pl.BlockSpec((D_MODEL, IN_WIDTH), const),
        pl.BlockSpec((1, NA_WIDTH), const),
        pl.BlockSpec((1, NA_WIDTH), const),
        pl.BlockSpec((2 * GLA_GATE_RANK, 2 * GLA_KEY_WIDTH), const),
        pl.BlockSpec((1, 2 * GLA_KEY_WIDTH), const),
    ]
    args = [x, mods_l, lw["norm1_gain"], lw["w_in"], lw["q_gain"], lw["k_gain"], lw["w_gate"], lw["b_gate"]]
    if rope:
        in_specs += [pl.BlockSpec((tm, GLA_KEY_WIDTH), lambda b, i: (i, 0))] * 2
        args += list(rope_tabs)

    widths = [(POOL_WIDTH, F32), (NA_WIDTH, BF16), (NA_WIDTH, BF16), (NA_WIDTH, BF16),
              (GLA_KEY_WIDTH, F32), (GLA_KEY_WIDTH, F32), (GLA_WIDTH, F32), (GLA_WIDTH, F32),
              (2 * GLA_KEY_WIDTH, F32)]
    if kv_f32:
        widths += [(NA_WIDTH, F32), (NA_WIDTH, F32)]
    out_specs = [pl.BlockSpec((1, tm, w), tile) for w, _ in widths]
    out_shape = [jax.ShapeDtypeStruct((B, L, w), dt) for w, dt in widths]

    return pl.pallas_call(
        functools.partial(_inproj_kernel, rope=rope, kv_f32=kv_f32),
        grid=(B, L // tm),
        in_specs=in_specs,
        out_specs=out_specs,
        out_shape=out_shape,
        compiler_params=pltpu.CompilerParams(vmem_limit_bytes=VMEM_LIMIT),
        name="in_proj",
    )(*args)


def _pool_kernel(u_ref, w_ref, sc_ref, o_ref, pad_ref, *, L, tp):
    H = POOL_HALO
    zeros = jnp.zeros((H, POOL_WIDTH), F32)
    pad_ref[0:H, :] = zeros
    pad_ref[H + L:H + L + H, :] = zeros
    pad_ref[H:H + L, :] = u_ref[0]
    lane = lax.broadcasted_iota(jnp.int32, (1, LANES), 1)
    low = lane < POOL_GROUP_DIM

    for base in range(0, L, tp):
        t = base + lax.broadcasted_iota(jnp.int32, (tp, 1), 0)

        def count(win):
            return (jnp.minimum(t + win // 2, L) - jnp.maximum(t - win // 2, 0)).astype(F32)

        def shifted(off, col):
            return pad_ref[H + base + off:H + base + off + tp, col * LANES:(col + 1) * LANES]

        u_a = shifted(0, 0)
        w2 = shifted(-1, 0) + u_a
        w4 = w2 + shifted(-2, 0) + shifted(1, 0)
        mean_a = jnp.where(low, w2 / count(2), w4 / count(4))
        u_b = shifted(0, 1)
        w8 = u_b
        for off in (-4, -3, -2, -1, 1, 2, 3):
            w8 = w8 + shifted(off, 1)
        w16 = w8
        for off in (-8, -7, -6, -5, 4, 5, 6, 7):
            w16 = w16 + shifted(off, 1)
        mean_b = jnp.where(low, w8 / count(8), w16 / count(16))
        d = jnp.concatenate([mean_a - u_a, mean_b - u_b], axis=-1).astype(BF16)
        y = _dot(d, w_ref[...]) * sc_ref[...]
        o_ref[0, base:base + tp, :] = y.astype(BF16)


def _pool(u, lw):
    B, L, _ = u.shape
    tp = min(512, L)
    return pl.pallas_call(
        functools.partial(_pool_kernel, L=L, tp=tp),
        grid=(B,),
        in_specs=[
            pl.BlockSpec((1, L, POOL_WIDTH), lambda b: (b, 0, 0)),
            pl.BlockSpec((POOL_WIDTH, POOL_WIDTH), lambda b: (0, 0)),
            pl.BlockSpec((1, POOL_WIDTH), lambda b: (0, 0)),
        ],
        out_specs=pl.BlockSpec((1, L, POOL_WIDTH), lambda b: (b, 0, 0)),
        out_shape=jax.ShapeDtypeStruct((B, L, POOL_WIDTH), BF16),
        scratch_shapes=[pltpu.VMEM((L + 2 * POOL_HALO, POOL_WIDTH), F32)],
        compiler_params=pltpu.CompilerParams(vmem_limit_bytes=VMEM_LIMIT),
        name="pool",
    )(u, lw["w_pool_bd"], lw["pool_scale"])


def _ctx_attn_kernel(q_ref, k_ref, v_ref, o_ref):
    q = q_ref[0]
    k = k_ref[0]
    v = v_ref[0]
    outs = []
    for h in range(NA_HEADS):
        sl = slice(h * NA_HEAD_DIM, (h + 1) * NA_HEAD_DIM)
        s = _dot_nt(q[:, sl], k[:, sl])
        p = jnp.exp(s - jnp.max(s, axis=-1, keepdims=True))
        denom = jnp.sum(p, axis=-1, keepdims=True)
        outs.append(_dot(p.astype(BF16), v[:, sl]) / denom)
    o_ref[0] = jnp.concatenate(outs, axis=-1).astype(BF16)


def _ctx_attn(q, k, v):
    B, L, _ = q.shape
    spec = pl.BlockSpec((1, L, NA_WIDTH), lambda b: (b, 0, 0))
    return pl.pallas_call(
        _ctx_attn_kernel,
        grid=(B,),
        in_specs=[spec, spec, spec],
        out_specs=spec,
        out_shape=jax.ShapeDtypeStruct((B, L, NA_WIDTH), BF16),
        name="ctx_attn",
    )(q, k, v)


def _band_start(r, n_rows):
    return jnp.clip(r - NA_WIN_ROWS // 2, 0, n_rows - NA_WIN_ROWS)


def _lat_attn_kernel(q_ref, k_ref, v_ref, kc_ref, vc_ref, bias_ref, o_ref, *, n_rows):
    r = pl.program_id(1)
    start = pl.multiple_of(_band_start(r, n_rows) * GRID_W, GRID_W)
    band = NA_WIN_ROWS * GRID_W
    q = q_ref[0]
    kb = k_ref[0, pl.ds(start, band), :]
    vb = v_ref[0, pl.ds(start, band), :]
    kc = kc_ref[0]
    vc = vc_ref[0]
    outs = []
    for h in range(NA_HEADS):
        sl = slice(h * NA_HEAD_DIM, (h + 1) * NA_HEAD_DIM)
        qh = q[:, sl]
        s_loc = _dot_nt(qh, kb[:, sl]) + bias_ref[0, h]
        s_ctx = _dot_nt(qh, kc[:, sl])
        m = jnp.maximum(jnp.max(s_loc, axis=-1, keepdims=True), jnp.max(s_ctx, axis=-1, keepdims=True))
        p_loc = jnp.exp(s_loc - m)
        p_ctx = jnp.exp(s_ctx - m)
        denom = jnp.sum(p_loc, axis=-1, keepdims=True) + jnp.sum(p_ctx, axis=-1, keepdims=True)
        o = _dot(p_loc.astype(BF16), vb[:, sl]) + _dot(p_ctx.astype(BF16), vc[:, sl])
        outs.append(o / denom)
    o_ref[0] = jnp.concatenate(outs, axis=-1).astype(BF16)


def _lat_attn(q, k, v, kc, vc, bias_tab):
    B, L, _ = q.shape
    n_rows = L // GRID_W
    past = kc.shape[1]
    band = NA_WIN_ROWS * GRID_W
    whole = pl.BlockSpec((1, L, NA_WIDTH), lambda b, r: (b, 0, 0))
    row = pl.BlockSpec((1, GRID_W, NA_WIDTH), lambda b, r: (b, r, 0))
    ctx = pl.BlockSpec((1, past, NA_WIDTH), lambda b, r: (b, 0, 0))
    bias = pl.BlockSpec((1, NA_HEADS, GRID_W, band), lambda b, r: (r - _band_start(r, n_rows), 0, 0, 0))
    return pl.pallas_call(
        functools.partial(_lat_attn_kernel, n_rows=n_rows),
        grid=(B, n_rows),
        in_specs=[row, whole, whole, ctx, ctx, bias],
        out_specs=row,
        out_shape=jax.ShapeDtypeStruct((B, L, NA_WIDTH), BF16),
        compiler_params=pltpu.CompilerParams(vmem_limit_bytes=VMEM_LIMIT),
        name="lat_attn",
    )(q, k, v, kc, vc, bias_tab)


def _bias_table(rel_bias_l, n_rows):
    n_off = min(NA_WIN_ROWS, n_rows)
    d = np.arange(n_off)[:, None, None, None]
    w = np.arange(GRID_W)[None, :, None, None]
    i = np.arange(NA_WIN_ROWS)[None, None, :, None]
    c = np.arange(GRID_W)[None, None, None, :]
    shape = (n_off, GRID_W, NA_WIN_ROWS, GRID_W)
    dr = np.broadcast_to(i - d + NA_WIN_ROWS - 1, shape)
    cs = np.clip(w - NA_WIN_COLS // 2, 0, GRID_W - NA_WIN_COLS)
    valid = np.broadcast_to((c >= cs) & (c < cs + NA_WIN_COLS), shape)
    dc = np.broadcast_to(np.clip(c - w + NA_WIN_COLS - 1, 0, 2 * NA_WIN_COLS - 2), shape)
    tab = rel_bias_l[:, dr, dc]
    tab = jnp.where(valid[None], tab, MASK_VALUE)
    return jnp.moveaxis(tab, 0, 1).reshape(n_off, NA_HEADS, GRID_W, NA_WIN_ROWS * GRID_W).astype(F32)


def _gla_kernel(q_ref, k_ref, v_ref, la_ref, s0_ref, e_ref, o_ref, sfin_ref, st_ref, b_ref, *, tm, reverse):
    n = pl.program_id(1)
    n_blocks = tm // GLA_BLOCK

    @pl.when(n == 0)
    def _():
        st_ref[...] = s0_ref[0]

    ti = lax.broadcasted_iota(jnp.int32, (tm, tm), 0)
    si = lax.broadcasted_iota(jnp.int32, (tm, tm), 1)
    same = (ti // GLA_BLOCK) == (si // GLA_BLOCK)
    order = (si >= ti) if reverse else (si <= ti)
    tri = jnp.where(same & order, 1.0, 0.0).astype(BF16)
    g_hi, g_mid, g_lo = _split3(la_ref[0])
    b_ref[...] = _dot(tri, g_hi) + _dot(tri, g_mid) + _dot(tri, g_lo)

    expand = e_ref[...]
    head_v = lax.broadcasted_iota(jnp.int32, (GLA_WIDTH, GLA_KEY_WIDTH), 0) // GLA_DV
    head_k = lax.broadcasted_iota(jnp.int32, (GLA_WIDTH, GLA_KEY_WIDTH), 1) // GLA_DK
    head_mask = jnp.where(head_v == head_k, 1.0, 0.0)
    t_in = lax.broadcasted_iota(jnp.int32, (GLA_BLOCK, 1), 0)
    end_row = 0 if reverse else GLA_BLOCK - 1

    def block(j, carry):
        jb = (n_blocks - 1 - j) if reverse else j
        r0 = pl.multiple_of(jb * GLA_BLOCK, GLA_BLOCK)
        rows = pl.ds(r0, GLA_BLOCK)
        qb = q_ref[0, rows, :]
        kb = k_ref[0, rows, :]
        vb = v_ref[0, rows, :]
        bb = b_ref[rows, :]
        b_end = bb[end_row:end_row + 1, :]
        st = st_ref[...]

        o_inter = _dot_nt((qb * jnp.exp(bb)).astype(BF16), st.astype(BF16))

        parts = []
        for s in range(GLA_BLOCK):
            seen = (t_in <= s) if reverse else (t_in >= s)
            decay = jnp.where(seen, jnp.exp(bb - bb[s:s + 1, :]), 0.0)
            parts.append((qb * kb[s:s + 1, :] * decay).astype(BF16))
        a_exp = _dot(jnp.concatenate(parts, axis=0), expand)
        o_intra = jnp.zeros((GLA_BLOCK, GLA_WIDTH), F32)
        for s in range(GLA_BLOCK):
            o_intra = o_intra + a_exp[s * GLA_BLOCK:(s + 1) * GLA_BLOCK, :] * vb[s:s + 1, :]
        o_ref[0, rows, :] = o_inter + o_intra

        k_hat = (kb * jnp.exp(b_end - bb)).astype(BF16)
        upd = _dot_tn(vb.astype(BF16), k_hat)
        st_ref[...] = st * jnp.exp(b_end) + upd * head_mask
        return carry

    lax.fori_loop(0, n_blocks, block, 0)

    @pl.when(n == pl.num_programs(1) - 1)
    def _():
        sfin_ref[0] = st_ref[...]


def _gla_scan(q, k, v, la, s0_t, expand, reverse):
    B, L, _ = q.shape
    tm = min(ROW_TILE, L)
    n_tiles = L // tm
    col = 1 if reverse else 0
    tmap = (lambda b, n: (b, n_tiles - 1 - n, 0)) if reverse else (lambda b, n: (b, n, 0))
    lmap = (lambda b, n: (b, n_tiles - 1 - n, col)) if reverse else (lambda b, n: (b, n, col))
    state = pl.BlockSpec((1, GLA_WIDTH, GLA_KEY_WIDTH), lambda b, n: (b, 0, 0))
    return pl.pallas_call(
        functools.partial(_gla_kernel, tm=tm, reverse=reverse),
        grid=(B, n_tiles),
        in_specs=[
            pl.BlockSpec((1, tm, GLA_KEY_WIDTH), tmap),
            pl.BlockSpec((1, tm, GLA_KEY_WIDTH), tmap),
            pl.BlockSpec((1, tm, GLA_WIDTH), tmap),
            pl.BlockSpec((1, tm, GLA_KEY_WIDTH), lmap),
            state,
            pl.BlockSpec((GLA_KEY_WIDTH, GLA_WIDTH), lambda b, n: (0, 0)),
        ],
        out_specs=[pl.BlockSpec((1, tm, GLA_WIDTH), tmap), state],
        out_shape=[jax.ShapeDtypeStruct((B, L, GLA_WIDTH), F32),
                   jax.ShapeDtypeStruct((B, GLA_WIDTH, GLA_KEY_WIDTH), F32)],
        scratch_shapes=[pltpu.VMEM((GLA_WIDTH, GLA_KEY_WIDTH), F32), pltpu.VMEM((tm, GLA_KEY_WIDTH), F32)],
        compiler_params=pltpu.CompilerParams(vmem_limit_bytes=VMEM_LIMIT),
        name="gla_bwd" if reverse else "gla_fwd",
    )(q, k, v, la, s0_t, expand)


def _state_to_kernel(s):
    B = s.shape[0]
    eye = jnp.eye(GLA_HEADS, dtype=s.dtype)
    st = jnp.einsum("bhde,hg->bhegd", s, eye)
    return st.reshape(B, GLA_WIDTH, GLA_KEY_WIDTH)


def _state_from_kernel(st):
    B = st.shape[0]
    blocks = st.reshape(B, GLA_HEADS, GLA_DV, GLA_HEADS, GLA_DK)
    diag = jnp.stack([blocks[:, h, :, h, :] for h in range(GLA_HEADS)], axis=1)
    return diag.transpose(0, 1, 3, 2)


def _post_kernel(x_ref, m_ref, yp_ref, yn_ref, of_ref, ob_ref, lg_ref, gg_ref, g2_ref,
                 wo_ref, wi_ref, wf_ref, o_ref, acc_ref):
    gate1 = m_ref[0, 2:3, :]
    shift2 = m_ref[0, 3:4, :]
    scale2 = m_ref[0, 4:5, :]
    gate2 = m_ref[0, 5:6, :]

    o_la = _half_lane_rms(of_ref[0] + ob_ref[0], NORM_EPS) * gg_ref[...]
    y_la = (o_la * _silu(lg_ref[0])).astype(BF16)
    mixed = (_dot(yp_ref[0], wo_ref[0:POOL_WIDTH, :])
             + _dot(yn_ref[0], wo_ref[POOL_WIDTH:POOL_WIDTH + NA_WIDTH, :])
             + _dot(y_la, wo_ref[POOL_WIDTH + NA_WIDTH:D_MODEL, :]))
    x1 = x_ref[0] + gate1 * mixed

    h2 = (_row_rms(x1, NORM_EPS) * g2_ref[...] * (1.0 + scale2) + shift2).astype(BF16)
    for c0 in range(0, D_FF, FF_CHUNK):
        hg = _dot(h2, wi_ref[:, c0:c0 + FF_CHUNK])
        hu = _dot(h2, wi_ref[:, D_FF + c0:D_FF + c0 + FF_CHUNK])
        part = _dot((_silu(hg) * hu).astype(BF16), wf_ref[c0:c0 + FF_CHUNK, :])
        if c0 == 0:
            acc_ref[...] = part
        else:
            acc_ref[...] += part
    o_ref[0] = x1 + gate2 * acc_ref[...]


def _post(x, mods_l, row_of_batch, y_pool, y_na, o_f, o_b, lg, lw):
    B, L, _ = x.shape
    tm = min(ROW_TILE, L)
    const = lambda b, i: (0, 0)
    tile = lambda b, i: (b, i, 0)

    def resident(shape):
        return pl.BlockSpec(shape, const, pipeline_mode=pl.Buffered(1))

    return pl.pallas_call(
        _post_kernel,
        grid=(B, L // tm),
        in_specs=[
            pl.BlockSpec((1, tm, D_MODEL), tile),
            pl.BlockSpec((1, N_MODS, D_MODEL), lambda b, i: (row_of_batch(b), 0, 0)),
            pl.BlockSpec((1, tm, POOL_WIDTH), tile),
            pl.BlockSpec((1, tm, NA_WIDTH), tile),
            pl.BlockSpec((1, tm, GLA_WIDTH), tile),
            pl.BlockSpec((1, tm, GLA_WIDTH), tile),
            pl.BlockSpec((1, tm, GLA_WIDTH), tile),
            pl.BlockSpec((1, GLA_WIDTH), const),
            pl.BlockSpec((1, D_MODEL), const),
            resident((D_MODEL, D_MODEL)),
            resident((D_MODEL, 2 * D_FF)),
            resident((D_FF, D_MODEL)),
        ],
        out_specs=pl.BlockSpec((1, tm, D_MODEL), tile),
        out_shape=jax.ShapeDtypeStruct((B, L, D_MODEL), F32),
        scratch_shapes=[pltpu.VMEM((tm, D_MODEL), F32)],
        compiler_params=pltpu.CompilerParams(vmem_limit_bytes=VMEM_LIMIT),
        name="post",
    )(x, mods_l, y_pool, y_na, o_f, o_b, lg, lw["gla_gain"], lw["norm2_gain"],
      lw["w_out"], lw["w_ffn_in"], lw["w_ffn_out"])


def _rope_tables(L):
    t = jnp.arange(L)
    row = (t // GRID_W).astype(F32)
    col = (t % GRID_W).astype(F32)
    half = GLA_DK // 2
    inv_freq = ROPE_THETA ** (-jnp.arange(0, half, 2, dtype=F32) / half)
    ang_r = row[:, None] * inv_freq
    ang_c = col[:, None] * inv_freq
    sign = jnp.concatenate([-jnp.ones((half // 2,), F32), jnp.ones((half // 2,), F32)])

    def lanes(fn, signed):
        per_axis = []
        for ang in (ang_r, ang_c):
            v = jnp.concatenate([fn(ang), fn(ang)], axis=-1)
            per_axis.append(v * sign if signed else v)
        return jnp.tile(jnp.concatenate(per_axis, axis=-1), (1, GLA_HEADS))

    return lanes(jnp.cos, False), lanes(jnp.sin, True)


def _layer_weights(l, w):
    tile_heads = lambda g, n: jnp.tile(g, n)[None, :]
    return {
        "norm1_gain": w["norm1_gain"][l][None, :],
        "norm2_gain": w["norm2_gain"][l][None, :],
        "w_in": w["w_in"][l].astype(BF16),
        "q_gain": tile_heads(w["q_norm_gain"][l], NA_HEADS),
        "k_gain": tile_heads(w["k_norm_gain"][l], NA_HEADS),
        "w_gate": jax.scipy.linalg.block_diag(w["w_gate_f"][l], w["w_gate_b"][l]).astype(BF16),
        "b_gate": jnp.concatenate([w["b_gate_f"][l], w["b_gate_b"][l]])[None, :],
        "w_pool_bd": jax.scipy.linalg.block_diag(*[w["w_pool"][l, g] for g in range(len(POOL_WINDOWS))]).astype(BF16),
        "pool_scale": w["pool_scale"][l][None, :],
        "gla_gain": tile_heads(w["gla_norm_gain"][l], GLA_HEADS),
        "w_out": w["w_out"][l].astype(BF16),
        "w_ffn_in": w["w_ffn_in"][l].astype(BF16),
        "w_ffn_out": w["w_ffn_out"][l].astype(BF16),
    }


def _trunk_layer(x, mods_l, row_of_batch, lw, expand, s0_f, s0_b, latent):
    is_ctx = latent is None
    outs = _in_proj(x, mods_l, row_of_batch, lw, None if is_ctx else latent[0], kv_f32=is_ctx)
    u, q, k, v, lq, lk, lv, lg, la = outs[:9]
    y_pool = _pool(u, lw)
    if is_ctx:
        y_na = _ctx_attn(q, k, v)
    else:
        y_na = _lat_attn(q, k, v, latent[1], latent[2], latent[3])
    o_f, s_f = _gla_scan(lq, lk, lv, la, s0_f, expand, reverse=False)
    o_b, s_b = _gla_scan(lq, lk, lv, la, s0_b, expand, reverse=True)
    x = _post(x, mods_l, row_of_batch, y_pool, y_na, o_f, o_b, lg, lw)
    extras = (outs[9], outs[10]) if is_ctx else ()
    return x, s_f, s_b, extras


def kernel(x_prompt, x_sample, c, cache_na_k, cache_na_v, state_gla_fwd, state_gla_bwd, c_ctx, w_ada, b_ada, norm1_gain, norm2_gain, w_in, w_pool, pool_scale, q_norm_gain, k_norm_gain, rel_bias, w_gate_f, b_gate_f, w_gate_b, b_gate_b, gla_norm_gain, w_out, w_ffn_in, w_ffn_out):
    weights = dict(norm1_gain=norm1_gain, norm2_gain=norm2_gain, w_in=w_in, w_pool=w_pool, pool_scale=pool_scale,
                   q_norm_gain=q_norm_gain, k_norm_gain=k_norm_gain, w_gate_f=w_gate_f, b_gate_f=b_gate_f,
                   w_gate_b=w_gate_b, b_gate_b=b_gate_b, gla_norm_gain=gla_norm_gain, w_out=w_out,
                   w_ffn_in=w_ffn_in, w_ffn_out=w_ffn_out)
    B_ctx, L_ctx, _ = x_prompt.shape
    B_lat, L_lat, _ = x_sample.shape
    assert 1 + B_lat <= COND_ROWS

    conds = jnp.zeros((COND_ROWS, D_MODEL), F32).at[0].set(c_ctx).at[1:1 + B_lat].set(c)
    mods = _ada(conds, w_ada, b_ada).reshape(DEPTH, COND_ROWS, N_MODS, D_MODEL)
    lws = [_layer_weights(l, weights) for l in range(DEPTH)]
    head_of_k = np.arange(GLA_KEY_WIDTH) // GLA_DK
    head_of_v = np.arange(GLA_WIDTH) // GLA_DV
    expand = jnp.asarray(head_of_k[:, None] == head_of_v[None, :], BF16)

    xp = x_prompt
    zero_state = jnp.zeros((B_ctx, GLA_WIDTH, GLA_KEY_WIDTH), F32)
    ks, vs, sfs, sbs = [], [], [], []
    for l in range(DEPTH):
        xp, s_f, s_b, (k_f32, v_f32) = _trunk_layer(xp, mods[l], lambda b: 0, lws[l], expand,
                                                     zero_state, zero_state, None)
        to_heads = lambda a: a.reshape(B_ctx, L_ctx, NA_HEADS, NA_HEAD_DIM).transpose(0, 2, 1, 3)
        ks.append(to_heads(k_f32))
        vs.append(to_heads(v_f32))
        sfs.append(_state_from_kernel(s_f))
        sbs.append(_state_from_kernel(s_b))

    xs = x_sample
    rope_tabs = _rope_tables(L_lat)
    n_rows = L_lat // GRID_W
    from_heads = lambda a: a.transpose(0, 2, 1, 3).reshape(B_lat, a.shape[2], NA_WIDTH).astype(BF16)
    for l in range(DEPTH):
        latent = (rope_tabs, from_heads(cache_na_k[:, l]), from_heads(cache_na_v[:, l]),
                  _bias_table(rel_bias[l], n_rows))
        xs, _, _, _ = _trunk_layer(xs, mods[l], lambda b: b + 1, lws[l], expand,
                                   _state_to_kernel(state_gla_fwd[:, l]), _state_to_kernel(state_gla_bwd[:, l]),
                                   latent)

    return (xp, xs, jnp.stack(ks, axis=1), jnp.stack(vs, axis=1), jnp.stack(sfs, axis=1), jnp.stack(sbs, axis=1))
```

```python
import functools

import numpy as np
import jax
import jax.numpy as jnp
from jax import lax
from jax.experimental import pallas as pl
from jax.experimental.pallas import tpu as pltpu

F32 = jnp.float32
BF16 = jnp.bfloat16

D_MODEL = 1024
DEPTH = 2
GRID_W = 64
POOL_WIDTH = 256
POOL_GROUP_DIM = 64
POOL_WINDOWS = (2, 4, 8, 16)
POOL_HALO = max(POOL_WINDOWS) // 2
NA_HEADS = 8
NA_HEAD_DIM = 64
NA_WIDTH = NA_HEADS * NA_HEAD_DIM
NA_WIN_ROWS = 8
NA_WIN_COLS = 16
NA_ROWS_PER_STEP = 2
GLA_HEADS = 4
GLA_DV = 64
GLA_DK = 32
GLA_WIDTH = GLA_HEADS * GLA_DV
GLA_KEY_WIDTH = GLA_HEADS * GLA_DK
GLA_GATE_RANK = 16
GLA_GATE_TAU = 16.0
GLA_BLOCK = 16
ROPE_THETA = 10000.0
D_FF = 2816
NORM_EPS = 1e-6
N_MODS = 6
COND_ROWS = 16
MASK_VALUE = -1e30

OFF_POOL = 0
OFF_NA_Q = OFF_POOL + POOL_WIDTH
OFF_NA_K = OFF_NA_Q + NA_WIDTH
OFF_NA_V = OFF_NA_K + NA_WIDTH
OFF_LA_Q = OFF_NA_V + NA_WIDTH
OFF_LA_K = OFF_LA_Q + GLA_KEY_WIDTH
OFF_LA_V = OFF_LA_K + GLA_KEY_WIDTH
OFF_LA_G = OFF_LA_V + GLA_WIDTH
OFF_LR = OFF_LA_G + GLA_WIDTH
IN_WIDTH = OFF_LR + 2 * GLA_GATE_RANK

LANES = 128
ROW_TILE = 256
FF_CHUNK = 256
ADA_COL_TILE = 512
VMEM_LIMIT = 56 * 1024 * 1024


def _dot(a, b):
    return jnp.dot(a, b, preferred_element_type=F32)


def _dot_nt(a, b):
    return lax.dot_general(a, b, (((1,), (1,)), ((), ())), preferred_element_type=F32)


def _dot_tn(a, b):
    return lax.dot_general(a, b, (((0,), (0,)), ((), ())), preferred_element_type=F32)


def _silu(x):
    return x * (1.0 / (1.0 + jnp.exp(-x)))


def _split3(x):
    hi = x.astype(BF16)
    r1 = x - hi.astype(F32)
    mid = r1.astype(BF16)
    lo = (r1 - mid.astype(F32)).astype(BF16)
    return hi, mid, lo


def _half_lane_rms(x, eps):
    lane = lax.broadcasted_iota(jnp.int32, (1, LANES), 1)
    low = lane < NA_HEAD_DIM
    cols = []
    for j in range(x.shape[-1] // LANES):
        blk = x[:, j * LANES:(j + 1) * LANES]
        sq = blk * blk
        s_lo = jnp.sum(jnp.where(low, sq, 0.0), axis=-1, keepdims=True)
        s_hi = jnp.sum(jnp.where(low, 0.0, sq), axis=-1, keepdims=True)
        r_lo = lax.rsqrt(s_lo * (1.0 / NA_HEAD_DIM) + eps)
        r_hi = lax.rsqrt(s_hi * (1.0 / NA_HEAD_DIM) + eps)
        cols.append(blk * jnp.where(low, r_lo, r_hi))
    return jnp.concatenate(cols, axis=-1)


def _row_rms(x, eps):
    return x * lax.rsqrt(jnp.mean(x * x, axis=-1, keepdims=True) + eps)


def _ada_kernel(c_ref, w_ref, b_ref, o_ref):
    s_hi, s_mid, _ = _split3(_silu(c_ref[...]))
    w = w_ref[0]
    w_hi, w_mid, _ = _split3(w)
    acc = _dot(s_hi, w_hi) + _dot(s_mid, w_hi) + _dot(s_hi, w_mid)
    o_ref[0] = acc + b_ref[0]


def _ada(conds, w_ada, b_ada):
    n_out = w_ada.shape[-1]
    return pl.pallas_call(
        _ada_kernel,
        grid=(DEPTH, n_out // ADA_COL_TILE),
        in_specs=[
            pl.BlockSpec((COND_ROWS, D_MODEL), lambda l, j: (0, 0)),
            pl.BlockSpec((1, D_MODEL, ADA_COL_TILE), lambda l, j: (l, 0, j)),
            pl.BlockSpec((1, 1, ADA_COL_TILE), lambda l, j: (l, 0, j)),
        ],
        out_specs=pl.BlockSpec((1, COND_ROWS, ADA_COL_TILE), lambda l, j: (l, 0, j)),
        out_shape=jax.ShapeDtypeStruct((DEPTH, COND_ROWS, n_out), F32),
        name="ada",
    )(conds, w_ada, b_ada.reshape(DEPTH, 1, n_out))


def _inproj_kernel(*refs, rope, kv_f32):
    x_ref, m_ref, g1_ref, w_ref, qg_ref, kg_ref, wg_ref, bg_ref = refs[:8]
    pos = 8
    if rope:
        cos_ref, sin_ref = refs[pos:pos + 2]
        pos += 2
    u_ref, q_ref, k_ref, v_ref, lq_ref, lk_ref, lv_ref, lg_ref, la_ref = refs[pos:pos + 9]
    pos += 9
    if kv_f32:
        kf_ref, vf_ref = refs[pos:pos + 2]

    x = x_ref[0]
    shift1 = m_ref[0, 0:1, :]
    scale1 = m_ref[0, 1:2, :]
    h = _row_rms(x, NORM_EPS) * g1_ref[...] * (1.0 + scale1) + shift1
    hb = h.astype(BF16)

    def proj(off, width):
        return _dot(hb, w_ref[:, off:off + width])

    u_ref[0] = proj(OFF_POOL, POOL_WIDTH)

    qn = _half_lane_rms(proj(OFF_NA_Q, NA_WIDTH), NORM_EPS) * qg_ref[...]
    q_ref[0] = (qn * (NA_HEAD_DIM ** -0.5)).astype(BF16)
    kn = _half_lane_rms(proj(OFF_NA_K, NA_WIDTH), NORM_EPS) * kg_ref[...]
    k_ref[0] = kn.astype(BF16)
    vn = proj(OFF_NA_V, NA_WIDTH)
    v_ref[0] = vn.astype(BF16)
    if kv_f32:
        kf_ref[0] = kn
        vf_ref[0] = vn

    lq = proj(OFF_LA_Q, GLA_KEY_WIDTH) * (GLA_DK ** -0.5)
    lk = proj(OFF_LA_K, GLA_KEY_WIDTH)
    if rope:
        lane = lax.broadcasted_iota(jnp.int32, (1, LANES), 1)
        first = (lane % 16) < 8
        cos = cos_ref[...]
        sin = sin_ref[...]

        def rot(t):
            partner = jnp.where(first, pltpu.roll(t, LANES - 8, axis=1), pltpu.roll(t, 8, axis=1))
            return t * cos + partner * sin

        lq = rot(lq)
        lk = rot(lk)
    lq_ref[0] = lq
    lk_ref[0] = lk
    lv_ref[0] = proj(OFF_LA_V, GLA_WIDTH)
    lg_ref[0] = proj(OFF_LA_G, GLA_WIDTH)

    lr = proj(OFF_LR, 2 * GLA_GATE_RANK)
    z = _dot(lr.astype(BF16), wg_ref[...]) + bg_ref[...]
    log_sig = jnp.minimum(z, 0.0) - jnp.log1p(jnp.exp(-jnp.abs(z)))
    la_ref[0] = log_sig * (1.0 / GLA_GATE_TAU)


def _in_proj(x, mods_l, row_of_batch, lw, rope_tabs, kv_f32):
    B, L, _ = x.shape
    tm = min(ROW_TILE, L)
    rope = rope_tabs is not None
    const = lambda b, i: (0, 0)
    tile = lambda b, i: (b, i, 0)

    in_specs = [
        pl.BlockSpec((1, tm, D_MODEL), tile),
        pl.BlockSpec((1, N_MODS, D_MODEL), lambda b, i: (row_of_batch(b), 0, 0)),
        pl.BlockSpec((1, D_MODEL), const),
        pl.BlockSpec((D_MODEL, IN_WIDTH), const),
        pl.BlockSpec((1, NA_WIDTH), const),
        pl.BlockSpec((1, NA_WIDTH), const),
        pl.BlockSpec((2 * GLA_GATE_RANK, 2 * GLA_KEY_WIDTH), const),
        pl.BlockSpec((1, 2 * GLA_KEY_WIDTH), const),
    ]
    args = [x, mods_l, lw["norm1_gain"], lw["w_in"], lw["q_gain"], lw["k_gain"], lw["w_gate"], lw["b_gate"]]
    if rope:
        in_specs += [pl.BlockSpec((tm, GLA_KEY_WIDTH), lambda b, i: (i, 0))] * 2
        args += list(rope_tabs)

    widths = [(POOL_WIDTH, F32), (NA_WIDTH, BF16), (NA_WIDTH, BF16), (NA_WIDTH, BF16),
              (GLA_KEY_WIDTH, F32), (GLA_KEY_WIDTH, F32), (GLA_WIDTH, F32), (GLA_WIDTH, F32),
              (2 * GLA_KEY_WIDTH, F32)]
    if kv_f32:
        widths += [(NA_WIDTH, F32), (NA_WIDTH, F32)]
    out_specs = [pl.BlockSpec((1, tm, w), tile) for w, _ in widths]
    out_shape = [jax.ShapeDtypeStruct((B, L, w), dt) for w, dt in widths]

    return pl.pallas_call(
        functools.partial(_inproj_kernel, rope=rope, kv_f32=kv_f32),
        grid=(B, L // tm),
        in_specs=in_specs,
        out_specs=out_specs,
        out_shape=out_shape,
        compiler_params=pltpu.CompilerParams(vmem_limit_bytes=VMEM_LIMIT),
        name="in_proj",
    )(*args)


def _pool_kernel(u_ref, w_ref, sc_ref, o_ref, pad_ref, *, L, tp):
    H = POOL_HALO
    zeros = jnp.zeros((H, POOL_WIDTH), F32)
    pad_ref[0:H, :] = zeros
    pad_ref[H + L:H + L + H, :] = zeros
    pad_ref[H:H + L, :] = u_ref[0]
    lane = lax.broadcasted_iota(jnp.int32, (1, LANES), 1)
    low = lane < POOL_GROUP_DIM

    for base in range(0, L, tp):
        t = base + lax.broadcasted_iota(jnp.int32, (tp, 1), 0)

        def count(win):
            return (jnp.minimum(t + win // 2, L) - jnp.maximum(t - win // 2, 0)).astype(F32)

        def shifted(off, col):
            return pad_ref[H + base + off:H + base + off + tp, col * LANES:(col + 1) * LANES]

        u_a = shifted(0, 0)
        w2 = shifted(-1, 0) + u_a
        w4 = w2 + shifted(-2, 0) + shifted(1, 0)
        mean_a = jnp.where(low, w2 / count(2), w4 / count(4))
        u_b = shifted(0, 1)
        w8 = u_b
        for off in (-4, -3, -2, -1, 1, 2, 3):
            w8 = w8 + shifted(off, 1)
        w16 = w8
        for off in (-8, -7, -6, -5, 4, 5, 6, 7):
            w16 = w16 + shifted(off, 1)
        mean_b = jnp.where(low, w8 / count(8), w16 / count(16))
        d = jnp.concatenate([mean_a - u_a, mean_b - u_b], axis=-1).astype(BF16)
        y = _dot(d, w_ref[...]) * sc_ref[...]
        o_ref[0, base:base + tp, :] = y.astype(BF16)


def _pool(u, lw):
    B, L, _ = u.shape
    tp = min(512, L)
    return pl.pallas_call(
        functools.partial(_pool_kernel, L=L, tp=tp),
        grid=(B,),
        in_specs=[
            pl.BlockSpec((1, L, POOL_WIDTH), lambda b: (b, 0, 0)),
            pl.BlockSpec((POOL_WIDTH, POOL_WIDTH), lambda b: (0, 0)),
            pl.BlockSpec((1, POOL_WIDTH), lambda b: (0, 0)),
        ],
        out_specs=pl.BlockSpec((1, L, POOL_WIDTH), lambda b: (b, 0, 0)),
        out_shape=jax.ShapeDtypeStruct((B, L, POOL_WIDTH), BF16),
        scratch_shapes=[pltpu.VMEM((L + 2 * POOL_HALO, POOL_WIDTH), F32)],
        compiler_params=pltpu.CompilerParams(vmem_limit_bytes=VMEM_LIMIT),
        name="pool",
    )(u, lw["w_pool_bd"], lw["pool_scale"])


def _ctx_attn_kernel(q_ref, k_ref, v_ref, o_ref):
    q = q_ref[0]
    k = k_ref[0]
    v = v_ref[0]
    outs = []
    for h in range(NA_HEADS):
        sl = slice(h * NA_HEAD_DIM, (h + 1) * NA_HEAD_DIM)
        s = _dot_nt(q[:, sl], k[:, sl])
        p = jnp.exp(s - jnp.max(s, axis=-1, keepdims=True))
        denom = jnp.sum(p, axis=-1, keepdims=True)
        outs.append(_dot(p.astype(BF16), v[:, sl]) / denom)
    o_ref[0] = jnp.concatenate(outs, axis=-1).astype(BF16)


def _ctx_attn(q, k, v):
    B, L, _ = q.shape
    spec = pl.BlockSpec((1, L, NA_WIDTH), lambda b: (b, 0, 0))
    return pl.pallas_call(
        _ctx_attn_kernel,
        grid=(B,),
        in_specs=[spec, spec, spec],
        out_specs=spec,
        out_shape=jax.ShapeDtypeStruct((B, L, NA_WIDTH), BF16),
        name="ctx_attn",
    )(q, k, v)


def _band_start(r, n_rows):
    return jnp.clip(r - NA_WIN_ROWS // 2, 0, n_rows - NA_WIN_ROWS)


def _lat_attn_kernel(q_ref, k_ref, v_ref, kc_ref, vc_ref, bias_ref, o_ref, *, n_rows):
    band = NA_WIN_ROWS * GRID_W
    lane = lax.broadcasted_iota(jnp.int32, (1, LANES), 1)
    even = lane < NA_HEAD_DIM
    zero = jnp.zeros((), BF16)
    for rr in range(NA_ROWS_PER_STEP):
        r = pl.program_id(1) * NA_ROWS_PER_STEP + rr
        first = _band_start(r, n_rows)
        start = pl.multiple_of(first * GRID_W, GRID_W)
        tile0 = NA_WIN_ROWS - 1 - (r - first)
        q_rows = slice(rr * GRID_W, (rr + 1) * GRID_W)
        for j in range(NA_HEADS // 2):
            cols = slice(j * LANES, (j + 1) * LANES)
            qp = q_ref[0, q_rows, cols]
            q2 = jnp.concatenate([jnp.where(even, qp, zero), jnp.where(even, zero, qp)], axis=0)
            bias = jnp.concatenate(
                [jnp.concatenate([bias_ref[2 * j + hh, tile0 + 2 * ii] for ii in range(NA_WIN_ROWS // 2)], axis=-1)
                 for hh in range(2)], axis=0)
            s_loc = _dot_nt(q2, k_ref[0, pl.ds(start, band), cols]) + bias
            s_ctx = _dot_nt(q2, kc_ref[0, :, cols])
            m = jnp.maximum(jnp.max(s_loc, axis=-1, keepdims=True), jnp.max(s_ctx, axis=-1, keepdims=True))
            p_loc = jnp.exp(s_loc - m)
            p_ctx = jnp.exp(s_ctx - m)
            denom = jnp.sum(p_loc, axis=-1, keepdims=True) + jnp.sum(p_ctx, axis=-1, keepdims=True)
            o2 = (_dot(p_loc.astype(BF16), v_ref[0, pl.ds(start, band), cols])
                  + _dot(p_ctx.astype(BF16), vc_ref[0, :, cols])) / denom
            o_ref[0, q_rows, cols] = jnp.where(even, o2[:GRID_W], o2[GRID_W:]).astype(BF16)


def _lat_attn(q, k, v, kc, vc, bias_tiles):
    B, L, _ = q.shape
    n_rows = L // GRID_W
    assert n_rows >= NA_WIN_ROWS and n_rows % NA_ROWS_PER_STEP == 0
    past = kc.shape[1]
    whole = pl.BlockSpec((1, L, NA_WIDTH), lambda b, r: (b, 0, 0))
    rows = pl.BlockSpec((1, NA_ROWS_PER_STEP * GRID_W, NA_WIDTH), lambda b, r: (b, r, 0))
    ctx = pl.BlockSpec((1, past, NA_WIDTH), lambda b, r: (b, 0, 0))
    bias = pl.BlockSpec(bias_tiles.shape, lambda b, r: (0, 0, 0, 0))
    return pl.pallas_call(
        functools.partial(_lat_attn_kernel, n_rows=n_rows),
        grid=(B, n_rows // NA_ROWS_PER_STEP),
        in_specs=[rows, whole, whole, ctx, ctx, bias],
        out_specs=rows,
        out_shape=jax.ShapeDtypeStruct((B, L, NA_WIDTH), BF16),
        compiler_params=pltpu.CompilerParams(vmem_limit_bytes=VMEM_LIMIT),
        name="lat_attn",
    )(q, k, v, kc, vc, bias_tiles)


def _bias_tile_constants():
    n_dc = 2 * NA_WIN_COLS
    w = np.arange(GRID_W)[:, None]
    cc = np.arange(2 * GRID_W)[None, :]
    c = cc % GRID_W
    cs = np.clip(w - NA_WIN_COLS // 2, 0, GRID_W - NA_WIN_COLS)
    valid = (c >= cs) & (c < cs + NA_WIN_COLS)
    k_idx = (cc // GRID_W) * n_dc + (c - w + NA_WIN_COLS - 1)
    onehot = (np.arange(2 * n_dc)[:, None, None] == k_idx[None]) & valid[None]
    onehot = onehot.reshape(2 * n_dc, GRID_W * 2 * GRID_W)
    mask = np.where(valid, 0.0, MASK_VALUE).reshape(1, GRID_W * 2 * GRID_W)
    return jnp.asarray(onehot, BF16), jnp.asarray(mask, F32)


def _bias_tiles_kernel(rb_ref, oh_ref, mask_ref, o_ref):
    hi, mid, lo = _split3(rb_ref[...])
    oh = oh_ref[...]
    o_ref[...] = _dot(hi, oh) + _dot(mid, oh) + _dot(lo, oh) + mask_ref[...]


def _bias_tiles(rel_bias_l):
    n_dr = 2 * NA_WIN_ROWS - 1
    padded = jnp.pad(rel_bias_l, ((0, 0), (0, 0), (0, 1)))
    pairs = jnp.concatenate([padded[:, :-1], padded[:, 1:]], axis=-1)
    pairs = pairs.reshape(NA_HEADS * (n_dr - 1), 4 * NA_WIN_COLS)
    onehot, mask = _bias_tile_constants()
    full = lambda shape: pl.BlockSpec(shape, lambda: (0,) * len(shape))
    out = pl.pallas_call(
        _bias_tiles_kernel,
        in_specs=[full(pairs.shape), full(onehot.shape), full(mask.shape)],
        out_specs=full((pairs.shape[0], onehot.shape[1])),
        out_shape=jax.ShapeDtypeStruct((pairs.shape[0], onehot.shape[1]), F32),
        name="bias_tiles",
    )(pairs, onehot, mask)
    return out.reshape(NA_HEADS, n_dr - 1, GRID_W, 2 * GRID_W)


def _gla_block(q_ref, k_ref, v_ref, b_ref, st_ref, o_ref, expand, head_mask, r0, reverse):
    half = GLA_BLOCK // 2
    rows = pl.ds(r0, GLA_BLOCK)
    qb = q_ref[0, rows, :]
    kb = k_ref[0, rows, :]
    vb = v_ref[0, rows, :]
    bb = b_ref[rows, :]
    end_row = 0 if reverse else GLA_BLOCK - 1
    b_end = bb[end_row:end_row + 1, :]
    st = st_ref[...]

    o_inter = _dot_nt((qb * jnp.exp(bb)).astype(BF16), st.astype(BF16))

    t_in = lax.broadcasted_iota(jnp.int32, (half, 1), 0)
    q_half = (qb[:half], qb[half:])
    b_half = (bb[:half], bb[half:])
    pieces, owners = [], []
    for s in range(GLA_BLOCK):
        hs = s // half
        k_s = kb[s:s + 1, :]
        b_s = bb[s:s + 1, :]
        for ht in range(2):
            if (ht > hs) if reverse else (ht < hs):
                continue
            decay = jnp.exp(b_half[ht] - b_s)
            if ht == hs:
                seen = (t_in <= s - hs * half) if reverse else (t_in >= s - hs * half)
                decay = jnp.where(seen, decay, 0.0)
            pieces.append(q_half[ht] * k_s * decay)
            owners.append((ht, s))
    a_exp = _dot(jnp.concatenate(pieces, axis=0).astype(BF16), expand)
    o_intra = [jnp.zeros((half, GLA_WIDTH), F32), jnp.zeros((half, GLA_WIDTH), F32)]
    for i, (ht, s) in enumerate(owners):
        o_intra[ht] = o_intra[ht] + a_exp[i * half:(i + 1) * half, :] * vb[s:s + 1, :]
    o_ref[0, rows, :] = o_inter + jnp.concatenate(o_intra, axis=0)

    k_hat = (kb * jnp.exp(b_end - bb)).astype(BF16)
    upd = _dot_tn(vb.astype(BF16), k_hat)
    st_ref[...] = st * jnp.exp(b_end) + upd * head_mask


def _gla_kernel(qf_ref, kf_ref, vf_ref, laf_ref, qb_ref, kb_ref, vb_ref, lab_ref, s0f_ref, s0b_ref,
                trif_ref, trib_ref, e_ref, of_ref, ob_ref, sff_ref, sfb_ref,
                stf_ref, stb_ref, bf_ref, bb_ref, *, tm):
    n = pl.program_id(1)
    n_blocks = tm // GLA_BLOCK

    @pl.when(n == 0)
    def _():
        stf_ref[...] = s0f_ref[0]
        stb_ref[...] = s0b_ref[0]

    for la_ref, tri_ref, b_ref in ((laf_ref, trif_ref, bf_ref), (lab_ref, trib_ref, bb_ref)):
        tri = tri_ref[...]
        g_hi, g_mid, g_lo = _split3(la_ref[0])
        b_ref[...] = _dot(tri, g_hi) + _dot(tri, g_mid) + _dot(tri, g_lo)

    expand = e_ref[...]
    head_v = lax.broadcasted_iota(jnp.int32, (GLA_WIDTH, GLA_KEY_WIDTH), 0) // GLA_DV
    head_k = lax.broadcasted_iota(jnp.int32, (GLA_WIDTH, GLA_KEY_WIDTH), 1) // GLA_DK
    head_mask = jnp.where(head_v == head_k, 1.0, 0.0)

    def block(j, carry):
        r_f = pl.multiple_of(j * GLA_BLOCK, GLA_BLOCK)
        r_b = pl.multiple_of((n_blocks - 1 - j) * GLA_BLOCK, GLA_BLOCK)
        _gla_block(qf_ref, kf_ref, vf_ref, bf_ref, stf_ref, of_ref, expand, head_mask, r_f, reverse=False)
        _gla_block(qb_ref, kb_ref, vb_ref, bb_ref, stb_ref, ob_ref, expand, head_mask, r_b, reverse=True)
        return carry

    lax.fori_loop(0, n_blocks, block, 0)

    @pl.when(n == pl.num_programs(1) - 1)
    def _():
        sff_ref[0] = stf_ref[...]
        sfb_ref[0] = stb_ref[...]


def _block_tri(tm, reverse):
    t = np.arange(tm)[:, None]
    s = np.arange(tm)[None, :]
    same = (t // GLA_BLOCK) == (s // GLA_BLOCK)
    return jnp.asarray(same & ((s >= t) if reverse else (s <= t)), BF16)


def _gla_scan(q, k, v, la, s0f_t, s0b_t, expand):
    B, L, _ = q.shape
    tm = min(ROW_TILE, L)
    n_tiles = L // tm
    fwd = lambda b, n: (b, n, 0)
    bwd = lambda b, n: (b, n_tiles - 1 - n, 0)
    state = pl.BlockSpec((1, GLA_WIDTH, GLA_KEY_WIDTH), lambda b, n: (b, 0, 0))
    const = lambda shape: pl.BlockSpec(shape, lambda b, n: (0, 0))

    def operands(tmap, la_col):
        return [pl.BlockSpec((1, tm, GLA_KEY_WIDTH), tmap), pl.BlockSpec((1, tm, GLA_KEY_WIDTH), tmap),
                pl.BlockSpec((1, tm, GLA_WIDTH), tmap),
                pl.BlockSpec((1, tm, GLA_KEY_WIDTH), lambda b, n: tmap(b, n)[:2] + (la_col,))]

    return pl.pallas_call(
        functools.partial(_gla_kernel, tm=tm),
        grid=(B, n_tiles),
        in_specs=operands(fwd, 0) + operands(bwd, 1) + [state, state, const((tm, tm)), const((tm, tm)),
                                                        const((GLA_KEY_WIDTH, GLA_WIDTH))],
        out_specs=[pl.BlockSpec((1, tm, GLA_WIDTH), fwd), pl.BlockSpec((1, tm, GLA_WIDTH), bwd), state, state],
        out_shape=[jax.ShapeDtypeStruct((B, L, GLA_WIDTH), F32), jax.ShapeDtypeStruct((B, L, GLA_WIDTH), F32),
                   jax.ShapeDtypeStruct((B, GLA_WIDTH, GLA_KEY_WIDTH), F32),
                   jax.ShapeDtypeStruct((B, GLA_WIDTH, GLA_KEY_WIDTH), F32)],
        scratch_shapes=[pltpu.VMEM((GLA_WIDTH, GLA_KEY_WIDTH), F32), pltpu.VMEM((GLA_WIDTH, GLA_KEY_WIDTH), F32),
                        pltpu.VMEM((tm, GLA_KEY_WIDTH), F32), pltpu.VMEM((tm, GLA_KEY_WIDTH), F32)],
        compiler_params=pltpu.CompilerParams(vmem_limit_bytes=VMEM_LIMIT),
        name="gla",
    )(q, k, v, la, q, k, v, la, s0f_t, s0b_t, _block_tri(tm, False), _block_tri(tm, True), expand)


def _state_to_kernel(s):
    B = s.shape[0]
    same_head = np.eye(GLA_HEADS, dtype=bool)[None, :, None, :, None]
    st = jnp.where(same_head, s.transpose(0, 1, 3, 2)[:, :, :, None, :], 0.0)
    return st.reshape(B, GLA_WIDTH, GLA_KEY_WIDTH)


def _state_from_kernel(st):
    B = st.shape[0]
    blocks = st.reshape(B, GLA_HEADS, GLA_DV, GLA_HEADS, GLA_DK)
    diag = jnp.stack([blocks[:, h, :, h, :] for h in range(GLA_HEADS)], axis=1)
    return diag.transpose(0, 1, 3, 2)


def _post_kernel(x_ref, m_ref, yp_ref, yn_ref, of_ref, ob_ref, lg_ref, gg_ref, g2_ref,
                 wo_ref, wi_ref, wf_ref, o_ref, acc_ref):
    gate1 = m_ref[0, 2:3, :]
    shift2 = m_ref[0, 3:4, :]
    scale2 = m_ref[0, 4:5, :]
    gate2 = m_ref[0, 5:6, :]

    o_la = _half_lane_rms(of_ref[0] + ob_ref[0], NORM_EPS) * gg_ref[...]
    y_la = (o_la * _silu(lg_ref[0])).astype(BF16)
    mixed = (_dot(yp_ref[0], wo_ref[0:POOL_WIDTH, :])
             + _dot(yn_ref[0], wo_ref[POOL_WIDTH:POOL_WIDTH + NA_WIDTH, :])
             + _dot(y_la, wo_ref[POOL_WIDTH + NA_WIDTH:D_MODEL, :]))
    x1 = x_ref[0] + gate1 * mixed

    h2 = (_row_rms(x1, NORM_EPS) * g2_ref[...] * (1.0 + scale2) + shift2).astype(BF16)
    for c0 in range(0, D_FF, FF_CHUNK):
        hg = _dot(h2, wi_ref[:, c0:c0 + FF_CHUNK])
        hu = _dot(h2, wi_ref[:, D_FF + c0:D_FF + c0 + FF_CHUNK])
        part = _dot((_silu(hg) * hu).astype(BF16), wf_ref[c0:c0 + FF_CHUNK, :])
        if c0 == 0:
            acc_ref[...] = part
        else:
            acc_ref[...] += part
    o_ref[0] = x1 + gate2 * acc_ref[...]


def _post(x, mods_l, row_of_batch, y_pool, y_na, o_f, o_b, lg, lw):
    B, L, _ = x.shape
    tm = min(ROW_TILE, L)
    const = lambda b, i: (0, 0)
    tile = lambda b, i: (b, i, 0)

    def resident(shape):
        return pl.BlockSpec(shape, const, pipeline_mode=pl.Buffered(1))

    return pl.pallas_call(
        _post_kernel,
        grid=(B, L // tm),
        in_specs=[
            pl.BlockSpec((1, tm, D_MODEL), tile),
            pl.BlockSpec((1, N_MODS, D_MODEL), lambda b, i: (row_of_batch(b), 0, 0)),
            pl.BlockSpec((1, tm, POOL_WIDTH), tile),
            pl.BlockSpec((1, tm, NA_WIDTH), tile),
            pl.BlockSpec((1, tm, GLA_WIDTH), tile),
            pl.BlockSpec((1, tm, GLA_WIDTH), tile),
            pl.BlockSpec((1, tm, GLA_WIDTH), tile),
            pl.BlockSpec((1, GLA_WIDTH), const),
            pl.BlockSpec((1, D_MODEL), const),
            resident((D_MODEL, D_MODEL)),
            resident((D_MODEL, 2 * D_FF)),
            resident((D_FF, D_MODEL)),
        ],
        out_specs=pl.BlockSpec((1, tm, D_MODEL), tile),
        out_shape=jax.ShapeDtypeStruct((B, L, D_MODEL), F32),
        scratch_shapes=[pltpu.VMEM((tm, D_MODEL), F32)],
        compiler_params=pltpu.CompilerParams(vmem_limit_bytes=VMEM_LIMIT),
        name="post",
    )(x, mods_l, y_pool, y_na, o_f, o_b, lg, lw["gla_gain"], lw["norm2_gain"],
      lw["w_out"], lw["w_ffn_in"], lw["w_ffn_out"])


def _rope_tables(L):
    t = jnp.arange(L)
    row = (t // GRID_W).astype(F32)
    col = (t % GRID_W).astype(F32)
    half = GLA_DK // 2
    inv_freq = ROPE_THETA ** (-jnp.arange(0, half, 2, dtype=F32) / half)
    ang_r = row[:, None] * inv_freq
    ang_c = col[:, None] * inv_freq
    sign = jnp.concatenate([-jnp.ones((half // 2,), F32), jnp.ones((half // 2,), F32)])

    def lanes(fn, signed):
        per_axis = []
        for ang in (ang_r, ang_c):
            v = jnp.concatenate([fn(ang), fn(ang)], axis=-1)
            per_axis.append(v * sign if signed else v)
        return jnp.tile(jnp.concatenate(per_axis, axis=-1), (1, GLA_HEADS))

    return lanes(jnp.cos, False), lanes(jnp.sin, True)


def _layer_weights(l, w):
    tile_heads = lambda g, n: jnp.tile(g, n)[None, :]
    return {
        "norm1_gain": w["norm1_gain"][l][None, :],
        "norm2_gain": w["norm2_gain"][l][None, :],
        "w_in": w["w_in"][l].astype(BF16),
        "q_gain": tile_heads(w["q_norm_gain"][l], NA_HEADS),
        "k_gain": tile_heads(w["k_norm_gain"][l], NA_HEADS),
        "w_gate": jax.scipy.linalg.block_diag(w["w_gate_f"][l], w["w_gate_b"][l]).astype(BF16),
        "b_gate": jnp.concatenate([w["b_gate_f"][l], w["b_gate_b"][l]])[None, :],
        "w_pool_bd": jax.scipy.linalg.block_diag(*[w["w_pool"][l, g] for g in range(len(POOL_WINDOWS))]).astype(BF16),
        "pool_scale": w["pool_scale"][l][None, :],
        "gla_gain": tile_heads(w["gla_norm_gain"][l], GLA_HEADS),
        "w_out": w["w_out"][l].astype(BF16),
        "w_ffn_in": w["w_ffn_in"][l].astype(BF16),
        "w_ffn_out": w["w_ffn_out"][l].astype(BF16),
    }


def _trunk_layer(x, mods_l, row_of_batch, lw, expand, s0_f, s0_b, latent):
    is_ctx = latent is None
    outs = _in_proj(x, mods_l, row_of_batch, lw, None if is_ctx else latent[0], kv_f32=is_ctx)
    u, q, k, v, lq, lk, lv, lg, la = outs[:9]
    y_pool = _pool(u, lw)
    if is_ctx:
        y_na = _ctx_attn(q, k, v)
    else:
        y_na = _lat_attn(q, k, v, latent[1], latent[2], latent[3])
    o_f, o_b, s_f, s_b = _gla_scan(lq, lk, lv, la, s0_f, s0_b, expand)
    x = _post(x, mods_l, row_of_batch, y_pool, y_na, o_f, o_b, lg, lw)
    extras = (outs[9], outs[10]) if is_ctx else ()
    return x, s_f, s_b, extras


def kernel(x_prompt, x_sample, c, cache_na_k, cache_na_v, state_gla_fwd, state_gla_bwd, c_ctx, w_ada, b_ada, norm1_gain, norm2_gain, w_in, w_pool, pool_scale, q_norm_gain, k_norm_gain, rel_bias, w_gate_f, b_gate_f, w_gate_b, b_gate_b, gla_norm_gain, w_out, w_ffn_in, w_ffn_out):
    weights = dict(norm1_gain=norm1_gain, norm2_gain=norm2_gain, w_in=w_in, w_pool=w_pool, pool_scale=pool_scale,
                   q_norm_gain=q_norm_gain, k_norm_gain=k_norm_gain, w_gate_f=w_gate_f, b_gate_f=b_gate_f,
                   w_gate_b=w_gate_b, b_gate_b=b_gate_b, gla_norm_gain=gla_norm_gain, w_out=w_out,
                   w_ffn_in=w_ffn_in, w_ffn_out=w_ffn_out)
    B_ctx, L_ctx, _ = x_prompt.shape
    B_lat, L_lat, _ = x_sample.shape
    assert 1 + B_lat <= COND_ROWS

    conds = jnp.zeros((COND_ROWS, D_MODEL), F32).at[0].set(c_ctx).at[1:1 + B_lat].set(c)
    mods = _ada(conds, w_ada, b_ada).reshape(DEPTH, COND_ROWS, N_MODS, D_MODEL)
    lws = [_layer_weights(l, weights) for l in range(DEPTH)]
    head_of_k = np.arange(GLA_KEY_WIDTH) // GLA_DK
    head_of_v = np.arange(GLA_WIDTH) // GLA_DV
    expand = jnp.asarray(head_of_k[:, None] == head_of_v[None, :], BF16)

    xp = x_prompt
    zero_state = jnp.zeros((B_ctx, GLA_WIDTH, GLA_KEY_WIDTH), F32)
    ks, vs, sfs, sbs = [], [], [], []
    for l in range(DEPTH):
        xp, s_f, s_b, (k_f32, v_f32) = _trunk_layer(xp, mods[l], lambda b: 0, lws[l], expand,
                                                     zero_state, zero_state, None)
        to_heads = lambda a: a.reshape(B_ctx, L_ctx, NA_HEADS, NA_HEAD_DIM).transpose(0, 2, 1, 3)
        ks.append(to_heads(k_f32))
        vs.append(to_heads(v_f32))
        sfs.append(_state_from_kernel(s_f))
        sbs.append(_state_from_kernel(s_b))

    xs = x_sample
    rope_tabs = _rope_tables(L_lat)
    from_heads = lambda a: a.transpose(0, 2, 1, 3).reshape(B_lat, a.shape[2], NA_WIDTH).astype(BF16)
    for l in range(DEPTH):
        latent = (rope_tabs, from_heads(cache_na_k[:, l]), from_heads(cache_na_v[:, l]), _bias_tiles(rel_bias[l]))
        xs, _, _, _ = _trunk_layer(xs, mods[l], lambda b: b + 1, lws[l], expand,
                                   _state_to_kernel(state_gla_fwd[:, l]), _state_to_kernel(state_gla_bwd[:, l]),
                                   latent)

    return (xp, xs, jnp.stack(ks, axis=1), jnp.stack(vs, axis=1), jnp.stack(sfs, axis=1), jnp.stack(sbs, axis=1))
```

```python
import functools
import math

import numpy as np
import jax
import jax.numpy as jnp
from jax import lax
from jax.experimental import pallas as pl
from jax.experimental.pallas import tpu as pltpu

F32 = jnp.float32
BF16 = jnp.bfloat16

D_MODEL = 1024
DEPTH = 2
GRID_W = 64
POOL_WIDTH = 256
POOL_GROUP_DIM = 64
POOL_WINDOWS = (2, 4, 8, 16)
POOL_HALO = max(POOL_WINDOWS) // 2
NA_HEADS = 8
NA_HEAD_DIM = 64
NA_WIDTH = NA_HEADS * NA_HEAD_DIM
NA_WIN_ROWS = 8
NA_WIN_COLS = 16
NA_ROWS_PER_STEP = 4
GLA_HEADS = 4
GLA_DV = 64
GLA_DK = 32
GLA_WIDTH = GLA_HEADS * GLA_DV
GLA_KEY_WIDTH = GLA_HEADS * GLA_DK
GLA_GATE_RANK = 16
GLA_GATE_TAU = 16.0
GLA_BLOCK = 16
ROPE_THETA = 10000.0
D_FF = 2816
NORM_EPS = 1e-6
N_MODS = 6
COND_ROWS = 16
MASK_VALUE = -1e30
LOG2_E = 1.4426950408889634

OFF_POOL = 0
OFF_NA_Q = OFF_POOL + POOL_WIDTH
OFF_NA_K = OFF_NA_Q + NA_WIDTH
OFF_NA_V = OFF_NA_K + NA_WIDTH
OFF_LA_Q = OFF_NA_V + NA_WIDTH
OFF_LA_K = OFF_LA_Q + GLA_KEY_WIDTH
OFF_LA_V = OFF_LA_K + GLA_KEY_WIDTH
OFF_LA_G = OFF_LA_V + GLA_WIDTH
OFF_LR = OFF_LA_G + GLA_WIDTH
IN_WIDTH = OFF_LR + 2 * GLA_GATE_RANK

LANES = 128
ROW_TILE = 512
GLA_TILE = 256
GLA_BATCH_PER_STEP = 4
FF_CHUNK = 256
ADA_COL_TILE = 512
VMEM_LIMIT = 56 * 1024 * 1024


def _dot(a, b):
    return jnp.dot(a, b, preferred_element_type=F32)


def _dot_nt(a, b):
    return lax.dot_general(a, b, (((1,), (1,)), ((), ())), preferred_element_type=F32)


def _dot_tn(a, b):
    return lax.dot_general(a, b, (((0,), (0,)), ((), ())), preferred_element_type=F32)


def _silu(x):
    return x * (1.0 / (1.0 + jnp.exp(-x)))


def _split3(x):
    hi = x.astype(BF16)
    r1 = x - hi.astype(F32)
    mid = r1.astype(BF16)
    lo = (r1 - mid.astype(F32)).astype(BF16)
    return hi, mid, lo


def _half_lane_rms(x, eps):
    lane = lax.broadcasted_iota(jnp.int32, (1, LANES), 1)
    low = lane < NA_HEAD_DIM
    cols = []
    for j in range(x.shape[-1] // LANES):
        blk = x[:, j * LANES:(j + 1) * LANES]
        sq = blk * blk
        s_lo = jnp.sum(jnp.where(low, sq, 0.0), axis=-1, keepdims=True)
        s_hi = jnp.sum(jnp.where(low, 0.0, sq), axis=-1, keepdims=True)
        r_lo = lax.rsqrt(s_lo * (1.0 / NA_HEAD_DIM) + eps)
        r_hi = lax.rsqrt(s_hi * (1.0 / NA_HEAD_DIM) + eps)
        cols.append(blk * jnp.where(low, r_lo, r_hi))
    return jnp.concatenate(cols, axis=-1)


def _row_rms(x, eps):
    return x * lax.rsqrt(jnp.mean(x * x, axis=-1, keepdims=True) + eps)


def _ada_kernel(c_ref, w_ref, b_ref, o_ref):
    s_hi, s_mid, _ = _split3(_silu(c_ref[...]))
    w = w_ref[0]
    w_hi, w_mid, _ = _split3(w)
    acc = _dot(s_hi, w_hi) + _dot(s_mid, w_hi) + _dot(s_hi, w_mid)
    o_ref[0] = acc + b_ref[0]


def _ada(conds, w_ada, b_ada):
    n_out = w_ada.shape[-1]
    return pl.pallas_call(
        _ada_kernel,
        grid=(DEPTH, n_out // ADA_COL_TILE),
        in_specs=[
            pl.BlockSpec((COND_ROWS, D_MODEL), lambda l, j: (0, 0)),
            pl.BlockSpec((1, D_MODEL, ADA_COL_TILE), lambda l, j: (l, 0, j)),
            pl.BlockSpec((1, 1, ADA_COL_TILE), lambda l, j: (l, 0, j)),
        ],
        out_specs=pl.BlockSpec((1, COND_ROWS, ADA_COL_TILE), lambda l, j: (l, 0, j)),
        out_shape=jax.ShapeDtypeStruct((DEPTH, COND_ROWS, n_out), F32),
        name="ada",
    )(conds, w_ada, b_ada.reshape(DEPTH, 1, n_out))


def _inproj_kernel(*refs, rope, kv_f32):
    x_ref, m_ref, g1_ref, w_ref, qg_ref, kg_ref, wg_ref, bg_ref = refs[:8]
    pos = 8
    if rope:
        cos_ref, sin_ref = refs[pos:pos + 2]
        pos += 2
    u_ref, q_ref, k_ref, v_ref, lq_ref, lk_ref, lv_ref, lg_ref, la_ref = refs[pos:pos + 9]
    pos += 9
    if kv_f32:
        kf_ref, vf_ref = refs[pos:pos + 2]

    x = x_ref[0]
    shift1 = m_ref[0, 0:1, :]
    scale1 = m_ref[0, 1:2, :]
    h = _row_rms(x, NORM_EPS) * g1_ref[...] * (1.0 + scale1) + shift1
    hb = h.astype(BF16)

    def proj(off, width):
        return _dot(hb, w_ref[:, off:off + width])

    u_ref[0] = proj(OFF_POOL, POOL_WIDTH)

    qn = _half_lane_rms(proj(OFF_NA_Q, NA_WIDTH), NORM_EPS) * qg_ref[...]
    q_ref[0] = (qn * (NA_HEAD_DIM ** -0.5 * LOG2_E)).astype(BF16)
    kn = _half_lane_rms(proj(OFF_NA_K, NA_WIDTH), NORM_EPS) * kg_ref[...]
    k_ref[0] = kn.astype(BF16)
    vn = proj(OFF_NA_V, NA_WIDTH)
    v_ref[0] = vn.astype(BF16)
    if kv_f32:
        kf_ref[0] = kn
        vf_ref[0] = vn

    lq = proj(OFF_LA_Q, GLA_KEY_WIDTH) * (GLA_DK ** -0.5)
    lk = proj(OFF_LA_K, GLA_KEY_WIDTH)
    if rope:
        lane = lax.broadcasted_iota(jnp.int32, (1, LANES), 1)
        first = (lane % 16) < 8
        cos = cos_ref[...]
        sin = sin_ref[...]

        def rot(t):
            partner = jnp.where(first, pltpu.roll(t, LANES - 8, axis=1), pltpu.roll(t, 8, axis=1))
            return t * cos + partner * sin

        lq = rot(lq)
        lk = rot(lk)
    lq_ref[0] = lq
    lk_ref[0] = lk
    lv_ref[0] = proj(OFF_LA_V, GLA_WIDTH)
    lg_ref[0] = proj(OFF_LA_G, GLA_WIDTH)

    lr = proj(OFF_LR, 2 * GLA_GATE_RANK)
    z = _dot(lr.astype(BF16), wg_ref[...]) + bg_ref[...]
    log_sig = jnp.minimum(z, 0.0) - jnp.log1p(jnp.exp(-jnp.abs(z)))
    la_ref[0] = log_sig * (LOG2_E / GLA_GATE_TAU)


def _in_proj(x, mods_l, row_of_batch, lw, rope_tabs, kv_f32):
    B, L, _ = x.shape
    tm = min(ROW_TILE, L)
    rope = rope_tabs is not None
    const = lambda b, i: (0, 0)
    tile = lambda b, i: (b, i, 0)

    in_specs = [
        pl.BlockSpec((1, tm, D_MODEL), tile),
        pl.BlockSpec((1, N_MODS, D_MODEL), lambda b, i: (row_of_batch(b), 0, 0)),
        pl.BlockSpec((1, D_MODEL), const),
        pl.BlockSpec((D_MODEL, IN_WIDTH), const),
        pl.BlockSpec((1, NA_WIDTH), const),
        pl.BlockSpec((1, NA_WIDTH), const),
        pl.BlockSpec((2 * GLA_GATE_RANK, 2 * GLA_KEY_WIDTH), const),
        pl.BlockSpec((1, 2 * GLA_KEY_WIDTH), const),
    ]
    args = [x, mods_l, lw["norm1_gain"], lw["w_in"], lw["q_gain"], lw["k_gain"], lw["w_gate"], lw["b_gate"]]
    if rope:
        in_specs += [pl.BlockSpec((tm, GLA_KEY_WIDTH), lambda b, i: (i, 0))] * 2
        args += list(rope_tabs)

    widths = [(POOL_WIDTH, F32), (NA_WIDTH, BF16), (NA_WIDTH, BF16), (NA_WIDTH, BF16),
              (GLA_KEY_WIDTH, F32), (GLA_KEY_WIDTH, F32), (GLA_WIDTH, F32), (GLA_WIDTH, F32),
              (2 * GLA_KEY_WIDTH, F32)]
    if kv_f32:
        widths += [(NA_WIDTH, F32), (NA_WIDTH, F32)]
    out_specs = [pl.BlockSpec((1, tm, w), tile) for w, _ in widths]
    out_shape = [jax.ShapeDtypeStruct((B, L, w), dt) for w, dt in widths]

    return pl.pallas_call(
        functools.partial(_inproj_kernel, rope=rope, kv_f32=kv_f32),
        grid=(B, L // tm),
        in_specs=in_specs,
        out_specs=out_specs,
        out_shape=out_shape,
        compiler_params=pltpu.CompilerParams(vmem_limit_bytes=VMEM_LIMIT),
        name="in_proj",
    )(*args)


def _pool_kernel(u_ref, w_ref, sc_ref, o_ref, pad_ref, *, L, tp):
    H = POOL_HALO
    zeros = jnp.zeros((H, POOL_WIDTH), F32)
    pad_ref[0:H, :] = zeros
    pad_ref[H + L:H + L + H, :] = zeros
    pad_ref[H:H + L, :] = u_ref[0]
    lane = lax.broadcasted_iota(jnp.int32, (1, LANES), 1)
    low = lane < POOL_GROUP_DIM

    for base in range(0, L, tp):
        t = base + lax.broadcasted_iota(jnp.int32, (tp, 1), 0)

        def count(win):
            return (jnp.minimum(t + win // 2, L) - jnp.maximum(t - win // 2, 0)).astype(F32)

        def shifted(off, col):
            return pad_ref[H + base + off:H + base + off + tp, col * LANES:(col + 1) * LANES]

        u_a = shifted(0, 0)
        w2 = shifted(-1, 0) + u_a
        w4 = w2 + shifted(-2, 0) + shifted(1, 0)
        mean_a = jnp.where(low, w2 / count(2), w4 / count(4))
        u_b = shifted(0, 1)
        w8 = u_b
        for off in (-4, -3, -2, -1, 1, 2, 3):
            w8 = w8 + shifted(off, 1)
        w16 = w8
        for off in (-8, -7, -6, -5, 4, 5, 6, 7):
            w16 = w16 + shifted(off, 1)
        mean_b = jnp.where(low, w8 / count(8), w16 / count(16))
        d = jnp.concatenate([mean_a - u_a, mean_b - u_b], axis=-1).astype(BF16)
        y = _dot(d, w_ref[...]) * sc_ref[...]
        o_ref[0, base:base + tp, :] = y.astype(BF16)


def _pool(u, lw):
    B, L, _ = u.shape
    tp = min(512, L)
    return pl.pallas_call(
        functools.partial(_pool_kernel, L=L, tp=tp),
        grid=(B,),
        in_specs=[
            pl.BlockSpec((1, L, POOL_WIDTH), lambda b: (b, 0, 0)),
            pl.BlockSpec((POOL_WIDTH, POOL_WIDTH), lambda b: (0, 0)),
            pl.BlockSpec((1, POOL_WIDTH), lambda b: (0, 0)),
        ],
        out_specs=pl.BlockSpec((1, L, POOL_WIDTH), lambda b: (b, 0, 0)),
        out_shape=jax.ShapeDtypeStruct((B, L, POOL_WIDTH), BF16),
        scratch_shapes=[pltpu.VMEM((L + 2 * POOL_HALO, POOL_WIDTH), F32)],
        compiler_params=pltpu.CompilerParams(vmem_limit_bytes=VMEM_LIMIT),
        name="pool",
    )(u, lw["w_pool_bd"], lw["pool_scale"])


def _ctx_attn_kernel(q_ref, k_ref, v_ref, o_ref):
    q = q_ref[0]
    k = k_ref[0]
    v = v_ref[0]
    outs = []
    for h in range(NA_HEADS):
        sl = slice(h * NA_HEAD_DIM, (h + 1) * NA_HEAD_DIM)
        s = _dot_nt(q[:, sl], k[:, sl])
        p = jnp.exp2(s - jnp.max(s, axis=-1, keepdims=True))
        denom = jnp.sum(p, axis=-1, keepdims=True)
        outs.append(_dot(p.astype(BF16), v[:, sl]) / denom)
    o_ref[0] = jnp.concatenate(outs, axis=-1).astype(BF16)


def _ctx_attn(q, k, v):
    B, L, _ = q.shape
    spec = pl.BlockSpec((1, L, NA_WIDTH), lambda b: (b, 0, 0))
    return pl.pallas_call(
        _ctx_attn_kernel,
        grid=(B,),
        in_specs=[spec, spec, spec],
        out_specs=spec,
        out_shape=jax.ShapeDtypeStruct((B, L, NA_WIDTH), BF16),
        name="ctx_attn",
    )(q, k, v)


def _band_start(r, n_rows):
    return jnp.clip(r - NA_WIN_ROWS // 2, 0, n_rows - NA_WIN_ROWS)


def _lat_attn_kernel(q_ref, k_ref, v_ref, kc_ref, vc_ref, bias_ref, o_ref, *, n_rows):
    band = NA_WIN_ROWS * GRID_W
    lane = lax.broadcasted_iota(jnp.int32, (1, LANES), 1)
    even = lane < NA_HEAD_DIM
    zero = jnp.zeros((), BF16)

    def band_of(rr):
        r = pl.program_id(1) * NA_ROWS_PER_STEP + rr
        first = _band_start(r, n_rows)
        return pl.multiple_of(first * GRID_W, GRID_W), NA_WIN_ROWS - 1 - (r - first)

    def scores(rr, j):
        start, tile0 = band_of(rr)
        cols = slice(j * LANES, (j + 1) * LANES)
        qp = q_ref[0, rr * GRID_W:(rr + 1) * GRID_W, cols]
        q2 = jnp.concatenate([jnp.where(even, qp, zero), jnp.where(even, zero, qp)], axis=0)
        bias = jnp.concatenate(
            [jnp.concatenate([bias_ref[2 * j + hh, tile0 + 2 * ii] for ii in range(NA_WIN_ROWS // 2)], axis=-1)
             for hh in range(2)], axis=0)
        return _dot_nt(q2, k_ref[0, pl.ds(start, band), cols]) + bias, _dot_nt(q2, kc_ref[0, :, cols])

    def softmax(s_loc, s_ctx):
        m = jnp.maximum(jnp.max(s_loc, axis=-1, keepdims=True), jnp.max(s_ctx, axis=-1, keepdims=True))
        p_loc = jnp.exp2(s_loc - m)
        p_ctx = jnp.exp2(s_ctx - m)
        denom = jnp.sum(p_loc, axis=-1, keepdims=True) + jnp.sum(p_ctx, axis=-1, keepdims=True)
        return p_loc.astype(BF16), p_ctx.astype(BF16), denom

    def values(rr, j, p_loc, p_ctx, denom):
        start, _ = band_of(rr)
        cols = slice(j * LANES, (j + 1) * LANES)
        o2 = (_dot(p_loc, v_ref[0, pl.ds(start, band), cols]) + _dot(p_ctx, vc_ref[0, :, cols])) / denom
        o_ref[0, rr * GRID_W:(rr + 1) * GRID_W, cols] = jnp.where(even, o2[:GRID_W], o2[GRID_W:]).astype(BF16)

    chains = [(rr, j) for rr in range(NA_ROWS_PER_STEP) for j in range(NA_HEADS // 2)]
    s_next = scores(*chains[0])
    p_prev = None
    for i, chain in enumerate(chains):
        s_cur = s_next
        s_next = scores(*chains[i + 1]) if i + 1 < len(chains) else None
        p_cur = softmax(*s_cur)
        if p_prev is not None:
            values(*chains[i - 1], *p_prev)
        p_prev = p_cur
    values(*chains[-1], *p_prev)


def _lat_attn(q, k, v, kc, vc, bias_tiles):
    B, L, _ = q.shape
    n_rows = L // GRID_W
    assert n_rows >= NA_WIN_ROWS and n_rows % NA_ROWS_PER_STEP == 0
    past = kc.shape[1]
    whole = pl.BlockSpec((1, L, NA_WIDTH), lambda b, r: (b, 0, 0))
    rows = pl.BlockSpec((1, NA_ROWS_PER_STEP * GRID_W, NA_WIDTH), lambda b, r: (b, r, 0))
    ctx = pl.BlockSpec((1, past, NA_WIDTH), lambda b, r: (b, 0, 0))
    bias = pl.BlockSpec(bias_tiles.shape, lambda b, r: (0, 0, 0, 0))
    return pl.pallas_call(
        functools.partial(_lat_attn_kernel, n_rows=n_rows),
        grid=(B, n_rows // NA_ROWS_PER_STEP),
        in_specs=[rows, whole, whole, ctx, ctx, bias],
        out_specs=rows,
        out_shape=jax.ShapeDtypeStruct((B, L, NA_WIDTH), BF16),
        compiler_params=pltpu.CompilerParams(vmem_limit_bytes=VMEM_LIMIT),
        name="lat_attn",
    )(q, k, v, kc, vc, bias_tiles)


def _bias_tile_constants():
    n_dc = 2 * NA_WIN_COLS
    w = np.arange(GRID_W)[:, None]
    cc = np.arange(2 * GRID_W)[None, :]
    c = cc % GRID_W
    cs = np.clip(w - NA_WIN_COLS // 2, 0, GRID_W - NA_WIN_COLS)
    valid = (c >= cs) & (c < cs + NA_WIN_COLS)
    k_idx = (cc // GRID_W) * n_dc + (c - w + NA_WIN_COLS - 1)
    onehot = (np.arange(2 * n_dc)[:, None, None] == k_idx[None]) & valid[None]
    onehot = onehot.reshape(2 * n_dc, GRID_W * 2 * GRID_W)
    mask = np.where(valid, 0.0, MASK_VALUE).reshape(1, GRID_W * 2 * GRID_W)
    return jnp.asarray(onehot, BF16), jnp.asarray(mask, F32)


def _bias_tiles_kernel(rb_ref, oh_ref, mask_ref, o_ref):
    hi, mid, lo = _split3(rb_ref[...])
    oh = oh_ref[...]
    o_ref[...] = (_dot(hi, oh) + _dot(mid, oh) + _dot(lo, oh)) * LOG2_E + mask_ref[...]


def _bias_tiles(rel_bias_l):
    n_dr = 2 * NA_WIN_ROWS - 1
    padded = jnp.pad(rel_bias_l, ((0, 0), (0, 0), (0, 1)))
    pairs = jnp.concatenate([padded[:, :-1], padded[:, 1:]], axis=-1)
    pairs = pairs.reshape(NA_HEADS * (n_dr - 1), 4 * NA_WIN_COLS)
    onehot, mask = _bias_tile_constants()
    full = lambda shape: pl.BlockSpec(shape, lambda: (0,) * len(shape))
    out = pl.pallas_call(
        _bias_tiles_kernel,
        in_specs=[full(pairs.shape), full(onehot.shape), full(mask.shape)],
        out_specs=full((pairs.shape[0], onehot.shape[1])),
        out_shape=jax.ShapeDtypeStruct((pairs.shape[0], onehot.shape[1]), F32),
        name="bias_tiles",
    )(pairs, onehot, mask)
    return out.reshape(NA_HEADS, n_dr - 1, GRID_W, 2 * GRID_W)


def _gla_block(q_ref, k_ref, v_ref, b_ref, st_ref, o_ref, expand, head_mask, bi, r0, reverse):
    half = GLA_BLOCK // 2
    rows = pl.ds(r0, GLA_BLOCK)
    qb = q_ref[bi, rows, :]
    kb = k_ref[bi, rows, :]
    vb = v_ref[bi, rows, :]
    bb = b_ref[bi, rows, :]
    end_row = 0 if reverse else GLA_BLOCK - 1
    b_end = bb[end_row:end_row + 1, :]
    st = st_ref[bi]

    o_inter = _dot_nt((qb * jnp.exp2(bb)).astype(BF16), st.astype(BF16))

    t_in = lax.broadcasted_iota(jnp.int32, (half, 1), 0)
    q_half = (qb[:half], qb[half:])
    b_half = (bb[:half], bb[half:])
    pieces, owners = [], []
    for s in range(GLA_BLOCK):
        hs = s // half
        k_s = kb[s:s + 1, :]
        b_s = bb[s:s + 1, :]
        for ht in range(2):
            if (ht > hs) if reverse else (ht < hs):
                continue
            decay = jnp.exp2(b_half[ht] - b_s)
            if ht == hs:
                seen = (t_in <= s - hs * half) if reverse else (t_in >= s - hs * half)
                decay = jnp.where(seen, decay, 0.0)
            pieces.append(q_half[ht] * k_s * decay)
            owners.append((ht, s))
    a_exp = _dot(jnp.concatenate(pieces, axis=0).astype(BF16), expand)
    o_intra = [jnp.zeros((half, GLA_WIDTH), F32), jnp.zeros((half, GLA_WIDTH), F32)]
    for i, (ht, s) in enumerate(owners):
        o_intra[ht] = o_intra[ht] + a_exp[i * half:(i + 1) * half, :] * vb[s:s + 1, :]
    o_ref[bi, rows, :] = o_inter + jnp.concatenate(o_intra, axis=0)

    k_hat = (kb * jnp.exp2(b_end - bb)).astype(BF16)
    upd = _dot_tn(vb.astype(BF16), k_hat)
    st_ref[bi] = st * jnp.exp2(b_end) + upd * head_mask


def _gla_kernel(qf_ref, kf_ref, vf_ref, laf_ref, qb_ref, kb_ref, vb_ref, lab_ref, s0f_ref, s0b_ref,
                trif_ref, trib_ref, e_ref, of_ref, ob_ref, sff_ref, sfb_ref,
                stf_ref, stb_ref, bf_ref, bb_ref, *, tm, nb):
    n = pl.program_id(1)
    n_blocks = tm // GLA_BLOCK

    @pl.when(n == 0)
    def _():
        stf_ref[...] = s0f_ref[...]
        stb_ref[...] = s0b_ref[...]

    for la_ref, tri_ref, b_ref in ((laf_ref, trif_ref, bf_ref), (lab_ref, trib_ref, bb_ref)):
        tri = tri_ref[...]
        for bi in range(nb):
            g_hi, g_mid, _ = _split3(la_ref[bi])
            b_ref[bi] = _dot(tri, g_hi) + _dot(tri, g_mid)

    expand = e_ref[...]
    head_v = lax.broadcasted_iota(jnp.int32, (GLA_WIDTH, GLA_KEY_WIDTH), 0) // GLA_DV
    head_k = lax.broadcasted_iota(jnp.int32, (GLA_WIDTH, GLA_KEY_WIDTH), 1) // GLA_DK
    head_mask = jnp.where(head_v == head_k, 1.0, 0.0)

    def block(j, carry):
        r_f = pl.multiple_of(j * GLA_BLOCK, GLA_BLOCK)
        r_b = pl.multiple_of((n_blocks - 1 - j) * GLA_BLOCK, GLA_BLOCK)
        for bi in range(nb):
            _gla_block(qf_ref, kf_ref, vf_ref, bf_ref, stf_ref, of_ref, expand, head_mask, bi, r_f, reverse=False)
            _gla_block(qb_ref, kb_ref, vb_ref, bb_ref, stb_ref, ob_ref, expand, head_mask, bi, r_b, reverse=True)
        return carry

    lax.fori_loop(0, n_blocks, block, 0)

    @pl.when(n == pl.num_programs(1) - 1)
    def _():
        sff_ref[...] = stf_ref[...]
        sfb_ref[...] = stb_ref[...]


def _block_tri(tm, reverse):
    t = np.arange(tm)[:, None]
    s = np.arange(tm)[None, :]
    same = (t // GLA_BLOCK) == (s // GLA_BLOCK)
    return jnp.asarray(same & ((s >= t) if reverse else (s <= t)), BF16)


def _gla_scan(q, k, v, la, s0f_t, s0b_t, expand):
    B, L, _ = q.shape
    tm = min(GLA_TILE, L)
    n_tiles = L // tm
    nb = math.gcd(B, GLA_BATCH_PER_STEP)
    fwd = lambda b, n: (b, n, 0)
    bwd = lambda b, n: (b, n_tiles - 1 - n, 0)
    state = pl.BlockSpec((nb, GLA_WIDTH, GLA_KEY_WIDTH), lambda b, n: (b, 0, 0))
    const = lambda shape: pl.BlockSpec(shape, lambda b, n: (0, 0))

    def operands(tmap, la_col):
        return [pl.BlockSpec((nb, tm, GLA_KEY_WIDTH), tmap), pl.BlockSpec((nb, tm, GLA_KEY_WIDTH), tmap),
                pl.BlockSpec((nb, tm, GLA_WIDTH), tmap),
                pl.BlockSpec((nb, tm, GLA_KEY_WIDTH), lambda b, n: tmap(b, n)[:2] + (la_col,))]

    return pl.pallas_call(
        functools.partial(_gla_kernel, tm=tm, nb=nb),
        grid=(B // nb, n_tiles),
        in_specs=operands(fwd, 0) + operands(bwd, 1) + [state, state, const((tm, tm)), const((tm, tm)),
                                                        const((GLA_KEY_WIDTH, GLA_WIDTH))],
        out_specs=[pl.BlockSpec((nb, tm, GLA_WIDTH), fwd), pl.BlockSpec((nb, tm, GLA_WIDTH), bwd), state, state],
        out_shape=[jax.ShapeDtypeStruct((B, L, GLA_WIDTH), F32), jax.ShapeDtypeStruct((B, L, GLA_WIDTH), F32),
                   jax.ShapeDtypeStruct((B, GLA_WIDTH, GLA_KEY_WIDTH), F32),
                   jax.ShapeDtypeStruct((B, GLA_WIDTH, GLA_KEY_WIDTH), F32)],
        scratch_shapes=[pltpu.VMEM((nb, GLA_WIDTH, GLA_KEY_WIDTH), F32), pltpu.VMEM((nb, GLA_WIDTH, GLA_KEY_WIDTH), F32),
                        pltpu.VMEM((nb, tm, GLA_KEY_WIDTH), F32), pltpu.VMEM((nb, tm, GLA_KEY_WIDTH), F32)],
        compiler_params=pltpu.CompilerParams(vmem_limit_bytes=VMEM_LIMIT),
        name="gla",
    )(q, k, v, la, q, k, v, la, s0f_t, s0b_t, _block_tri(tm, False), _block_tri(tm, True), expand)


def _state_to_kernel(s):
    B = s.shape[0]
    same_head = np.eye(GLA_HEADS, dtype=bool)[None, :, None, :, None]
    st = jnp.where(same_head, s.transpose(0, 1, 3, 2)[:, :, :, None, :], 0.0)
    return st.reshape(B, GLA_WIDTH, GLA_KEY_WIDTH)


def _state_from_kernel(st):
    B = st.shape[0]
    blocks = st.reshape(B, GLA_HEADS, GLA_DV, GLA_HEADS, GLA_DK)
    diag = jnp.stack([blocks[:, h, :, h, :] for h in range(GLA_HEADS)], axis=1)
    return diag.transpose(0, 1, 3, 2)


def _post_kernel(x_ref, m_ref, yp_ref, yn_ref, of_ref, ob_ref, lg_ref, gg_ref, g2_ref,
                 wo_ref, wi_ref, wf_ref, o_ref, acc_ref):
    gate1 = m_ref[0, 2:3, :]
    shift2 = m_ref[0, 3:4, :]
    scale2 = m_ref[0, 4:5, :]
    gate2 = m_ref[0, 5:6, :]

    o_la = _half_lane_rms(of_ref[0] + ob_ref[0], NORM_EPS) * gg_ref[...]
    y_la = (o_la * _silu(lg_ref[0])).astype(BF16)
    mixed = (_dot(yp_ref[0], wo_ref[0:POOL_WIDTH, :])
             + _dot(yn_ref[0], wo_ref[POOL_WIDTH:POOL_WIDTH + NA_WIDTH, :])
             + _dot(y_la, wo_ref[POOL_WIDTH + NA_WIDTH:D_MODEL, :]))
    x1 = x_ref[0] + gate1 * mixed

    h2 = (_row_rms(x1, NORM_EPS) * g2_ref[...] * (1.0 + scale2) + shift2).astype(BF16)
    for c0 in range(0, D_FF, FF_CHUNK):
        hg = _dot(h2, wi_ref[:, c0:c0 + FF_CHUNK])
        hu = _dot(h2, wi_ref[:, D_FF + c0:D_FF + c0 + FF_CHUNK])
        part = _dot((_silu(hg) * hu).astype(BF16), wf_ref[c0:c0 + FF_CHUNK, :])
        if c0 == 0:
            acc_ref[...] = part
        else:
            acc_ref[...] += part
    o_ref[0] = x1 + gate2 * acc_ref[...]


def _post(x, mods_l, row_of_batch, y_pool, y_na, o_f, o_b, lg, lw):
    B, L, _ = x.shape
    tm = min(ROW_TILE, L)
    const = lambda b, i: (0, 0)
    tile = lambda b, i: (b, i, 0)

    def resident(shape):
        return pl.BlockSpec(shape, const, pipeline_mode=pl.Buffered(1))

    return pl.pallas_call(
        _post_kernel,
        grid=(B, L // tm),
        in_specs=[
            pl.BlockSpec((1, tm, D_MODEL), tile),
            pl.BlockSpec((1, N_MODS, D_MODEL), lambda b, i: (row_of_batch(b), 0, 0)),
            pl.BlockSpec((1, tm, POOL_WIDTH), tile),
            pl.BlockSpec((1, tm, NA_WIDTH), tile),
            pl.BlockSpec((1, tm, GLA_WIDTH), tile),
            pl.BlockSpec((1, tm, GLA_WIDTH), tile),
            pl.BlockSpec((1, tm, GLA_WIDTH), tile),
            pl.BlockSpec((1, GLA_WIDTH), const),
            pl.BlockSpec((1, D_MODEL), const),
            resident((D_MODEL, D_MODEL)),
            resident((D_MODEL, 2 * D_FF)),
            resident((D_FF, D_MODEL)),
        ],
        out_specs=pl.BlockSpec((1, tm, D_MODEL), tile),
        out_shape=jax.ShapeDtypeStruct((B, L, D_MODEL), F32),
        scratch_shapes=[pltpu.VMEM((tm, D_MODEL), F32)],
        compiler_params=pltpu.CompilerParams(vmem_limit_bytes=VMEM_LIMIT),
        name="post",
    )(x, mods_l, y_pool, y_na, o_f, o_b, lg, lw["gla_gain"], lw["norm2_gain"],
      lw["w_out"], lw["w_ffn_in"], lw["w_ffn_out"])


def _rope_tables(L):
    t = jnp.arange(L)
    row = (t // GRID_W).astype(F32)
    col = (t % GRID_W).astype(F32)
    half = GLA_DK // 2
    inv_freq = ROPE_THETA ** (-jnp.arange(0, half, 2, dtype=F32) / half)
    ang_r = row[:, None] * inv_freq
    ang_c = col[:, None] * inv_freq
    sign = jnp.concatenate([-jnp.ones((half // 2,), F32), jnp.ones((half // 2,), F32)])

    def lanes(fn, signed):
        per_axis = []
        for ang in (ang_r, ang_c):
            v = jnp.concatenate([fn(ang), fn(ang)], axis=-1)
            per_axis.append(v * sign if signed else v)
        return jnp.tile(jnp.concatenate(per_axis, axis=-1), (1, GLA_HEADS))

    return lanes(jnp.cos, False), lanes(jnp.sin, True)


def _layer_weights(l, w):
    tile_heads = lambda g, n: jnp.tile(g, n)[None, :]
    return {
        "norm1_gain": w["norm1_gain"][l][None, :],
        "norm2_gain": w["norm2_gain"][l][None, :],
        "w_in": w["w_in"][l].astype(BF16),
        "q_gain": tile_heads(w["q_norm_gain"][l], NA_HEADS),
        "k_gain": tile_heads(w["k_norm_gain"][l], NA_HEADS),
        "w_gate": jax.scipy.linalg.block_diag(w["w_gate_f"][l], w["w_gate_b"][l]).astype(BF16),
        "b_gate": jnp.concatenate([w["b_gate_f"][l], w["b_gate_b"][l]])[None, :],
        "w_pool_bd": jax.scipy.linalg.block_diag(*[w["w_pool"][l, g] for g in range(len(POOL_WINDOWS))]).astype(BF16),
        "pool_scale": w["pool_scale"][l][None, :],
        "gla_gain": tile_heads(w["gla_norm_gain"][l], GLA_HEADS),
        "w_out": w["w_out"][l].astype(BF16),
        "w_ffn_in": w["w_ffn_in"][l].astype(BF16),
        "w_ffn_out": w["w_ffn_out"][l].astype(BF16),
    }


def _trunk_layer(x, mods_l, row_of_batch, lw, expand, s0_f, s0_b, latent):
    is_ctx = latent is None
    outs = _in_proj(x, mods_l, row_of_batch, lw, None if is_ctx else latent[0], kv_f32=is_ctx)
    u, q, k, v, lq, lk, lv, lg, la = outs[:9]
    y_pool = _pool(u, lw)
    if is_ctx:
        y_na = _ctx_attn(q, k, v)
    else:
        y_na = _lat_attn(q, k, v, latent[1], latent[2], latent[3])
    o_f, o_b, s_f, s_b = _gla_scan(lq, lk, lv, la, s0_f, s0_b, expand)
    x = _post(x, mods_l, row_of_batch, y_pool, y_na, o_f, o_b, lg, lw)
    extras = (outs[9], outs[10]) if is_ctx else ()
    return x, s_f, s_b, extras


def kernel(x_prompt, x_sample, c, cache_na_k, cache_na_v, state_gla_fwd, state_gla_bwd, c_ctx, w_ada, b_ada, norm1_gain, norm2_gain, w_in, w_pool, pool_scale, q_norm_gain, k_norm_gain, rel_bias, w_gate_f, b_gate_f, w_gate_b, b_gate_b, gla_norm_gain, w_out, w_ffn_in, w_ffn_out):
    weights = dict(norm1_gain=norm1_gain, norm2_gain=norm2_gain, w_in=w_in, w_pool=w_pool, pool_scale=pool_scale,
                   q_norm_gain=q_norm_gain, k_norm_gain=k_norm_gain, w_gate_f=w_gate_f, b_gate_f=b_gate_f,
                   w_gate_b=w_gate_b, b_gate_b=b_gate_b, gla_norm_gain=gla_norm_gain, w_out=w_out,
                   w_ffn_in=w_ffn_in, w_ffn_out=w_ffn_out)
    B_ctx, L_ctx, _ = x_prompt.shape
    B_lat, L_lat, _ = x_sample.shape
    assert 1 + B_lat <= COND_ROWS

    conds = jnp.zeros((COND_ROWS, D_MODEL), F32).at[0].set(c_ctx).at[1:1 + B_lat].set(c)
    mods = _ada(conds, w_ada, b_ada).reshape(DEPTH, COND_ROWS, N_MODS, D_MODEL)
    lws = [_layer_weights(l, weights) for l in range(DEPTH)]
    head_of_k = np.arange(GLA_KEY_WIDTH) // GLA_DK
    head_of_v = np.arange(GLA_WIDTH) // GLA_DV
    expand = jnp.asarray(head_of_k[:, None] == head_of_v[None, :], BF16)

    xp = x_prompt
    zero_state = jnp.zeros((B_ctx, GLA_WIDTH, GLA_KEY_WIDTH), F32)
    ks, vs, sfs, sbs = [], [], [], []
    for l in range(DEPTH):
        xp, s_f, s_b, (k_f32, v_f32) = _trunk_layer(xp, mods[l], lambda b: 0, lws[l], expand,
                                                     zero_state, zero_state, None)
        to_heads = lambda a: a.reshape(B_ctx, L_ctx, NA_HEADS, NA_HEAD_DIM).transpose(0, 2, 1, 3)
        ks.append(to_heads(k_f32))
        vs.append(to_heads(v_f32))
        sfs.append(_state_from_kernel(s_f))
        sbs.append(_state_from_kernel(s_b))

    xs = x_sample
    rope_tabs = _rope_tables(L_lat)
    from_heads = lambda a: a.transpose(0, 2, 1, 3).reshape(B_lat, a.shape[2], NA_WIDTH).astype(BF16)
    for l in range(DEPTH):
        latent = (rope_tabs, from_heads(cache_na_k[:, l]), from_heads(cache_na_v[:, l]), _bias_tiles(rel_bias[l]))
        xs, _, _, _ = _trunk_layer(xs, mods[l], lambda b: b + 1, lws[l], expand,
                                   _state_to_kernel(state_gla_fwd[:, l]), _state_to_kernel(state_gla_bwd[:, l]),
                                   latent)

    return (xp, xs, jnp.stack(ks, axis=1), jnp.stack(vs, axis=1), jnp.stack(sfs, axis=1), jnp.stack(sbs, axis=1))
```

```python
import functools
import math

import numpy as np
import jax
import jax.numpy as jnp
from jax import lax
from jax.experimental import pallas as pl
from jax.experimental.pallas import tpu as pltpu

F32 = jnp.float32
BF16 = jnp.bfloat16

D_MODEL = 1024
DEPTH = 2
GRID_W = 64
POOL_WIDTH = 256
POOL_GROUP_DIM = 64
POOL_WINDOWS = (2, 4, 8, 16)
POOL_HALO = max(POOL_WINDOWS) // 2
NA_HEADS = 8
NA_HEAD_DIM = 64
NA_WIDTH = NA_HEADS * NA_HEAD_DIM
NA_WIN_ROWS = 8
NA_WIN_COLS = 16
NA_ROWS_PER_STEP = 4
GLA_HEADS = 4
GLA_DV = 64
GLA_DK = 32
GLA_WIDTH = GLA_HEADS * GLA_DV
GLA_KEY_WIDTH = GLA_HEADS * GLA_DK
GLA_GATE_RANK = 16
GLA_GATE_TAU = 16.0
GLA_BLOCK = 16
ROPE_THETA = 10000.0
D_FF = 2816
NORM_EPS = 1e-6
N_MODS = 6
COND_ROWS = 16
MASK_VALUE = -1e30
LOG2_E = 1.4426950408889634

OFF_POOL = 0
OFF_NA_Q = OFF_POOL + POOL_WIDTH
OFF_NA_K = OFF_NA_Q + NA_WIDTH
OFF_NA_V = OFF_NA_K + NA_WIDTH
OFF_LA_Q = OFF_NA_V + NA_WIDTH
OFF_LA_K = OFF_LA_Q + GLA_KEY_WIDTH
OFF_LA_V = OFF_LA_K + GLA_KEY_WIDTH
OFF_LA_G = OFF_LA_V + GLA_WIDTH
OFF_LR = OFF_LA_G + GLA_WIDTH
IN_WIDTH = OFF_LR + 2 * GLA_GATE_RANK

LANES = 128
ROW_TILE = 512
IN_ROW_TILE = 1024
IN_SUB_TILE = 256
GLA_TILE = 256
GLA_BATCH_PER_STEP = 4
FF_CHUNK = 256
ADA_COL_TILE = 512
VMEM_LIMIT = 56 * 1024 * 1024


def _dot(a, b):
    return jnp.dot(a, b, preferred_element_type=F32)


def _dot_nt(a, b):
    return lax.dot_general(a, b, (((1,), (1,)), ((), ())), preferred_element_type=F32)


def _dot_tn(a, b):
    return lax.dot_general(a, b, (((0,), (0,)), ((), ())), preferred_element_type=F32)


def _silu(x):
    return x * (1.0 / (1.0 + jnp.exp(-x)))


def _split3(x):
    hi = x.astype(BF16)
    r1 = x - hi.astype(F32)
    mid = r1.astype(BF16)
    lo = (r1 - mid.astype(F32)).astype(BF16)
    return hi, mid, lo


def _half_lane_rms(x, eps):
    lane = lax.broadcasted_iota(jnp.int32, (1, LANES), 1)
    low = lane < NA_HEAD_DIM
    cols = []
    for j in range(x.shape[-1] // LANES):
        blk = x[:, j * LANES:(j + 1) * LANES]
        sq = blk * blk
        s_lo = jnp.sum(jnp.where(low, sq, 0.0), axis=-1, keepdims=True)
        s_hi = jnp.sum(jnp.where(low, 0.0, sq), axis=-1, keepdims=True)
        r_lo = lax.rsqrt(s_lo * (1.0 / NA_HEAD_DIM) + eps)
        r_hi = lax.rsqrt(s_hi * (1.0 / NA_HEAD_DIM) + eps)
        cols.append(blk * jnp.where(low, r_lo, r_hi))
    return jnp.concatenate(cols, axis=-1)


def _row_rms(x, eps):
    return x * lax.rsqrt(jnp.mean(x * x, axis=-1, keepdims=True) + eps)


def _ada_kernel(c_ref, w_ref, b_ref, o_ref):
    s_hi, s_mid, _ = _split3(_silu(c_ref[...]))
    w = w_ref[0]
    w_hi, w_mid, _ = _split3(w)
    acc = _dot(s_hi, w_hi) + _dot(s_mid, w_hi) + _dot(s_hi, w_mid)
    o_ref[0] = acc + b_ref[0]


def _ada(conds, w_ada, b_ada):
    n_out = w_ada.shape[-1]
    return pl.pallas_call(
        _ada_kernel,
        grid=(DEPTH, n_out // ADA_COL_TILE),
        in_specs=[
            pl.BlockSpec((COND_ROWS, D_MODEL), lambda l, j: (0, 0)),
            pl.BlockSpec((1, D_MODEL, ADA_COL_TILE), lambda l, j: (l, 0, j)),
            pl.BlockSpec((1, 1, ADA_COL_TILE), lambda l, j: (l, 0, j)),
        ],
        out_specs=pl.BlockSpec((1, COND_ROWS, ADA_COL_TILE), lambda l, j: (l, 0, j)),
        out_shape=jax.ShapeDtypeStruct((DEPTH, COND_ROWS, n_out), F32),
        name="ada",
    )(conds, w_ada, b_ada.reshape(DEPTH, 1, n_out))


def _inproj_kernel(*refs, rope, kv_f32):
    x_ref, m_ref, g1_ref, w_ref, qg_ref, kg_ref, wg_ref, bg_ref = refs[:8]
    pos = 8
    if rope:
        cos_ref, sin_ref = refs[pos:pos + 2]
        pos += 2
    u_ref, q_ref, k_ref, v_ref, lq_ref, lk_ref, lv_ref, lg_ref, la_ref = refs[pos:pos + 9]
    pos += 9
    if kv_f32:
        kf_ref, vf_ref = refs[pos:pos + 2]

    shift1 = m_ref[0, 0:1, :]
    scale1 = m_ref[0, 1:2, :]
    tm = x_ref.shape[1]
    sub = min(IN_SUB_TILE, tm)

    def normed(i):
        rows = slice(i * sub, (i + 1) * sub)
        h = _row_rms(x_ref[0, rows, :], NORM_EPS) * g1_ref[...] * (1.0 + scale1) + shift1
        return h.astype(BF16)

    def project(i, hb):
        rows = slice(i * sub, (i + 1) * sub)

        def proj(off, width):
            return _dot(hb, w_ref[:, off:off + width])

        u_ref[0, rows, :] = proj(OFF_POOL, POOL_WIDTH)

        qn = _half_lane_rms(proj(OFF_NA_Q, NA_WIDTH), NORM_EPS) * qg_ref[...]
        q_ref[0, rows, :] = (qn * (NA_HEAD_DIM ** -0.5 * LOG2_E)).astype(BF16)
        kn = _half_lane_rms(proj(OFF_NA_K, NA_WIDTH), NORM_EPS) * kg_ref[...]
        k_ref[0, rows, :] = kn.astype(BF16)
        vn = proj(OFF_NA_V, NA_WIDTH)
        v_ref[0, rows, :] = vn.astype(BF16)
        if kv_f32:
            kf_ref[0, rows, :] = kn
            vf_ref[0, rows, :] = vn

        lqk = proj(OFF_LA_Q, 2 * GLA_KEY_WIDTH)
        lq = lqk[:, :GLA_KEY_WIDTH] * (GLA_DK ** -0.5)
        lk = lqk[:, GLA_KEY_WIDTH:]
        if rope:
            lane = lax.broadcasted_iota(jnp.int32, (1, LANES), 1)
            first = (lane % 16) < 8
            cos = cos_ref[rows, :]
            sin = sin_ref[rows, :]

            def rot(t):
                partner = jnp.where(first, pltpu.roll(t, LANES - 8, axis=1), pltpu.roll(t, 8, axis=1))
                return t * cos + partner * sin

            lq = rot(lq)
            lk = rot(lk)
        lq_ref[0, rows, :] = lq
        lk_ref[0, rows, :] = lk
        lv_ref[0, rows, :] = proj(OFF_LA_V, GLA_WIDTH)
        lg_ref[0, rows, :] = proj(OFF_LA_G, GLA_WIDTH)

        lr = proj(OFF_LR, 2 * GLA_GATE_RANK)
        z = _dot(lr.astype(BF16), wg_ref[...]) + bg_ref[...]
        log_sig = jnp.minimum(z, 0.0) - jnp.log1p(jnp.exp(-jnp.abs(z)))
        la_ref[0, rows, :] = log_sig * (LOG2_E / GLA_GATE_TAU)

    hb_next = normed(0)
    for i in range(tm // sub):
        hb = hb_next
        hb_next = normed(i + 1) if (i + 1) * sub < tm else None
        project(i, hb)


def _in_proj(x, mods_l, row_of_batch, lw, rope_tabs, kv_f32):
    B, L, _ = x.shape
    tm = min(IN_ROW_TILE, L)
    rope = rope_tabs is not None
    const = lambda b, i: (0, 0)
    tile = lambda b, i: (b, i, 0)

    in_specs = [
        pl.BlockSpec((1, tm, D_MODEL), tile),
        pl.BlockSpec((1, N_MODS, D_MODEL), lambda b, i: (row_of_batch(b), 0, 0)),
        pl.BlockSpec((1, D_MODEL), const),
        pl.BlockSpec((D_MODEL, IN_WIDTH), const),
        pl.BlockSpec((1, NA_WIDTH), const),
        pl.BlockSpec((1, NA_WIDTH), const),
        pl.BlockSpec((2 * GLA_GATE_RANK, 2 * GLA_KEY_WIDTH), const),
        pl.BlockSpec((1, 2 * GLA_KEY_WIDTH), const),
    ]
    args = [x, mods_l, lw["norm1_gain"], lw["w_in"], lw["q_gain"], lw["k_gain"], lw["w_gate"], lw["b_gate"]]
    if rope:
        in_specs += [pl.BlockSpec((tm, GLA_KEY_WIDTH), lambda b, i: (i, 0))] * 2
        args += list(rope_tabs)

    widths = [(POOL_WIDTH, F32), (NA_WIDTH, BF16), (NA_WIDTH, BF16), (NA_WIDTH, BF16),
              (GLA_KEY_WIDTH, F32), (GLA_KEY_WIDTH, F32), (GLA_WIDTH, F32), (GLA_WIDTH, F32),
              (2 * GLA_KEY_WIDTH, F32)]
    if kv_f32:
        widths += [(NA_WIDTH, F32), (NA_WIDTH, F32)]
    out_specs = [pl.BlockSpec((1, tm, w), tile) for w, _ in widths]
    out_shape = [jax.ShapeDtypeStruct((B, L, w), dt) for w, dt in widths]

    return pl.pallas_call(
        functools.partial(_inproj_kernel, rope=rope, kv_f32=kv_f32),
        grid=(B, L // tm),
        in_specs=in_specs,
        out_specs=out_specs,
        out_shape=out_shape,
        compiler_params=pltpu.CompilerParams(vmem_limit_bytes=VMEM_LIMIT),
        name="in_proj",
    )(*args)


def _pool_kernel(u_ref, w_ref, sc_ref, o_ref, pad_ref, *, L, tp):
    H = POOL_HALO
    zeros = jnp.zeros((H, POOL_WIDTH), F32)
    pad_ref[0:H, :] = zeros
    pad_ref[H + L:H + L + H, :] = zeros
    pad_ref[H:H + L, :] = u_ref[0]
    lane = lax.broadcasted_iota(jnp.int32, (1, LANES), 1)
    low = lane < POOL_GROUP_DIM

    for base in range(0, L, tp):
        t = base + lax.broadcasted_iota(jnp.int32, (tp, 1), 0)

        def count(win):
            return (jnp.minimum(t + win // 2, L) - jnp.maximum(t - win // 2, 0)).astype(F32)

        def shifted(off, col):
            return pad_ref[H + base + off:H + base + off + tp, col * LANES:(col + 1) * LANES]

        u_a = shifted(0, 0)
        w2 = shifted(-1, 0) + u_a
        w4 = w2 + shifted(-2, 0) + shifted(1, 0)
        mean_a = jnp.where(low, w2 / count(2), w4 / count(4))
        u_b = shifted(0, 1)
        w8 = u_b
        for off in (-4, -3, -2, -1, 1, 2, 3):
            w8 = w8 + shifted(off, 1)
        w16 = w8
        for off in (-8, -7, -6, -5, 4, 5, 6, 7):
            w16 = w16 + shifted(off, 1)
        mean_b = jnp.where(low, w8 / count(8), w16 / count(16))
        d = jnp.concatenate([mean_a - u_a, mean_b - u_b], axis=-1).astype(BF16)
        y = _dot(d, w_ref[...]) * sc_ref[...]
        o_ref[0, base:base + tp, :] = y.astype(BF16)


def _pool(u, lw):
    B, L, _ = u.shape
    tp = min(512, L)
    return pl.pallas_call(
        functools.partial(_pool_kernel, L=L, tp=tp),
        grid=(B,),
        in_specs=[
            pl.BlockSpec((1, L, POOL_WIDTH), lambda b: (b, 0, 0)),
            pl.BlockSpec((POOL_WIDTH, POOL_WIDTH), lambda b: (0, 0)),
            pl.BlockSpec((1, POOL_WIDTH), lambda b: (0, 0)),
        ],
        out_specs=pl.BlockSpec((1, L, POOL_WIDTH), lambda b: (b, 0, 0)),
        out_shape=jax.ShapeDtypeStruct((B, L, POOL_WIDTH), BF16),
        scratch_shapes=[pltpu.VMEM((L + 2 * POOL_HALO, POOL_WIDTH), F32)],
        compiler_params=pltpu.CompilerParams(vmem_limit_bytes=VMEM_LIMIT),
        name="pool",
    )(u, lw["w_pool_bd"], lw["pool_scale"])


def _ctx_attn_kernel(q_ref, k_ref, v_ref, o_ref):
    q = q_ref[0]
    k = k_ref[0]
    v = v_ref[0]
    outs = []
    for h in range(NA_HEADS):
        sl = slice(h * NA_HEAD_DIM, (h + 1) * NA_HEAD_DIM)
        s = _dot_nt(q[:, sl], k[:, sl])
        p = jnp.exp2(s - jnp.max(s, axis=-1, keepdims=True))
        denom = jnp.sum(p, axis=-1, keepdims=True)
        outs.append(_dot(p.astype(BF16), v[:, sl]) / denom)
    o_ref[0] = jnp.concatenate(outs, axis=-1).astype(BF16)


def _ctx_attn(q, k, v):
    B, L, _ = q.shape
    spec = pl.BlockSpec((1, L, NA_WIDTH), lambda b: (b, 0, 0))
    return pl.pallas_call(
        _ctx_attn_kernel,
        grid=(B,),
        in_specs=[spec, spec, spec],
        out_specs=spec,
        out_shape=jax.ShapeDtypeStruct((B, L, NA_WIDTH), BF16),
        name="ctx_attn",
    )(q, k, v)


def _band_start(r, n_rows):
    return jnp.clip(r - NA_WIN_ROWS // 2, 0, n_rows - NA_WIN_ROWS)


def _lat_attn_kernel(q_ref, k_ref, v_ref, kc_ref, vc_ref, bias_ref, o_ref, *, n_rows):
    band = NA_WIN_ROWS * GRID_W
    lane = lax.broadcasted_iota(jnp.int32, (1, LANES), 1)
    even = lane < NA_HEAD_DIM
    zero = jnp.zeros((), BF16)

    def band_of(rr):
        r = pl.program_id(1) * NA_ROWS_PER_STEP + rr
        first = _band_start(r, n_rows)
        return pl.multiple_of(first * GRID_W, GRID_W), NA_WIN_ROWS - 1 - (r - first)

    def scores(rr, j):
        start, tile0 = band_of(rr)
        cols = slice(j * LANES, (j + 1) * LANES)
        qp = q_ref[0, rr * GRID_W:(rr + 1) * GRID_W, cols]
        q2 = jnp.concatenate([jnp.where(even, qp, zero), jnp.where(even, zero, qp)], axis=0)
        bias = jnp.concatenate(
            [jnp.concatenate([bias_ref[2 * j + hh, tile0 + 2 * ii] for ii in range(NA_WIN_ROWS // 2)], axis=-1)
             for hh in range(2)], axis=0)
        return _dot_nt(q2, k_ref[0, pl.ds(start, band), cols]) + bias, _dot_nt(q2, kc_ref[0, :, cols])

    def softmax(s_loc, s_ctx):
        m = jnp.maximum(jnp.max(s_loc, axis=-1, keepdims=True), jnp.max(s_ctx, axis=-1, keepdims=True))
        p_loc = jnp.exp2(s_loc - m)
        p_ctx = jnp.exp2(s_ctx - m)
        denom = jnp.sum(p_loc, axis=-1, keepdims=True) + jnp.sum(p_ctx, axis=-1, keepdims=True)
        return p_loc.astype(BF16), p_ctx.astype(BF16), denom

    def values(rr, j, p_loc, p_ctx, denom):
        start, _ = band_of(rr)
        cols = slice(j * LANES, (j + 1) * LANES)
        o2 = (_dot(p_loc, v_ref[0, pl.ds(start, band), cols]) + _dot(p_ctx, vc_ref[0, :, cols])) / denom
        o_ref[0, rr * GRID_W:(rr + 1) * GRID_W, cols] = jnp.where(even, o2[:GRID_W], o2[GRID_W:]).astype(BF16)

    chains = [(rr, j) for rr in range(NA_ROWS_PER_STEP) for j in range(NA_HEADS // 2)]
    s_next = scores(*chains[0])
    p_prev = None
    for i, chain in enumerate(chains):
        s_cur = s_next
        s_next = scores(*chains[i + 1]) if i + 1 < len(chains) else None
        p_cur = softmax(*s_cur)
        if p_prev is not None:
            values(*chains[i - 1], *p_prev)
        p_prev = p_cur
    values(*chains[-1], *p_prev)


def _lat_attn(q, k, v, kc, vc, bias_tiles):
    B, L, _ = q.shape
    n_rows = L // GRID_W
    assert n_rows >= NA_WIN_ROWS and n_rows % NA_ROWS_PER_STEP == 0
    past = kc.shape[1]
    whole = pl.BlockSpec((1, L, NA_WIDTH), lambda b, r: (b, 0, 0))
    rows = pl.BlockSpec((1, NA_ROWS_PER_STEP * GRID_W, NA_WIDTH), lambda b, r: (b, r, 0))
    ctx = pl.BlockSpec((1, past, NA_WIDTH), lambda b, r: (b, 0, 0))
    bias = pl.BlockSpec(bias_tiles.shape, lambda b, r: (0, 0, 0, 0))
    return pl.pallas_call(
        functools.partial(_lat_attn_kernel, n_rows=n_rows),
        grid=(B, n_rows // NA_ROWS_PER_STEP),
        in_specs=[rows, whole, whole, ctx, ctx, bias],
        out_specs=rows,
        out_shape=jax.ShapeDtypeStruct((B, L, NA_WIDTH), BF16),
        compiler_params=pltpu.CompilerParams(vmem_limit_bytes=VMEM_LIMIT),
        name="lat_attn",
    )(q, k, v, kc, vc, bias_tiles)


def _bias_tile_constants():
    n_dc = 2 * NA_WIN_COLS
    w = np.arange(GRID_W)[:, None]
    cc = np.arange(2 * GRID_W)[None, :]
    c = cc % GRID_W
    cs = np.clip(w - NA_WIN_COLS // 2, 0, GRID_W - NA_WIN_COLS)
    valid = (c >= cs) & (c < cs + NA_WIN_COLS)
    k_idx = (cc // GRID_W) * n_dc + (c - w + NA_WIN_COLS - 1)
    onehot = (np.arange(2 * n_dc)[:, None, None] == k_idx[None]) & valid[None]
    onehot = onehot.reshape(2 * n_dc, GRID_W * 2 * GRID_W)
    mask = np.where(valid, 0.0, MASK_VALUE).reshape(1, GRID_W * 2 * GRID_W)
    return jnp.asarray(onehot, BF16), jnp.asarray(mask, F32)


def _bias_tiles_kernel(rb_ref, oh_ref, mask_ref, o_ref):
    hi, mid, lo = _split3(rb_ref[...])
    oh = oh_ref[...]
    o_ref[...] = (_dot(hi, oh) + _dot(mid, oh) + _dot(lo, oh)) * LOG2_E + mask_ref[...]


def _bias_tiles(rel_bias_l):
    n_dr = 2 * NA_WIN_ROWS - 1
    padded = jnp.pad(rel_bias_l, ((0, 0), (0, 0), (0, 1)))
    pairs = jnp.concatenate([padded[:, :-1], padded[:, 1:]], axis=-1)
    pairs = pairs.reshape(NA_HEADS * (n_dr - 1), 4 * NA_WIN_COLS)
    onehot, mask = _bias_tile_constants()
    full = lambda shape: pl.BlockSpec(shape, lambda: (0,) * len(shape))
    out = pl.pallas_call(
        _bias_tiles_kernel,
        in_specs=[full(pairs.shape), full(onehot.shape), full(mask.shape)],
        out_specs=full((pairs.shape[0], onehot.shape[1])),
        out_shape=jax.ShapeDtypeStruct((pairs.shape[0], onehot.shape[1]), F32),
        name="bias_tiles",
    )(pairs, onehot, mask)
    return out.reshape(NA_HEADS, n_dr - 1, GRID_W, 2 * GRID_W)


def _gla_block(q_ref, k_ref, v_ref, b_ref, st_ref, o_ref, expand, head_mask, bi, r0, reverse):
    half = GLA_BLOCK // 2
    rows = pl.ds(r0, GLA_BLOCK)
    qb = q_ref[bi, rows, :]
    kb = k_ref[bi, rows, :]
    vb = v_ref[bi, rows, :]
    bb = b_ref[bi, rows, :]
    end_row = 0 if reverse else GLA_BLOCK - 1
    b_end = bb[end_row:end_row + 1, :]
    st = st_ref[bi]

    o_inter = _dot_nt((qb * jnp.exp2(bb)).astype(BF16), st.astype(BF16))

    t_in = lax.broadcasted_iota(jnp.int32, (half, 1), 0)
    q_half = (qb[:half], qb[half:])
    b_half = (bb[:half], bb[half:])
    pieces, owners = [], []
    for s in range(GLA_BLOCK):
        hs = s // half
        k_s = kb[s:s + 1, :]
        b_s = bb[s:s + 1, :]
        for ht in range(2):
            if (ht > hs) if reverse else (ht < hs):
                continue
            decay = jnp.exp2(b_half[ht] - b_s)
            if ht == hs:
                seen = (t_in <= s - hs * half) if reverse else (t_in >= s - hs * half)
                decay = jnp.where(seen, decay, 0.0)
            pieces.append(q_half[ht] * k_s * decay)
            owners.append((ht, s))
    a_exp = _dot(jnp.concatenate(pieces, axis=0).astype(BF16), expand)
    o_intra = [jnp.zeros((half, GLA_WIDTH), F32), jnp.zeros((half, GLA_WIDTH), F32)]
    for i, (ht, s) in enumerate(owners):
        o_intra[ht] = o_intra[ht] + a_exp[i * half:(i + 1) * half, :] * vb[s:s + 1, :]
    o_ref[bi, rows, :] = o_inter + jnp.concatenate(o_intra, axis=0)

    k_hat = (kb * jnp.exp2(b_end - bb)).astype(BF16)
    upd = _dot_tn(vb.astype(BF16), k_hat)
    st_ref[bi] = st * jnp.exp2(b_end) + upd * head_mask


def _gla_kernel(qf_ref, kf_ref, vf_ref, laf_ref, qb_ref, kb_ref, vb_ref, lab_ref, s0f_ref, s0b_ref,
                trif_ref, trib_ref, e_ref, of_ref, ob_ref, sff_ref, sfb_ref,
                stf_ref, stb_ref, bf_ref, bb_ref, *, tm, nb):
    n = pl.program_id(1)
    n_blocks = tm // GLA_BLOCK

    @pl.when(n == 0)
    def _():
        stf_ref[...] = s0f_ref[...]
        stb_ref[...] = s0b_ref[...]

    for la_ref, tri_ref, b_ref in ((laf_ref, trif_ref, bf_ref), (lab_ref, trib_ref, bb_ref)):
        tri = tri_ref[...]
        for bi in range(nb):
            g_hi, g_mid, _ = _split3(la_ref[bi])
            sums = _dot(tri, jnp.concatenate([g_hi, g_mid], axis=-1))
            b_ref[bi] = sums[:, :GLA_KEY_WIDTH] + sums[:, GLA_KEY_WIDTH:]

    expand = e_ref[...]
    head_v = lax.broadcasted_iota(jnp.int32, (GLA_WIDTH, GLA_KEY_WIDTH), 0) // GLA_DV
    head_k = lax.broadcasted_iota(jnp.int32, (GLA_WIDTH, GLA_KEY_WIDTH), 1) // GLA_DK
    head_mask = jnp.where(head_v == head_k, 1.0, 0.0)

    def block(j, carry):
        r_f = pl.multiple_of(j * GLA_BLOCK, GLA_BLOCK)
        r_b = pl.multiple_of((n_blocks - 1 - j) * GLA_BLOCK, GLA_BLOCK)
        for bi in range(nb):
            _gla_block(qf_ref, kf_ref, vf_ref, bf_ref, stf_ref, of_ref, expand, head_mask, bi, r_f, reverse=False)
            _gla_block(qb_ref, kb_ref, vb_ref, bb_ref, stb_ref, ob_ref, expand, head_mask, bi, r_b, reverse=True)
        return carry

    lax.fori_loop(0, n_blocks, block, 0)

    @pl.when(n == pl.num_programs(1) - 1)
    def _():
        sff_ref[...] = stf_ref[...]
        sfb_ref[...] = stb_ref[...]


def _block_tri(tm, reverse):
    t = np.arange(tm)[:, None]
    s = np.arange(tm)[None, :]
    same = (t // GLA_BLOCK) == (s // GLA_BLOCK)
    return jnp.asarray(same & ((s >= t) if reverse else (s <= t)), BF16)


def _gla_scan(q, k, v, la, s0f_t, s0b_t, expand):
    B, L, _ = q.shape
    tm = min(GLA_TILE, L)
    n_tiles = L // tm
    nb = math.gcd(B, GLA_BATCH_PER_STEP)
    fwd = lambda b, n: (b, n, 0)
    bwd = lambda b, n: (b, n_tiles - 1 - n, 0)
    state = pl.BlockSpec((nb, GLA_WIDTH, GLA_KEY_WIDTH), lambda b, n: (b, 0, 0))
    const = lambda shape: pl.BlockSpec(shape, lambda b, n: (0, 0))

    def operands(tmap, la_col):
        return [pl.BlockSpec((nb, tm, GLA_KEY_WIDTH), tmap), pl.BlockSpec((nb, tm, GLA_KEY_WIDTH), tmap),
                pl.BlockSpec((nb, tm, GLA_WIDTH), tmap),
                pl.BlockSpec((nb, tm, GLA_KEY_WIDTH), lambda b, n: tmap(b, n)[:2] + (la_col,))]

    return pl.pallas_call(
        functools.partial(_gla_kernel, tm=tm, nb=nb),
        grid=(B // nb, n_tiles),
        in_specs=operands(fwd, 0) + operands(bwd, 1) + [state, state, const((tm, tm)), const((tm, tm)),
                                                        const((GLA_KEY_WIDTH, GLA_WIDTH))],
        out_specs=[pl.BlockSpec((nb, tm, GLA_WIDTH), fwd), pl.BlockSpec((nb, tm, GLA_WIDTH), bwd), state, state],
        out_shape=[jax.ShapeDtypeStruct((B, L, GLA_WIDTH), F32), jax.ShapeDtypeStruct((B, L, GLA_WIDTH), F32),
                   jax.ShapeDtypeStruct((B, GLA_WIDTH, GLA_KEY_WIDTH), F32),
                   jax.ShapeDtypeStruct((B, GLA_WIDTH, GLA_KEY_WIDTH), F32)],
        scratch_shapes=[pltpu.VMEM((nb, GLA_WIDTH, GLA_KEY_WIDTH), F32), pltpu.VMEM((nb, GLA_WIDTH, GLA_KEY_WIDTH), F32),
                        pltpu.VMEM((nb, tm, GLA_KEY_WIDTH), F32), pltpu.VMEM((nb, tm, GLA_KEY_WIDTH), F32)],
        compiler_params=pltpu.CompilerParams(vmem_limit_bytes=VMEM_LIMIT),
        name="gla",
    )(q, k, v, la, q, k, v, la, s0f_t, s0b_t, _block_tri(tm, False), _block_tri(tm, True), expand)


def _state_to_kernel(s):
    B = s.shape[0]
    same_head = np.eye(GLA_HEADS, dtype=bool)[None, :, None, :, None]
    st = jnp.where(same_head, s.transpose(0, 1, 3, 2)[:, :, :, None, :], 0.0)
    return st.reshape(B, GLA_WIDTH, GLA_KEY_WIDTH)


def _state_from_kernel(st):
    B = st.shape[0]
    blocks = st.reshape(B, GLA_HEADS, GLA_DV, GLA_HEADS, GLA_DK)
    diag = jnp.stack([blocks[:, h, :, h, :] for h in range(GLA_HEADS)], axis=1)
    return diag.transpose(0, 1, 3, 2)


def _post_kernel(x_ref, m_ref, yp_ref, yn_ref, of_ref, ob_ref, lg_ref, gg_ref, g2_ref,
                 wo_ref, wi_ref, wf_ref, o_ref, acc_ref):
    gate1 = m_ref[0, 2:3, :]
    shift2 = m_ref[0, 3:4, :]
    scale2 = m_ref[0, 4:5, :]
    gate2 = m_ref[0, 5:6, :]

    o_la = _half_lane_rms(of_ref[0] + ob_ref[0], NORM_EPS) * gg_ref[...]
    y_la = (o_la * _silu(lg_ref[0])).astype(BF16)
    mixed = (_dot(yp_ref[0], wo_ref[0:POOL_WIDTH, :])
             + _dot(yn_ref[0], wo_ref[POOL_WIDTH:POOL_WIDTH + NA_WIDTH, :])
             + _dot(y_la, wo_ref[POOL_WIDTH + NA_WIDTH:D_MODEL, :]))
    x1 = x_ref[0] + gate1 * mixed

    h2 = (_row_rms(x1, NORM_EPS) * g2_ref[...] * (1.0 + scale2) + shift2).astype(BF16)
    for c0 in range(0, D_FF, FF_CHUNK):
        hg = _dot(h2, wi_ref[:, c0:c0 + FF_CHUNK])
        hu = _dot(h2, wi_ref[:, D_FF + c0:D_FF + c0 + FF_CHUNK])
        part = _dot((_silu(hg) * hu).astype(BF16), wf_ref[c0:c0 + FF_CHUNK, :])
        if c0 == 0:
            acc_ref[...] = part
        else:
            acc_ref[...] += part
    o_ref[0] = x1 + gate2 * acc_ref[...]


def _post(x, mods_l, row_of_batch, y_pool, y_na, o_f, o_b, lg, lw):
    B, L, _ = x.shape
    tm = min(ROW_TILE, L)
    const = lambda b, i: (0, 0)
    tile = lambda b, i: (b, i, 0)

    def resident(shape):
        return pl.BlockSpec(shape, const, pipeline_mode=pl.Buffered(1))

    return pl.pallas_call(
        _post_kernel,
        grid=(B, L // tm),
        in_specs=[
            pl.BlockSpec((1, tm, D_MODEL), tile),
            pl.BlockSpec((1, N_MODS, D_MODEL), lambda b, i: (row_of_batch(b), 0, 0)),
            pl.BlockSpec((1, tm, POOL_WIDTH), tile),
            pl.BlockSpec((1, tm, NA_WIDTH), tile),
            pl.BlockSpec((1, tm, GLA_WIDTH), tile),
            pl.BlockSpec((1, tm, GLA_WIDTH), tile),
            pl.BlockSpec((1, tm, GLA_WIDTH), tile),
            pl.BlockSpec((1, GLA_WIDTH), const),
            pl.BlockSpec((1, D_MODEL), const),
            resident((D_MODEL, D_MODEL)),
            resident((D_MODEL, 2 * D_FF)),
            resident((D_FF, D_MODEL)),
        ],
        out_specs=pl.BlockSpec((1, tm, D_MODEL), tile),
        out_shape=jax.ShapeDtypeStruct((B, L, D_MODEL), F32),
        scratch_shapes=[pltpu.VMEM((tm, D_MODEL), F32)],
        compiler_params=pltpu.CompilerParams(vmem_limit_bytes=VMEM_LIMIT),
        name="post",
    )(x, mods_l, y_pool, y_na, o_f, o_b, lg, lw["gla_gain"], lw["norm2_gain"],
      lw["w_out"], lw["w_ffn_in"], lw["w_ffn_out"])


def _rope_tables(L):
    t = jnp.arange(L)
    row = (t // GRID_W).astype(F32)
    col = (t % GRID_W).astype(F32)
    half = GLA_DK // 2
    inv_freq = ROPE_THETA ** (-jnp.arange(0, half, 2, dtype=F32) / half)
    ang_r = row[:, None] * inv_freq
    ang_c = col[:, None] * inv_freq
    sign = jnp.concatenate([-jnp.ones((half // 2,), F32), jnp.ones((half // 2,), F32)])

    def lanes(fn, signed):
        per_axis = []
        for ang in (ang_r, ang_c):
            v = jnp.concatenate([fn(ang), fn(ang)], axis=-1)
            per_axis.append(v * sign if signed else v)
        return jnp.tile(jnp.concatenate(per_axis, axis=-1), (1, GLA_HEADS))

    return lanes(jnp.cos, False), lanes(jnp.sin, True)


def _layer_weights(l, w):
    tile_heads = lambda g, n: jnp.tile(g, n)[None, :]
    return {
        "norm1_gain": w["norm1_gain"][l][None, :],
        "norm2_gain": w["norm2_gain"][l][None, :],
        "w_in": w["w_in"][l].astype(BF16),
        "q_gain": tile_heads(w["q_norm_gain"][l], NA_HEADS),
        "k_gain": tile_heads(w["k_norm_gain"][l], NA_HEADS),
        "w_gate": jax.scipy.linalg.block_diag(w["w_gate_f"][l], w["w_gate_b"][l]).astype(BF16),
        "b_gate": jnp.concatenate([w["b_gate_f"][l], w["b_gate_b"][l]])[None, :],
        "w_pool_bd": jax.scipy.linalg.block_diag(*[w["w_pool"][l, g] for g in range(len(POOL_WINDOWS))]).astype(BF16),
        "pool_scale": w["pool_scale"][l][None, :],
        "gla_gain": tile_heads(w["gla_norm_gain"][l], GLA_HEADS),
        "w_out": w["w_out"][l].astype(BF16),
        "w_ffn_in": w["w_ffn_in"][l].astype(BF16),
        "w_ffn_out": w["w_ffn_out"][l].astype(BF16),
    }


def _trunk_layer(x, mods_l, row_of_batch, lw, expand, s0_f, s0_b, latent):
    is_ctx = latent is None
    B, L, _ = x.shape
    per_token = (lambda a: a.reshape(1, B * L, a.shape[-1])) if is_ctx else (lambda a: a)
    per_seq = lambda a: a.reshape(B, L, a.shape[-1])
    outs = _in_proj(per_token(x), mods_l, row_of_batch, lw, None if is_ctx else latent[0], kv_f32=is_ctx)
    outs = [per_seq(a) for a in outs]
    u, q, k, v, lq, lk, lv, lg, la = outs[:9]
    y_pool = _pool(u, lw)
    if is_ctx:
        y_na = _ctx_attn(q, k, v)
    else:
        y_na = _lat_attn(q, k, v, latent[1], latent[2], latent[3])
    o_f, o_b, s_f, s_b = _gla_scan(lq, lk, lv, la, s0_f, s0_b, expand)
    x = per_seq(_post(per_token(x), mods_l, row_of_batch,
                      *[per_token(a) for a in (y_pool, y_na, o_f, o_b, lg)], lw))
    extras = (outs[9], outs[10]) if is_ctx else ()
    return x, s_f, s_b, extras


def kernel(x_prompt, x_sample, c, cache_na_k, cache_na_v, state_gla_fwd, state_gla_bwd, c_ctx, w_ada, b_ada, norm1_gain, norm2_gain, w_in, w_pool, pool_scale, q_norm_gain, k_norm_gain, rel_bias, w_gate_f, b_gate_f, w_gate_b, b_gate_b, gla_norm_gain, w_out, w_ffn_in, w_ffn_out):
    weights = dict(norm1_gain=norm1_gain, norm2_gain=norm2_gain, w_in=w_in, w_pool=w_pool, pool_scale=pool_scale,
                   q_norm_gain=q_norm_gain, k_norm_gain=k_norm_gain, w_gate_f=w_gate_f, b_gate_f=b_gate_f,
                   w_gate_b=w_gate_b, b_gate_b=b_gate_b, gla_norm_gain=gla_norm_gain, w_out=w_out,
                   w_ffn_in=w_ffn_in, w_ffn_out=w_ffn_out)
    B_ctx, L_ctx, _ = x_prompt.shape
    B_lat, L_lat, _ = x_sample.shape
    assert 1 + B_lat <= COND_ROWS

    conds = jnp.zeros((COND_ROWS, D_MODEL), F32).at[0].set(c_ctx).at[1:1 + B_lat].set(c)
    mods = _ada(conds, w_ada, b_ada).reshape(DEPTH, COND_ROWS, N_MODS, D_MODEL)
    lws = [_layer_weights(l, weights) for l in range(DEPTH)]
    head_of_k = np.arange(GLA_KEY_WIDTH) // GLA_DK
    head_of_v = np.arange(GLA_WIDTH) // GLA_DV
    expand = jnp.asarray(head_of_k[:, None] == head_of_v[None, :], BF16)

    xp = x_prompt
    zero_state = jnp.zeros((B_ctx, GLA_WIDTH, GLA_KEY_WIDTH), F32)
    ks, vs, sfs, sbs = [], [], [], []
    for l in range(DEPTH):
        xp, s_f, s_b, (k_f32, v_f32) = _trunk_layer(xp, mods[l], lambda b: 0, lws[l], expand,
                                                     zero_state, zero_state, None)
        to_heads = lambda a: a.reshape(B_ctx, L_ctx, NA_HEADS, NA_HEAD_DIM).transpose(0, 2, 1, 3)
        ks.append(to_heads(k_f32))
        vs.append(to_heads(v_f32))
        sfs.append(_state_from_kernel(s_f))
        sbs.append(_state_from_kernel(s_b))

    xs = x_sample
    rope_tabs = _rope_tables(L_lat)
    from_heads = lambda a: a.transpose(0, 2, 1, 3).reshape(B_lat, a.shape[2], NA_WIDTH).astype(BF16)
    for l in range(DEPTH):
        latent = (rope_tabs, from_heads(cache_na_k[:, l]), from_heads(cache_na_v[:, l]), _bias_tiles(rel_bias[l]))
        xs, _, _, _ = _trunk_layer(xs, mods[l], lambda b: b + 1, lws[l], expand,
                                   _state_to_kernel(state_gla_fwd[:, l]), _state_to_kernel(state_gla_bwd[:, l]),
                                   latent)

    return (xp, xs, jnp.stack(ks, axis=1), jnp.stack(vs, axis=1), jnp.stack(sfs, axis=1), jnp.stack(sbs, axis=1))
```

```python
import functools
import math

import numpy as np
import jax
import jax.numpy as jnp
from jax import lax
from jax.experimental import pallas as pl
from jax.experimental.pallas import tpu as pltpu

F32 = jnp.float32
BF16 = jnp.bfloat16

D_MODEL = 1024
DEPTH = 2
GRID_W = 64
POOL_WIDTH = 256
POOL_GROUP_DIM = 64
POOL_WINDOWS = (2, 4, 8, 16)
POOL_HALO = max(POOL_WINDOWS) // 2
POOL_PAD = 8 * len(POOL_WINDOWS)
NA_HEADS = 8
NA_HEAD_DIM = 64
NA_WIDTH = NA_HEADS * NA_HEAD_DIM
NA_WIN_ROWS = 8
NA_WIN_COLS = 16
NA_ROWS_PER_STEP = 8
GLA_HEADS = 4
GLA_DV = 64
GLA_DK = 32
GLA_WIDTH = GLA_HEADS * GLA_DV
GLA_KEY_WIDTH = GLA_HEADS * GLA_DK
GLA_GATE_RANK = 16
GLA_GATE_TAU = 16.0
GLA_BLOCK = 16
ROPE_THETA = 10000.0
D_FF = 2816
NORM_EPS = 1e-6
N_MODS = 6
COND_ROWS = 16
MASK_VALUE = -1e30
LOG2_E = 1.4426950408889634

OFF_POOL = 0
OFF_NA_Q = OFF_POOL + POOL_WIDTH
OFF_NA_K = OFF_NA_Q + NA_WIDTH
OFF_NA_V = OFF_NA_K + NA_WIDTH
OFF_LA_Q = OFF_NA_V + NA_WIDTH
OFF_LA_K = OFF_LA_Q + GLA_KEY_WIDTH
OFF_LA_V = OFF_LA_K + GLA_KEY_WIDTH
OFF_LA_G = OFF_LA_V + GLA_WIDTH
OFF_LR = OFF_LA_G + GLA_WIDTH
IN_WIDTH = OFF_LR + 2 * GLA_GATE_RANK

LANES = 128
ROW_TILE = 512
IN_ROW_TILE = 1024
IN_SUB_TILE = 256
GLA_TILE = 256
GLA_BATCH_PER_STEP = 4
FF_CHUNK = 256
ADA_COL_TILE = 512
VMEM_LIMIT = 56 * 1024 * 1024


def _dot(a, b):
    return jnp.dot(a, b, preferred_element_type=F32)


def _dot_nt(a, b):
    return lax.dot_general(a, b, (((1,), (1,)), ((), ())), preferred_element_type=F32)


def _dot_tn(a, b):
    return lax.dot_general(a, b, (((0,), (0,)), ((), ())), preferred_element_type=F32)


def _silu(x):
    return x * (1.0 / (1.0 + jnp.exp(-x)))


def _split3(x):
    hi = x.astype(BF16)
    r1 = x - hi.astype(F32)
    mid = r1.astype(BF16)
    lo = (r1 - mid.astype(F32)).astype(BF16)
    return hi, mid, lo


def _half_lane_rms(x, eps):
    lane = lax.broadcasted_iota(jnp.int32, (1, LANES), 1)
    low = lane < NA_HEAD_DIM
    cols = []
    for j in range(x.shape[-1] // LANES):
        blk = x[:, j * LANES:(j + 1) * LANES]
        sq = blk * blk
        s_lo = jnp.sum(jnp.where(low, sq, 0.0), axis=-1, keepdims=True)
        s_hi = jnp.sum(jnp.where(low, 0.0, sq), axis=-1, keepdims=True)
        r_lo = lax.rsqrt(s_lo * (1.0 / NA_HEAD_DIM) + eps)
        r_hi = lax.rsqrt(s_hi * (1.0 / NA_HEAD_DIM) + eps)
        cols.append(blk * jnp.where(low, r_lo, r_hi))
    return jnp.concatenate(cols, axis=-1)


def _row_rms(x, eps):
    return x * lax.rsqrt(jnp.mean(x * x, axis=-1, keepdims=True) + eps)


def _ada_kernel(c_ref, w_ref, b_ref, o_ref):
    s_hi, s_mid, _ = _split3(_silu(c_ref[...]))
    w = w_ref[0]
    w_hi, w_mid, _ = _split3(w)
    acc = _dot(s_hi, w_hi) + _dot(s_mid, w_hi) + _dot(s_hi, w_mid)
    o_ref[0] = acc + b_ref[0]


def _ada(conds, w_ada, b_ada):
    n_out = w_ada.shape[-1]
    return pl.pallas_call(
        _ada_kernel,
        grid=(DEPTH, n_out // ADA_COL_TILE),
        in_specs=[
            pl.BlockSpec((COND_ROWS, D_MODEL), lambda l, j: (0, 0)),
            pl.BlockSpec((1, D_MODEL, ADA_COL_TILE), lambda l, j: (l, 0, j)),
            pl.BlockSpec((1, 1, ADA_COL_TILE), lambda l, j: (l, 0, j)),
        ],
        out_specs=pl.BlockSpec((1, COND_ROWS, ADA_COL_TILE), lambda l, j: (l, 0, j)),
        out_shape=jax.ShapeDtypeStruct((DEPTH, COND_ROWS, n_out), F32),
        name="ada",
    )(conds, w_ada, b_ada.reshape(DEPTH, 1, n_out))


def _inproj_kernel(*refs, rope, kv_seq):
    x_ref, m_ref, g1_ref, w_ref, qg_ref, kg_ref, wg_ref, bg_ref = refs[:8]
    pos = 8
    if rope:
        cos_ref, sin_ref = refs[pos:pos + 2]
        pos += 2
    if kv_seq:
        pos += 2
    u_ref, q_ref, k_ref, v_ref, lq_ref, lk_ref, lv_ref, lg_ref, la_ref = refs[pos:pos + 9]
    pos += 9
    if kv_seq:
        kf_ref, vf_ref = refs[pos:pos + 2]

    shift1 = m_ref[0, 0:1, :]
    scale1 = m_ref[0, 1:2, :]
    tm = x_ref.shape[1]
    sub = min(IN_SUB_TILE, tm)

    def normed(i):
        rows = slice(i * sub, (i + 1) * sub)
        h = _row_rms(x_ref[0, rows, :], NORM_EPS) * g1_ref[...] * (1.0 + scale1) + shift1
        return h.astype(BF16)

    def project(i, hb):
        rows = slice(i * sub, (i + 1) * sub)

        def proj(off, width):
            return _dot(hb, w_ref[:, off:off + width])

        u_ref[0, rows, :] = proj(OFF_POOL, POOL_WIDTH)

        qn = _half_lane_rms(proj(OFF_NA_Q, NA_WIDTH), NORM_EPS) * qg_ref[...]
        q_ref[0, rows, :] = (qn * (NA_HEAD_DIM ** -0.5 * LOG2_E)).astype(BF16)
        kn = _half_lane_rms(proj(OFF_NA_K, NA_WIDTH), NORM_EPS) * kg_ref[...]
        k_ref[0, rows, :] = kn.astype(BF16)
        vn = proj(OFF_NA_V, NA_WIDTH)
        v_ref[0, rows, :] = vn.astype(BF16)
        if kv_seq:
            piece = min(sub, kv_seq)
            for j in range(sub // piece):
                first = i * sub + j * piece
                seq, off = (first // kv_seq, first % kv_seq) if tm >= kv_seq else (0, first)
                for hh in range(NA_HEADS):
                    cols = slice(hh * NA_HEAD_DIM, (hh + 1) * NA_HEAD_DIM)
                    kf_ref[seq, 0, hh, off:off + piece, :] = kn[j * piece:(j + 1) * piece, cols]
                    vf_ref[seq, 0, hh, off:off + piece, :] = vn[j * piece:(j + 1) * piece, cols]

        lqk = proj(OFF_LA_Q, 2 * GLA_KEY_WIDTH)
        lq = lqk[:, :GLA_KEY_WIDTH] * (GLA_DK ** -0.5)
        lk = lqk[:, GLA_KEY_WIDTH:]
        if rope:
            lane = lax.broadcasted_iota(jnp.int32, (1, LANES), 1)
            first = (lane % 16) < 8
            cos = cos_ref[rows, :]
            sin = sin_ref[rows, :]

            def rot(t):
                partner = jnp.where(first, pltpu.roll(t, LANES - 8, axis=1), pltpu.roll(t, 8, axis=1))
                return t * cos + partner * sin

            lq = rot(lq)
            lk = rot(lk)
        lq_ref[0, rows, :] = lq
        lk_ref[0, rows, :] = lk
        lv_ref[0, rows, :] = proj(OFF_LA_V, GLA_WIDTH)
        lg_ref[0, rows, :] = proj(OFF_LA_G, GLA_WIDTH)

        lr = proj(OFF_LR, 2 * GLA_GATE_RANK)
        z = _dot(lr.astype(BF16), wg_ref[...]) + bg_ref[...]
        log_sig = jnp.minimum(z, 0.0) - jnp.log1p(jnp.exp(-jnp.abs(z)))
        la_ref[0, rows, :] = log_sig * (LOG2_E / GLA_GATE_TAU)

    hb_next = normed(0)
    for i in range(tm // sub):
        hb = hb_next
        hb_next = normed(i + 1) if (i + 1) * sub < tm else None
        project(i, hb)


def _in_proj(x, mods_l, row_of_batch, lw, rope_tabs, kv_out):
    B, L, _ = x.shape
    tm = min(IN_ROW_TILE, L)
    rope = rope_tabs is not None
    const = lambda b, i: (0, 0)
    tile = lambda b, i: (b, i, 0)

    in_specs = [
        pl.BlockSpec((1, tm, D_MODEL), tile),
        pl.BlockSpec((1, N_MODS, D_MODEL), lambda b, i: (row_of_batch(b), 0, 0)),
        pl.BlockSpec((1, D_MODEL), const),
        pl.BlockSpec((D_MODEL, IN_WIDTH), const),
        pl.BlockSpec((1, NA_WIDTH), const),
        pl.BlockSpec((1, NA_WIDTH), const),
        pl.BlockSpec((2 * GLA_GATE_RANK, 2 * GLA_KEY_WIDTH), const),
        pl.BlockSpec((1, 2 * GLA_KEY_WIDTH), const),
    ]
    args = [x, mods_l, lw["norm1_gain"], lw["w_in"], lw["q_gain"], lw["k_gain"], lw["w_gate"], lw["b_gate"]]
    if rope:
        in_specs += [pl.BlockSpec((tm, GLA_KEY_WIDTH), lambda b, i: (i, 0))] * 2
        args += list(rope_tabs)

    widths = [(POOL_WIDTH, F32), (NA_WIDTH, BF16), (NA_WIDTH, BF16), (NA_WIDTH, BF16),
              (GLA_KEY_WIDTH, F32), (GLA_KEY_WIDTH, F32), (GLA_WIDTH, F32), (GLA_WIDTH, F32),
              (2 * GLA_KEY_WIDTH, F32)]
    out_specs = [pl.BlockSpec((1, tm, w), tile) for w, _ in widths]
    out_shape = [jax.ShapeDtypeStruct((B, L, w), dt) for w, dt in widths]
    aliases = {}
    kv_seq = None
    if kv_out is not None:
        k_buf, v_buf, layer, kv_seq = kv_out
        assert B == 1 and (tm % kv_seq == 0 or kv_seq % tm == 0)
        if tm >= kv_seq:
            kv_spec = pl.BlockSpec((tm // kv_seq, 1, NA_HEADS, kv_seq, NA_HEAD_DIM), lambda b, i: (i, layer, 0, 0, 0))
        else:
            per_seq = kv_seq // tm
            kv_spec = pl.BlockSpec((1, 1, NA_HEADS, tm, NA_HEAD_DIM),
                                   lambda b, i: (i // per_seq, layer, 0, i % per_seq, 0))
        for buf in (k_buf, v_buf):
            aliases[len(args)] = len(out_specs)
            in_specs.append(pl.BlockSpec(memory_space=pl.ANY))
            args.append(buf)
            out_specs.append(kv_spec)
            out_shape.append(jax.ShapeDtypeStruct(buf.shape, buf.dtype))

    return pl.pallas_call(
        functools.partial(_inproj_kernel, rope=rope, kv_seq=kv_seq),
        grid=(B, L // tm),
        in_specs=in_specs,
        out_specs=out_specs,
        out_shape=out_shape,
        input_output_aliases=aliases,
        compiler_params=pltpu.CompilerParams(vmem_limit_bytes=VMEM_LIMIT),
        name="in_proj",
    )(*args)


def _pool_kernel(u_ref, w_ref, sc_ref, o_ref, pad_ref, s2_ref, s4_ref, s8_ref, *, L, tp):
    P = POOL_PAD
    zeros = jnp.zeros((P, POOL_WIDTH), F32)
    pad_ref[0:P, :] = zeros
    pad_ref[P + L:P + L + P, :] = zeros
    pad_ref[P:P + L, :] = u_ref[0]
    narrow = slice(0, LANES)
    wide = slice(LANES, 2 * LANES)
    lane = lax.broadcasted_iota(jnp.int32, (1, LANES), 1)
    low = lane < POOL_GROUP_DIM

    def chunks(level):
        lo, hi = 8 * level, L + 2 * P - 8 * level
        return [(a, min(a + tp, hi)) for a in range(lo, hi, tp)]

    for a, b in chunks(1):
        s2_ref[a:b, :] = pad_ref[a - 1:b - 1, wide] + pad_ref[a:b, wide]
    for a, b in chunks(2):
        s4_ref[a:b, :] = s2_ref[a - 1:b - 1, :] + s2_ref[a + 1:b + 1, :]
    for a, b in chunks(3):
        s8_ref[a:b, :] = s4_ref[a - 2:b - 2, :] + s4_ref[a + 2:b + 2, :]

    for base in range(0, L, tp):
        p0 = P + base

        def rows(ref, off, cols):
            return ref[p0 + off:p0 + off + tp, cols]

        u_a = rows(pad_ref, 0, narrow)
        w2 = rows(pad_ref, -1, narrow) + u_a
        w4 = w2 + rows(pad_ref, -2, narrow) + rows(pad_ref, 1, narrow)
        u_b = rows(pad_ref, 0, wide)
        w8 = rows(s8_ref, 0, slice(None))
        w16 = rows(s8_ref, -4, slice(None)) + rows(s8_ref, 4, slice(None))

        if base < POOL_HALO or base + tp > L - POOL_HALO:
            t = base + lax.broadcasted_iota(jnp.int32, (tp, LANES), 0)

            def mean(total, win):
                count = jnp.minimum(t + win // 2, L) - jnp.maximum(t - win // 2, 0)
                return total / count.astype(F32)
        else:
            def mean(total, win):
                return total * (1.0 / win)

        mean_a = jnp.where(low, mean(w2, 2), mean(w4, 4))
        mean_b = jnp.where(low, mean(w8, 8), mean(w16, 16))
        d = jnp.concatenate([mean_a - u_a, mean_b - u_b], axis=-1).astype(BF16)
        y = _dot(d, w_ref[...]) * sc_ref[...]
        o_ref[0, base:base + tp, :] = y.astype(BF16)


def _pool(u, lw):
    B, L, _ = u.shape
    tp = min(512, L)
    return pl.pallas_call(
        functools.partial(_pool_kernel, L=L, tp=tp),
        grid=(B,),
        in_specs=[
            pl.BlockSpec((1, L, POOL_WIDTH), lambda b: (b, 0, 0)),
            pl.BlockSpec((POOL_WIDTH, POOL_WIDTH), lambda b: (0, 0)),
            pl.BlockSpec((1, POOL_WIDTH), lambda b: (0, 0)),
        ],
        out_specs=pl.BlockSpec((1, L, POOL_WIDTH), lambda b: (b, 0, 0)),
        out_shape=jax.ShapeDtypeStruct((B, L, POOL_WIDTH), BF16),
        scratch_shapes=[pltpu.VMEM((L + 2 * POOL_PAD, POOL_WIDTH), F32)]
                       + [pltpu.VMEM((L + 2 * POOL_PAD, LANES), F32)] * 3,
        compiler_params=pltpu.CompilerParams(vmem_limit_bytes=VMEM_LIMIT),
        name="pool",
    )(u, lw["w_pool_bd"], lw["pool_scale"])


def _ctx_attn_kernel(q_ref, k_ref, v_ref, o_ref):
    L = q_ref.shape[1]
    lane = lax.broadcasted_iota(jnp.int32, (1, LANES), 1)
    even = lane < NA_HEAD_DIM
    zero = jnp.zeros((), BF16)

    def scores(j):
        cols = slice(j * LANES, (j + 1) * LANES)
        qp = q_ref[0, :, cols]
        q2 = jnp.concatenate([jnp.where(even, qp, zero), jnp.where(even, zero, qp)], axis=0)
        return _dot_nt(q2, k_ref[0, :, cols])

    def finish(j, s):
        cols = slice(j * LANES, (j + 1) * LANES)
        p = jnp.exp2(s - jnp.max(s, axis=-1, keepdims=True))
        denom = jnp.sum(p, axis=-1, keepdims=True)
        o2 = _dot(p.astype(BF16), v_ref[0, :, cols]) / denom
        o_ref[0, :, cols] = jnp.where(even, o2[:L], o2[L:]).astype(BF16)

    n_pairs = NA_HEADS // 2
    s_next = scores(0)
    for j in range(n_pairs):
        s_cur = s_next
        s_next = scores(j + 1) if j + 1 < n_pairs else None
        finish(j, s_cur)


def _ctx_attn(q, k, v):
    B, L, _ = q.shape
    spec = pl.BlockSpec((1, L, NA_WIDTH), lambda b: (b, 0, 0))
    return pl.pallas_call(
        _ctx_attn_kernel,
        grid=(B,),
        in_specs=[spec, spec, spec],
        out_specs=spec,
        out_shape=jax.ShapeDtypeStruct((B, L, NA_WIDTH), BF16),
        name="ctx_attn",
    )(q, k, v)


def _band_start(r, n_rows):
    return jnp.clip(r - NA_WIN_ROWS // 2, 0, n_rows - NA_WIN_ROWS)


def _lat_attn_kernel(q_ref, k_ref, v_ref, kc_ref, vc_ref, bias_ref, o_ref, *, n_rows):
    band = NA_WIN_ROWS * GRID_W
    lane = lax.broadcasted_iota(jnp.int32, (1, LANES), 1)
    even = lane < NA_HEAD_DIM
    zero = jnp.zeros((), BF16)

    def band_of(rr):
        r = pl.program_id(1) * NA_ROWS_PER_STEP + rr
        first = _band_start(r, n_rows)
        return pl.multiple_of(first * GRID_W, GRID_W), NA_WIN_ROWS - 1 - (r - first)

    def scores(rr, j):
        start, tile0 = band_of(rr)
        cols = slice(j * LANES, (j + 1) * LANES)
        qp = q_ref[0, rr * GRID_W:(rr + 1) * GRID_W, cols]
        q2 = jnp.concatenate([jnp.where(even, qp, zero), jnp.where(even, zero, qp)], axis=0)
        bias = jnp.concatenate(
            [jnp.concatenate([bias_ref[2 * j + hh, tile0 + 2 * ii] for ii in range(NA_WIN_ROWS // 2)], axis=-1)
             for hh in range(2)], axis=0)
        return _dot_nt(q2, k_ref[0, pl.ds(start, band), cols]) + bias, _dot_nt(q2, kc_ref[0, :, cols])

    def softmax(s_loc, s_ctx):
        m = jnp.maximum(jnp.max(s_loc, axis=-1, keepdims=True), jnp.max(s_ctx, axis=-1, keepdims=True))
        p_loc = jnp.exp2(s_loc - m)
        p_ctx = jnp.exp2(s_ctx - m)
        denom = jnp.sum(p_loc, axis=-1, keepdims=True) + jnp.sum(p_ctx, axis=-1, keepdims=True)
        return p_loc.astype(BF16), p_ctx.astype(BF16), denom

    def values(rr, j, p_loc, p_ctx, denom):
        start, _ = band_of(rr)
        cols = slice(j * LANES, (j + 1) * LANES)
        o2 = (_dot(p_loc, v_ref[0, pl.ds(start, band), cols]) + _dot(p_ctx, vc_ref[0, :, cols])) / denom
        o_ref[0, rr * GRID_W:(rr + 1) * GRID_W, cols] = jnp.where(even, o2[:GRID_W], o2[GRID_W:]).astype(BF16)

    chains = [(rr, j) for rr in range(NA_ROWS_PER_STEP) for j in range(NA_HEADS // 2)]
    s_next = scores(*chains[0])
    p_prev = None
    for i, chain in enumerate(chains):
        s_cur = s_next
        s_next = scores(*chains[i + 1]) if i + 1 < len(chains) else None
        p_cur = softmax(*s_cur)
        if p_prev is not None:
            values(*chains[i - 1], *p_prev)
        p_prev = p_cur
    values(*chains[-1], *p_prev)


def _lat_attn(q, k, v, kc, vc, bias_tiles):
    B, L, _ = q.shape
    n_rows = L // GRID_W
    assert n_rows >= NA_WIN_ROWS and n_rows % NA_ROWS_PER_STEP == 0
    past = kc.shape[1]
    whole = pl.BlockSpec((1, L, NA_WIDTH), lambda b, r: (b, 0, 0))
    rows = pl.BlockSpec((1, NA_ROWS_PER_STEP * GRID_W, NA_WIDTH), lambda b, r: (b, r, 0))
    ctx = pl.BlockSpec((1, past, NA_WIDTH), lambda b, r: (b, 0, 0))
    bias = pl.BlockSpec(bias_tiles.shape, lambda b, r: (0, 0, 0, 0))
    return pl.pallas_call(
        functools.partial(_lat_attn_kernel, n_rows=n_rows),
        grid=(B, n_rows // NA_ROWS_PER_STEP),
        in_specs=[rows, whole, whole, ctx, ctx, bias],
        out_specs=rows,
        out_shape=jax.ShapeDtypeStruct((B, L, NA_WIDTH), BF16),
        compiler_params=pltpu.CompilerParams(vmem_limit_bytes=VMEM_LIMIT),
        name="lat_attn",
    )(q, k, v, kc, vc, bias_tiles)


def _bias_tile_constants():
    n_dc = 2 * NA_WIN_COLS
    w = np.arange(GRID_W)[:, None]
    cc = np.arange(2 * GRID_W)[None, :]
    c = cc % GRID_W
    cs = np.clip(w - NA_WIN_COLS // 2, 0, GRID_W - NA_WIN_COLS)
    valid = (c >= cs) & (c < cs + NA_WIN_COLS)
    k_idx = (cc // GRID_W) * n_dc + (c - w + NA_WIN_COLS - 1)
    onehot = (np.arange(2 * n_dc)[:, None, None] == k_idx[None]) & valid[None]
    onehot = onehot.reshape(2 * n_dc, GRID_W * 2 * GRID_W)
    mask = np.where(valid, 0.0, MASK_VALUE).reshape(1, GRID_W * 2 * GRID_W)
    return jnp.asarray(onehot, BF16), jnp.asarray(mask, F32)


def _bias_tiles_kernel(rb_ref, oh_ref, mask_ref, o_ref):
    hi, mid, lo = _split3(rb_ref[...])
    oh = oh_ref[...]
    o_ref[...] = (_dot(hi, oh) + _dot(mid, oh) + _dot(lo, oh)) * LOG2_E + mask_ref[...]


def _bias_tiles(rel_bias_l):
    n_dr = 2 * NA_WIN_ROWS - 1
    padded = jnp.pad(rel_bias_l, ((0, 0), (0, 0), (0, 1)))
    pairs = jnp.concatenate([padded[:, :-1], padded[:, 1:]], axis=-1)
    pairs = pairs.reshape(NA_HEADS * (n_dr - 1), 4 * NA_WIN_COLS)
    onehot, mask = _bias_tile_constants()
    full = lambda shape: pl.BlockSpec(shape, lambda: (0,) * len(shape))
    out = pl.pallas_call(
        _bias_tiles_kernel,
        in_specs=[full(pairs.shape), full(onehot.shape), full(mask.shape)],
        out_specs=full((pairs.shape[0], onehot.shape[1])),
        out_shape=jax.ShapeDtypeStruct((pairs.shape[0], onehot.shape[1]), F32),
        name="bias_tiles",
    )(pairs, onehot, mask)
    return out.reshape(NA_HEADS, n_dr - 1, GRID_W, 2 * GRID_W)


def _gla_block(q_ref, k_ref, v_ref, b_ref, st_ref, o_ref, expand, head_mask, bi, r0, reverse):
    half = GLA_BLOCK // 2
    rows = pl.ds(r0, GLA_BLOCK)
    qb = q_ref[bi, rows, :]
    kb = k_ref[bi, rows, :]
    vb = v_ref[bi, rows, :]
    bb = b_ref[bi, rows, :]
    end_row = 0 if reverse else GLA_BLOCK - 1
    b_end = bb[end_row:end_row + 1, :]
    st = st_ref[bi]

    o_inter = _dot_nt((qb * jnp.exp2(bb)).astype(BF16), st.astype(BF16))

    t_in = lax.broadcasted_iota(jnp.int32, (half, 1), 0)
    q_half = (qb[:half], qb[half:])
    b_half = (bb[:half], bb[half:])
    pieces, owners = [], []
    for s in range(GLA_BLOCK):
        hs = s // half
        k_s = kb[s:s + 1, :]
        b_s = bb[s:s + 1, :]
        for ht in range(2):
            if (ht > hs) if reverse else (ht < hs):
                continue
            decay = jnp.exp2(b_half[ht] - b_s)
            if ht == hs:
                seen = (t_in <= s - hs * half) if reverse else (t_in >= s - hs * half)
                decay = jnp.where(seen, decay, 0.0)
            pieces.append(q_half[ht] * k_s * decay)
            owners.append((ht, s))
    a_exp = _dot(jnp.concatenate(pieces, axis=0).astype(BF16), expand)
    o_intra = [jnp.zeros((half, GLA_WIDTH), F32), jnp.zeros((half, GLA_WIDTH), F32)]
    for i, (ht, s) in enumerate(owners):
        o_intra[ht] = o_intra[ht] + a_exp[i * half:(i + 1) * half, :] * vb[s:s + 1, :]
    o_ref[bi, rows, :] = o_inter + jnp.concatenate(o_intra, axis=0)

    k_hat = (kb * jnp.exp2(b_end - bb)).astype(BF16)
    upd = _dot_tn(vb.astype(BF16), k_hat)
    st_ref[bi] = st * jnp.exp2(b_end) + upd * head_mask


def _gla_kernel(qf_ref, kf_ref, vf_ref, laf_ref, qb_ref, kb_ref, vb_ref, lab_ref, s0f_ref, s0b_ref,
                trif_ref, trib_ref, e_ref, of_ref, ob_ref, sff_ref, sfb_ref,
                stf_ref, stb_ref, bf_ref, bb_ref, *, tm, nb):
    n = pl.program_id(1)
    n_blocks = tm // GLA_BLOCK

    @pl.when(n == 0)
    def _():
        stf_ref[...] = s0f_ref[...]
        stb_ref[...] = s0b_ref[...]

    for la_ref, tri_ref, b_ref in ((laf_ref, trif_ref, bf_ref), (lab_ref, trib_ref, bb_ref)):
        tri = tri_ref[...]
        for bi in range(nb):
            g_hi, g_mid, _ = _split3(la_ref[bi])
            sums = _dot(tri, jnp.concatenate([g_hi, g_mid], axis=-1))
            b_ref[bi] = sums[:, :GLA_KEY_WIDTH] + sums[:, GLA_KEY_WIDTH:]

    expand = e_ref[...]
    head_v = lax.broadcasted_iota(jnp.int32, (GLA_WIDTH, GLA_KEY_WIDTH), 0) // GLA_DV
    head_k = lax.broadcasted_iota(jnp.int32, (GLA_WIDTH, GLA_KEY_WIDTH), 1) // GLA_DK
    head_mask = jnp.where(head_v == head_k, 1.0, 0.0)

    def block(j, carry):
        r_f = pl.multiple_of(j * GLA_BLOCK, GLA_BLOCK)
        r_b = pl.multiple_of((n_blocks - 1 - j) * GLA_BLOCK, GLA_BLOCK)
        for bi in range(nb):
            _gla_block(qf_ref, kf_ref, vf_ref, bf_ref, stf_ref, of_ref, expand, head_mask, bi, r_f, reverse=False)
            _gla_block(qb_ref, kb_ref, vb_ref, bb_ref, stb_ref, ob_ref, expand, head_mask, bi, r_b, reverse=True)
        return carry

    lax.fori_loop(0, n_blocks, block, 0)

    @pl.when(n == pl.num_programs(1) - 1)
    def _():
        sff_ref[...] = stf_ref[...]
        sfb_ref[...] = stb_ref[...]


def _block_tri(tm, reverse):
    t = np.arange(tm)[:, None]
    s = np.arange(tm)[None, :]
    same = (t // GLA_BLOCK) == (s // GLA_BLOCK)
    return jnp.asarray(same & ((s >= t) if reverse else (s <= t)), BF16)


def _gla_scan(q, k, v, la, s0f_t, s0b_t, expand):
    B, L, _ = q.shape
    tm = min(GLA_TILE, L)
    n_tiles = L // tm
    nb = math.gcd(B, GLA_BATCH_PER_STEP)
    fwd = lambda b, n: (b, n, 0)
    bwd = lambda b, n: (b, n_tiles - 1 - n, 0)
    state = pl.BlockSpec((nb, GLA_WIDTH, GLA_KEY_WIDTH), lambda b, n: (b, 0, 0))
    const = lambda shape: pl.BlockSpec(shape, lambda b, n: (0, 0))

    def operands(tmap, la_col):
        return [pl.BlockSpec((nb, tm, GLA_KEY_WIDTH), tmap), pl.BlockSpec((nb, tm, GLA_KEY_WIDTH), tmap),
                pl.BlockSpec((nb, tm, GLA_WIDTH), tmap),
                pl.BlockSpec((nb, tm, GLA_KEY_WIDTH), lambda b, n: tmap(b, n)[:2] + (la_col,))]

    return pl.pallas_call(
        functools.partial(_gla_kernel, tm=tm, nb=nb),
        grid=(B // nb, n_tiles),
        in_specs=operands(fwd, 0) + operands(bwd, 1) + [state, state, const((tm, tm)), const((tm, tm)),
                                                        const((GLA_KEY_WIDTH, GLA_WIDTH))],
        out_specs=[pl.BlockSpec((nb, tm, GLA_WIDTH), fwd), pl.BlockSpec((nb, tm, GLA_WIDTH), bwd), state, state],
        out_shape=[jax.ShapeDtypeStruct((B, L, GLA_WIDTH), F32), jax.ShapeDtypeStruct((B, L, GLA_WIDTH), F32),
                   jax.ShapeDtypeStruct((B, GLA_WIDTH, GLA_KEY_WIDTH), F32),
                   jax.ShapeDtypeStruct((B, GLA_WIDTH, GLA_KEY_WIDTH), F32)],
        scratch_shapes=[pltpu.VMEM((nb, GLA_WIDTH, GLA_KEY_WIDTH), F32), pltpu.VMEM((nb, GLA_WIDTH, GLA_KEY_WIDTH), F32),
                        pltpu.VMEM((nb, tm, GLA_KEY_WIDTH), F32), pltpu.VMEM((nb, tm, GLA_KEY_WIDTH), F32)],
        compiler_params=pltpu.CompilerParams(vmem_limit_bytes=VMEM_LIMIT),
        name="gla",
    )(q, k, v, la, q, k, v, la, s0f_t, s0b_t, _block_tri(tm, False), _block_tri(tm, True), expand)


def _state_to_kernel(s):
    B = s.shape[0]
    same_head = np.eye(GLA_HEADS, dtype=bool)[None, :, None, :, None]
    st = jnp.where(same_head, s.transpose(0, 1, 3, 2)[:, :, :, None, :], 0.0)
    return st.reshape(B, GLA_WIDTH, GLA_KEY_WIDTH)


def _state_from_kernel(st):
    B = st.shape[0]
    blocks = st.reshape(B, GLA_HEADS, GLA_DV, GLA_HEADS, GLA_DK)
    diag = jnp.stack([blocks[:, h, :, h, :] for h in range(GLA_HEADS)], axis=1)
    return diag.transpose(0, 1, 3, 2)


def _post_kernel(x_ref, m_ref, yp_ref, yn_ref, of_ref, ob_ref, lg_ref, gg_ref, g2_ref,
                 wo_ref, wi_ref, wf_ref, o_ref, acc_ref):
    gate1 = m_ref[0, 2:3, :]
    shift2 = m_ref[0, 3:4, :]
    scale2 = m_ref[0, 4:5, :]
    gate2 = m_ref[0, 5:6, :]

    o_la = _half_lane_rms(of_ref[0] + ob_ref[0], NORM_EPS) * gg_ref[...]
    y_la = (o_la * _silu(lg_ref[0])).astype(BF16)
    mixed = (_dot(yp_ref[0], wo_ref[0:POOL_WIDTH, :])
             + _dot(yn_ref[0], wo_ref[POOL_WIDTH:POOL_WIDTH + NA_WIDTH, :])
             + _dot(y_la, wo_ref[POOL_WIDTH + NA_WIDTH:D_MODEL, :]))
    x1 = x_ref[0] + gate1 * mixed

    h2 = (_row_rms(x1, NORM_EPS) * g2_ref[...] * (1.0 + scale2) + shift2).astype(BF16)
    for c0 in range(0, D_FF, FF_CHUNK):
        hg = _dot(h2, wi_ref[:, c0:c0 + FF_CHUNK])
        hu = _dot(h2, wi_ref[:, D_FF + c0:D_FF + c0 + FF_CHUNK])
        part = _dot((_silu(hg) * hu).astype(BF16), wf_ref[c0:c0 + FF_CHUNK, :])
        if c0 == 0:
            acc_ref[...] = part
        else:
            acc_ref[...] += part
    o_ref[0] = x1 + gate2 * acc_ref[...]


def _post(x, mods_l, row_of_batch, y_pool, y_na, o_f, o_b, lg, lw):
    B, L, _ = x.shape
    tm = min(ROW_TILE, L)
    const = lambda b, i: (0, 0)
    tile = lambda b, i: (b, i, 0)

    def resident(shape):
        return pl.BlockSpec(shape, const, pipeline_mode=pl.Buffered(1))

    return pl.pallas_call(
        _post_kernel,
        grid=(B, L // tm),
        in_specs=[
            pl.BlockSpec((1, tm, D_MODEL), tile),
            pl.BlockSpec((1, N_MODS, D_MODEL), lambda b, i: (row_of_batch(b), 0, 0)),
            pl.BlockSpec((1, tm, POOL_WIDTH), tile),
            pl.BlockSpec((1, tm, NA_WIDTH), tile),
            pl.BlockSpec((1, tm, GLA_WIDTH), tile),
            pl.BlockSpec((1, tm, GLA_WIDTH), tile),
            pl.BlockSpec((1, tm, GLA_WIDTH), tile),
            pl.BlockSpec((1, GLA_WIDTH), const),
            pl.BlockSpec((1, D_MODEL), const),
            resident((D_MODEL, D_MODEL)),
            resident((D_MODEL, 2 * D_FF)),
            resident((D_FF, D_MODEL)),
        ],
        out_specs=pl.BlockSpec((1, tm, D_MODEL), tile),
        out_shape=jax.ShapeDtypeStruct((B, L, D_MODEL), F32),
        scratch_shapes=[pltpu.VMEM((tm, D_MODEL), F32)],
        compiler_params=pltpu.CompilerParams(vmem_limit_bytes=VMEM_LIMIT),
        name="post",
    )(x, mods_l, y_pool, y_na, o_f, o_b, lg, lw["gla_gain"], lw["norm2_gain"],
      lw["w_out"], lw["w_ffn_in"], lw["w_ffn_out"])


def _rope_tables(L):
    t = jnp.arange(L)
    row = (t // GRID_W).astype(F32)
    col = (t % GRID_W).astype(F32)
    half = GLA_DK // 2
    inv_freq = ROPE_THETA ** (-jnp.arange(0, half, 2, dtype=F32) / half)
    ang_r = row[:, None] * inv_freq
    ang_c = col[:, None] * inv_freq
    sign = jnp.concatenate([-jnp.ones((half // 2,), F32), jnp.ones((half // 2,), F32)])

    def lanes(fn, signed):
        per_axis = []
        for ang in (ang_r, ang_c):
            v = jnp.concatenate([fn(ang), fn(ang)], axis=-1)
            per_axis.append(v * sign if signed else v)
        return jnp.tile(jnp.concatenate(per_axis, axis=-1), (1, GLA_HEADS))

    return lanes(jnp.cos, False), lanes(jnp.sin, True)


def _layer_weights(l, w):
    tile_heads = lambda g, n: jnp.tile(g, n)[None, :]
    return {
        "norm1_gain": w["norm1_gain"][l][None, :],
        "norm2_gain": w["norm2_gain"][l][None, :],
        "w_in": w["w_in"][l].astype(BF16),
        "q_gain": tile_heads(w["q_norm_gain"][l], NA_HEADS),
        "k_gain": tile_heads(w["k_norm_gain"][l], NA_HEADS),
        "w_gate": jax.scipy.linalg.block_diag(w["w_gate_f"][l], w["w_gate_b"][l]).astype(BF16),
        "b_gate": jnp.concatenate([w["b_gate_f"][l], w["b_gate_b"][l]])[None, :],
        "w_pool_bd": jax.scipy.linalg.block_diag(*[w["w_pool"][l, g] for g in range(len(POOL_WINDOWS))]).astype(BF16),
        "pool_scale": w["pool_scale"][l][None, :],
        "gla_gain": tile_heads(w["gla_norm_gain"][l], GLA_HEADS),
        "w_out": w["w_out"][l].astype(BF16),
        "w_ffn_in": w["w_ffn_in"][l].astype(BF16),
        "w_ffn_out": w["w_ffn_out"][l].astype(BF16),
    }


def _trunk_layer(x, mods_l, row_of_batch, lw, expand, s0_f, s0_b, latent, kv_out=None):
    is_ctx = latent is None
    B, L, _ = x.shape
    per_token = (lambda a: a.reshape(1, B * L, a.shape[-1])) if is_ctx else (lambda a: a)
    per_seq = lambda a: a.reshape(B, L, a.shape[-1])
    outs = _in_proj(per_token(x), mods_l, row_of_batch, lw, None if is_ctx else latent[0],
                    kv_out + (L,) if is_ctx else None)
    u, q, k, v, lq, lk, lv, lg, la = [per_seq(a) for a in outs[:9]]
    y_pool = _pool(u, lw)
    if is_ctx:
        y_na = _ctx_attn(q, k, v)
    else:
        y_na = _lat_attn(q, k, v, latent[1], latent[2], latent[3])
    o_f, o_b, s_f, s_b = _gla_scan(lq, lk, lv, la, s0_f, s0_b, expand)
    x = per_seq(_post(per_token(x), mods_l, row_of_batch,
                      *[per_token(a) for a in (y_pool, y_na, o_f, o_b, lg)], lw))
    return x, s_f, s_b, tuple(outs[9:])


def kernel(x_prompt, x_sample, c, cache_na_k, cache_na_v, state_gla_fwd, state_gla_bwd, c_ctx, w_ada, b_ada, norm1_gain, norm2_gain, w_in, w_pool, pool_scale, q_norm_gain, k_norm_gain, rel_bias, w_gate_f, b_gate_f, w_gate_b, b_gate_b, gla_norm_gain, w_out, w_ffn_in, w_ffn_out):
    weights = dict(norm1_gain=norm1_gain, norm2_gain=norm2_gain, w_in=w_in, w_pool=w_pool, pool_scale=pool_scale,
                   q_norm_gain=q_norm_gain, k_norm_gain=k_norm_gain, w_gate_f=w_gate_f, b_gate_f=b_gate_f,
                   w_gate_b=w_gate_b, b_gate_b=b_gate_b, gla_norm_gain=gla_norm_gain, w_out=w_out,
                   w_ffn_in=w_ffn_in, w_ffn_out=w_ffn_out)
    B_ctx, L_ctx, _ = x_prompt.shape
    B_lat, L_lat, _ = x_sample.shape
    assert 1 + B_lat <= COND_ROWS

    conds = jnp.zeros((COND_ROWS, D_MODEL), F32).at[0].set(c_ctx).at[1:1 + B_lat].set(c)
    mods = _ada(conds, w_ada, b_ada).reshape(DEPTH, COND_ROWS, N_MODS, D_MODEL)
    lws = [_layer_weights(l, weights) for l in range(DEPTH)]
    head_of_k = np.arange(GLA_KEY_WIDTH) // GLA_DK
    head_of_v = np.arange(GLA_WIDTH) // GLA_DV
    expand = jnp.asarray(head_of_k[:, None] == head_of_v[None, :], BF16)

    xp = x_prompt
    zero_state = jnp.zeros((B_ctx, GLA_WIDTH, GLA_KEY_WIDTH), F32)
    new_k = jnp.zeros((B_ctx, DEPTH, NA_HEADS, L_ctx, NA_HEAD_DIM), F32)
    new_v = jnp.zeros((B_ctx, DEPTH, NA_HEADS, L_ctx, NA_HEAD_DIM), F32)
    sfs, sbs = [], []
    for l in range(DEPTH):
        xp, s_f, s_b, (new_k, new_v) = _trunk_layer(xp, mods[l], lambda b: 0, lws[l], expand,
                                                     zero_state, zero_state, None, (new_k, new_v, l))
        sfs.append(_state_from_kernel(s_f))
        sbs.append(_state_from_kernel(s_b))

    xs = x_sample
    rope_tabs = _rope_tables(L_lat)
    from_heads = lambda a: a.transpose(0, 2, 1, 3).reshape(B_lat, a.shape[2], NA_WIDTH).astype(BF16)
    for l in range(DEPTH):
        latent = (rope_tabs, from_heads(cache_na_k[:, l]), from_heads(cache_na_v[:, l]), _bias_tiles(rel_bias[l]))
        xs, _, _, _ = _trunk_layer(xs, mods[l], lambda b: b + 1, lws[l], expand,
                                   _state_to_kernel(state_gla_fwd[:, l]), _state_to_kernel(state_gla_bwd[:, l]),
                                   latent)

    return (xp, xs, new_k, new_v, jnp.stack(sfs, axis=1), jnp.stack(sbs, axis=1))
```

```python
import functools
import math

import numpy as np
import jax
import jax.numpy as jnp
from jax import lax
from jax.experimental import pallas as pl
from jax.experimental.pallas import tpu as pltpu

F32 = jnp.float32
BF16 = jnp.bfloat16

D_MODEL = 1024
DEPTH = 2
GRID_W = 64
POOL_WIDTH = 256
POOL_GROUP_DIM = 64
POOL_WINDOWS = (2, 4, 8, 16)
POOL_HALO = max(POOL_WINDOWS) // 2
POOL_PAD = 8 * len(POOL_WINDOWS)
NA_HEADS = 8
NA_HEAD_DIM = 64
NA_WIDTH = NA_HEADS * NA_HEAD_DIM
NA_WIN_ROWS = 8
NA_WIN_COLS = 16
NA_ROWS_PER_STEP = 8
GLA_HEADS = 4
GLA_DV = 64
GLA_DK = 32
GLA_WIDTH = GLA_HEADS * GLA_DV
GLA_KEY_WIDTH = GLA_HEADS * GLA_DK
GLA_GATE_RANK = 16
GLA_GATE_TAU = 16.0
GLA_BLOCK = 16
ROPE_THETA = 10000.0
D_FF = 2816
NORM_EPS = 1e-6
N_MODS = 6
COND_ROWS = 16
MASK_VALUE = -1e30
LOG2_E = 1.4426950408889634

OFF_POOL = 0
OFF_NA_Q = OFF_POOL + POOL_WIDTH
OFF_NA_K = OFF_NA_Q + NA_WIDTH
OFF_NA_V = OFF_NA_K + NA_WIDTH
OFF_LA_Q = OFF_NA_V + NA_WIDTH
OFF_LA_K = OFF_LA_Q + GLA_KEY_WIDTH
OFF_LA_V = OFF_LA_K + GLA_KEY_WIDTH
OFF_LA_G = OFF_LA_V + GLA_WIDTH
OFF_LR = OFF_LA_G + GLA_WIDTH
IN_WIDTH = OFF_LR + 2 * GLA_GATE_RANK

LANES = 128
ROW_TILE = 512
IN_ROW_TILE = 1024
IN_SUB_TILE = 256
GLA_TILE = 256
GLA_BATCH_PER_STEP = 4
FF_CHUNK = 256
ADA_COL_TILE = 512
VMEM_LIMIT = 56 * 1024 * 1024


def _dot(a, b):
    return jnp.dot(a, b, preferred_element_type=F32)


def _dot_nt(a, b):
    return lax.dot_general(a, b, (((1,), (1,)), ((), ())), preferred_element_type=F32)


def _dot_tn(a, b):
    return lax.dot_general(a, b, (((0,), (0,)), ((), ())), preferred_element_type=F32)


def _silu(x):
    return x * (1.0 / (1.0 + jnp.exp(-x)))


def _split3(x):
    hi = x.astype(BF16)
    r1 = x - hi.astype(F32)
    mid = r1.astype(BF16)
    lo = (r1 - mid.astype(F32)).astype(BF16)
    return hi, mid, lo


def _half_lane_rms(x, eps):
    lane = lax.broadcasted_iota(jnp.int32, (1, LANES), 1)
    low = lane < NA_HEAD_DIM
    cols = []
    for j in range(x.shape[-1] // LANES):
        blk = x[:, j * LANES:(j + 1) * LANES]
        sq = blk * blk
        s_lo = jnp.sum(jnp.where(low, sq, 0.0), axis=-1, keepdims=True)
        s_hi = jnp.sum(jnp.where(low, 0.0, sq), axis=-1, keepdims=True)
        r_lo = lax.rsqrt(s_lo * (1.0 / NA_HEAD_DIM) + eps)
        r_hi = lax.rsqrt(s_hi * (1.0 / NA_HEAD_DIM) + eps)
        cols.append(blk * jnp.where(low, r_lo, r_hi))
    return jnp.concatenate(cols, axis=-1)


def _row_rms(x, eps):
    return x * lax.rsqrt(jnp.mean(x * x, axis=-1, keepdims=True) + eps)


def _ada_kernel(c_ref, w_ref, b_ref, o_ref):
    s_hi, s_mid, _ = _split3(_silu(c_ref[...]))
    w = w_ref[0]
    w_hi, w_mid, _ = _split3(w)
    acc = _dot(s_hi, w_hi) + _dot(s_mid, w_hi) + _dot(s_hi, w_mid)
    o_ref[0] = acc + b_ref[0]


def _ada(conds, w_ada, b_ada):
    n_out = w_ada.shape[-1]
    return pl.pallas_call(
        _ada_kernel,
        grid=(DEPTH, n_out // ADA_COL_TILE),
        in_specs=[
            pl.BlockSpec((COND_ROWS, D_MODEL), lambda l, j: (0, 0)),
            pl.BlockSpec((1, D_MODEL, ADA_COL_TILE), lambda l, j: (l, 0, j)),
            pl.BlockSpec((1, 1, ADA_COL_TILE), lambda l, j: (l, 0, j)),
        ],
        out_specs=pl.BlockSpec((1, COND_ROWS, ADA_COL_TILE), lambda l, j: (l, 0, j)),
        out_shape=jax.ShapeDtypeStruct((DEPTH, COND_ROWS, n_out), F32),
        name="ada",
    )(conds, w_ada, b_ada.reshape(DEPTH, 1, n_out))


def _inproj_kernel(*refs, rope, kv_seq):
    x_ref, m_ref, g1_ref, w_ref, qg_ref, kg_ref, wg_ref, bg_ref = refs[:8]
    pos = 8
    if rope:
        cos_ref, sin_ref = refs[pos:pos + 2]
        pos += 2
    if kv_seq:
        pos += 2
    u_ref, q_ref, k_ref, v_ref, lq_ref, lk_ref, lv_ref, lg_ref, la_ref = refs[pos:pos + 9]
    pos += 9
    if kv_seq:
        kf_ref, vf_ref = refs[pos:pos + 2]

    shift1 = m_ref[0, 0:1, :]
    scale1 = m_ref[0, 1:2, :]
    tm = x_ref.shape[1]
    sub = min(IN_SUB_TILE, tm)

    def normed(i):
        rows = slice(i * sub, (i + 1) * sub)
        h = _row_rms(x_ref[0, rows, :], NORM_EPS) * g1_ref[...] * (1.0 + scale1) + shift1
        return h.astype(BF16)

    def project(i, hb):
        rows = slice(i * sub, (i + 1) * sub)

        def proj(off, width):
            return _dot(hb, w_ref[:, off:off + width])

        u_ref[0, rows, :] = proj(OFF_POOL, POOL_WIDTH)

        qn = _half_lane_rms(proj(OFF_NA_Q, NA_WIDTH), NORM_EPS) * qg_ref[...]
        q_ref[0, rows, :] = (qn * (NA_HEAD_DIM ** -0.5 * LOG2_E)).astype(BF16)
        kn = _half_lane_rms(proj(OFF_NA_K, NA_WIDTH), NORM_EPS) * kg_ref[...]
        k_ref[0, rows, :] = kn.astype(BF16)
        vn = proj(OFF_NA_V, NA_WIDTH)
        v_ref[0, rows, :] = vn.astype(BF16)
        if kv_seq:
            piece = min(sub, kv_seq)
            for j in range(sub // piece):
                first = i * sub + j * piece
                seq, off = (first // kv_seq, first % kv_seq) if tm >= kv_seq else (0, first)
                for hh in range(NA_HEADS):
                    cols = slice(hh * NA_HEAD_DIM, (hh + 1) * NA_HEAD_DIM)
                    kf_ref[seq, 0, hh, off:off + piece, :] = kn[j * piece:(j + 1) * piece, cols]
                    vf_ref[seq, 0, hh, off:off + piece, :] = vn[j * piece:(j + 1) * piece, cols]

        lqk = proj(OFF_LA_Q, 2 * GLA_KEY_WIDTH)
        lq = lqk[:, :GLA_KEY_WIDTH] * (GLA_DK ** -0.5)
        lk = lqk[:, GLA_KEY_WIDTH:]
        if rope:
            lane = lax.broadcasted_iota(jnp.int32, (1, LANES), 1)
            first = (lane % 16) < 8
            cos = cos_ref[rows, :]
            sin = sin_ref[rows, :]

            def rot(t):
                partner = jnp.where(first, pltpu.roll(t, LANES - 8, axis=1), pltpu.roll(t, 8, axis=1))
                return t * cos + partner * sin

            lq = rot(lq)
            lk = rot(lk)
        lq_ref[0, rows, :] = lq
        lk_ref[0, rows, :] = lk
        lv_ref[0, rows, :] = proj(OFF_LA_V, GLA_WIDTH)
        lg_ref[0, rows, :] = proj(OFF_LA_G, GLA_WIDTH)

        lr = proj(OFF_LR, 2 * GLA_GATE_RANK)
        z = _dot(lr.astype(BF16), wg_ref[...]) + bg_ref[...]
        log_sig = jnp.minimum(z, 0.0) - jnp.log1p(jnp.exp(-jnp.abs(z)))
        la_ref[0, rows, :] = log_sig * (LOG2_E / GLA_GATE_TAU)

    hb_next = normed(0)
    for i in range(tm // sub):
        hb = hb_next
        hb_next = normed(i + 1) if (i + 1) * sub < tm else None
        project(i, hb)


def _in_proj(x, mods_l, row_of_batch, lw, rope_tabs, kv_out):
    B, L, _ = x.shape
    tm = min(IN_ROW_TILE, L)
    rope = rope_tabs is not None
    const = lambda b, i: (0, 0)
    tile = lambda b, i: (b, i, 0)

    in_specs = [
        pl.BlockSpec((1, tm, D_MODEL), tile),
        pl.BlockSpec((1, N_MODS, D_MODEL), lambda b, i: (row_of_batch(b), 0, 0)),
        pl.BlockSpec((1, D_MODEL), const),
        pl.BlockSpec((D_MODEL, IN_WIDTH), const),
        pl.BlockSpec((1, NA_WIDTH), const),
        pl.BlockSpec((1, NA_WIDTH), const),
        pl.BlockSpec((2 * GLA_GATE_RANK, 2 * GLA_KEY_WIDTH), const),
        pl.BlockSpec((1, 2 * GLA_KEY_WIDTH), const),
    ]
    args = [x, mods_l, lw["norm1_gain"], lw["w_in"], lw["q_gain"], lw["k_gain"], lw["w_gate"], lw["b_gate"]]
    if rope:
        in_specs += [pl.BlockSpec((tm, GLA_KEY_WIDTH), lambda b, i: (i, 0))] * 2
        args += list(rope_tabs)

    widths = [(POOL_WIDTH, F32), (NA_WIDTH, BF16), (NA_WIDTH, BF16), (NA_WIDTH, BF16),
              (GLA_KEY_WIDTH, F32), (GLA_KEY_WIDTH, F32), (GLA_WIDTH, F32), (GLA_WIDTH, F32),
              (2 * GLA_KEY_WIDTH, F32)]
    out_specs = [pl.BlockSpec((1, tm, w), tile) for w, _ in widths]
    out_shape = [jax.ShapeDtypeStruct((B, L, w), dt) for w, dt in widths]
    aliases = {}
    kv_seq = None
    if kv_out is not None:
        k_buf, v_buf, layer, kv_seq = kv_out
        assert B == 1 and (tm % kv_seq == 0 or kv_seq % tm == 0)
        if tm >= kv_seq:
            kv_spec = pl.BlockSpec((tm // kv_seq, 1, NA_HEADS, kv_seq, NA_HEAD_DIM), lambda b, i: (i, layer, 0, 0, 0))
        else:
            per_seq = kv_seq // tm
            kv_spec = pl.BlockSpec((1, 1, NA_HEADS, tm, NA_HEAD_DIM),
                                   lambda b, i: (i // per_seq, layer, 0, i % per_seq, 0))
        for buf in (k_buf, v_buf):
            aliases[len(args)] = len(out_specs)
            in_specs.append(pl.BlockSpec(memory_space=pl.ANY))
            args.append(buf)
            out_specs.append(kv_spec)
            out_shape.append(jax.ShapeDtypeStruct(buf.shape, buf.dtype))

    return pl.pallas_call(
        functools.partial(_inproj_kernel, rope=rope, kv_seq=kv_seq),
        grid=(B, L // tm),
        in_specs=in_specs,
        out_specs=out_specs,
        out_shape=out_shape,
        input_output_aliases=aliases,
        compiler_params=pltpu.CompilerParams(vmem_limit_bytes=VMEM_LIMIT),
        name="in_proj",
    )(*args)


def _pool_kernel(u_ref, w_ref, sc_ref, o_ref, pad_ref, s2_ref, s4_ref, s8_ref, *, L, tp):
    P = POOL_PAD
    zeros = jnp.zeros((P, POOL_WIDTH), F32)
    pad_ref[0:P, :] = zeros
    pad_ref[P + L:P + L + P, :] = zeros
    pad_ref[P:P + L, :] = u_ref[0]
    narrow = slice(0, LANES)
    wide = slice(LANES, 2 * LANES)
    lane = lax.broadcasted_iota(jnp.int32, (1, LANES), 1)
    low = lane < POOL_GROUP_DIM

    def chunks(level):
        lo, hi = 8 * level, L + 2 * P - 8 * level
        return [(a, min(a + tp, hi)) for a in range(lo, hi, tp)]

    for a, b in chunks(1):
        s2_ref[a:b, :] = pad_ref[a - 1:b - 1, wide] + pad_ref[a:b, wide]
    for a, b in chunks(2):
        s4_ref[a:b, :] = s2_ref[a - 1:b - 1, :] + s2_ref[a + 1:b + 1, :]
    for a, b in chunks(3):
        s8_ref[a:b, :] = s4_ref[a - 2:b - 2, :] + s4_ref[a + 2:b + 2, :]

    for base in range(0, L, tp):
        p0 = P + base

        def rows(ref, off, cols):
            return ref[p0 + off:p0 + off + tp, cols]

        u_a = rows(pad_ref, 0, narrow)
        w2 = rows(pad_ref, -1, narrow) + u_a
        w4 = w2 + rows(pad_ref, -2, narrow) + rows(pad_ref, 1, narrow)
        u_b = rows(pad_ref, 0, wide)
        w8 = rows(s8_ref, 0, slice(None))
        w16 = rows(s8_ref, -4, slice(None)) + rows(s8_ref, 4, slice(None))

        if base < POOL_HALO or base + tp > L - POOL_HALO:
            t = base + lax.broadcasted_iota(jnp.int32, (tp, LANES), 0)

            def mean(total, win):
                count = jnp.minimum(t + win // 2, L) - jnp.maximum(t - win // 2, 0)
                return total / count.astype(F32)
        else:
            def mean(total, win):
                return total * (1.0 / win)

        mean_a = jnp.where(low, mean(w2, 2), mean(w4, 4))
        mean_b = jnp.where(low, mean(w8, 8), mean(w16, 16))
        d = jnp.concatenate([mean_a - u_a, mean_b - u_b], axis=-1).astype(BF16)
        y = _dot(d, w_ref[...]) * sc_ref[...]
        o_ref[0, base:base + tp, :] = y.astype(BF16)


def _pool(u, lw):
    B, L, _ = u.shape
    tp = min(512, L)
    return pl.pallas_call(
        functools.partial(_pool_kernel, L=L, tp=tp),
        grid=(B,),
        in_specs=[
            pl.BlockSpec((1, L, POOL_WIDTH), lambda b: (b, 0, 0)),
            pl.BlockSpec((POOL_WIDTH, POOL_WIDTH), lambda b: (0, 0)),
            pl.BlockSpec((1, POOL_WIDTH), lambda b: (0, 0)),
        ],
        out_specs=pl.BlockSpec((1, L, POOL_WIDTH), lambda b: (b, 0, 0)),
        out_shape=jax.ShapeDtypeStruct((B, L, POOL_WIDTH), BF16),
        scratch_shapes=[pltpu.VMEM((L + 2 * POOL_PAD, POOL_WIDTH), F32)]
                       + [pltpu.VMEM((L + 2 * POOL_PAD, LANES), F32)] * 3,
        compiler_params=pltpu.CompilerParams(vmem_limit_bytes=VMEM_LIMIT),
        name="pool",
    )(u, lw["w_pool_bd"], lw["pool_scale"])


def _ctx_attn_kernel(q_ref, k_ref, v_ref, o_ref):
    L = q_ref.shape[1]
    lane = lax.broadcasted_iota(jnp.int32, (1, LANES), 1)
    even = lane < NA_HEAD_DIM
    zero = jnp.zeros((), BF16)

    def scores(j):
        cols = slice(j * LANES, (j + 1) * LANES)
        qp = q_ref[0, :, cols]
        q2 = jnp.concatenate([jnp.where(even, qp, zero), jnp.where(even, zero, qp)], axis=0)
        return _dot_nt(q2, k_ref[0, :, cols])

    def finish(j, s):
        cols = slice(j * LANES, (j + 1) * LANES)
        p = jnp.exp2(s - jnp.max(s, axis=-1, keepdims=True))
        denom = jnp.sum(p, axis=-1, keepdims=True)
        o2 = _dot(p.astype(BF16), v_ref[0, :, cols]) / denom
        o_ref[0, :, cols] = jnp.where(even, o2[:L], o2[L:]).astype(BF16)

    n_pairs = NA_HEADS // 2
    s_next = scores(0)
    for j in range(n_pairs):
        s_cur = s_next
        s_next = scores(j + 1) if j + 1 < n_pairs else None
        finish(j, s_cur)


def _ctx_attn(q, k, v):
    B, L, _ = q.shape
    spec = pl.BlockSpec((1, L, NA_WIDTH), lambda b: (b, 0, 0))
    return pl.pallas_call(
        _ctx_attn_kernel,
        grid=(B,),
        in_specs=[spec, spec, spec],
        out_specs=spec,
        out_shape=jax.ShapeDtypeStruct((B, L, NA_WIDTH), BF16),
        name="ctx_attn",
    )(q, k, v)


def _band_start(r, n_rows):
    return jnp.clip(r - NA_WIN_ROWS // 2, 0, n_rows - NA_WIN_ROWS)


def _lat_attn_kernel(q_ref, k_ref, v_ref, kc_ref, vc_ref, bias_ref, o_ref, *, n_rows):
    band = NA_WIN_ROWS * GRID_W
    lane = lax.broadcasted_iota(jnp.int32, (1, LANES), 1)
    even = lane < NA_HEAD_DIM
    zero = jnp.zeros((), BF16)

    def band_of(rr):
        r = pl.program_id(1) * NA_ROWS_PER_STEP + rr
        first = _band_start(r, n_rows)
        return pl.multiple_of(first * GRID_W, GRID_W), NA_WIN_ROWS - 1 - (r - first)

    def scores(rr, j):
        start, tile0 = band_of(rr)
        cols = slice(j * LANES, (j + 1) * LANES)
        qp = q_ref[0, rr * GRID_W:(rr + 1) * GRID_W, cols]
        q2 = jnp.concatenate([jnp.where(even, qp, zero), jnp.where(even, zero, qp)], axis=0)
        bias = jnp.concatenate(
            [jnp.concatenate([bias_ref[2 * j + hh, tile0 + 2 * ii] for ii in range(NA_WIN_ROWS // 2)], axis=-1)
             for hh in range(2)], axis=0)
        return _dot_nt(q2, k_ref[0, pl.ds(start, band), cols]) + bias, _dot_nt(q2, kc_ref[0, :, cols])

    def softmax(s_loc, s_ctx):
        m = jnp.maximum(jnp.max(s_loc, axis=-1, keepdims=True), jnp.max(s_ctx, axis=-1, keepdims=True))
        p_loc = jnp.exp2(s_loc - m)
        p_ctx = jnp.exp2(s_ctx - m)
        denom = jnp.sum(p_loc, axis=-1, keepdims=True) + jnp.sum(p_ctx, axis=-1, keepdims=True)
        return p_loc.astype(BF16), p_ctx.astype(BF16), denom

    def values(rr, j, p_loc, p_ctx, denom):
        start, _ = band_of(rr)
        cols = slice(j * LANES, (j + 1) * LANES)
        o2 = (_dot(p_loc, v_ref[0, pl.ds(start, band), cols]) + _dot(p_ctx, vc_ref[0, :, cols])) / denom
        o_ref[0, rr * GRID_W:(rr + 1) * GRID_W, cols] = jnp.where(even, o2[:GRID_W], o2[GRID_W:]).astype(BF16)

    chains = [(rr, j) for rr in range(NA_ROWS_PER_STEP) for j in range(NA_HEADS // 2)]
    s_next = scores(*chains[0])
    p_prev = None
    for i, chain in enumerate(chains):
        s_cur = s_next
        s_next = scores(*chains[i + 1]) if i + 1 < len(chains) else None
        p_cur = softmax(*s_cur)
        if p_prev is not None:
            values(*chains[i - 1], *p_prev)
        p_prev = p_cur
    values(*chains[-1], *p_prev)


def _lat_attn(q, k, v, kc, vc, bias_tiles):
    B, L, _ = q.shape
    n_rows = L // GRID_W
    assert n_rows >= NA_WIN_ROWS and n_rows % NA_ROWS_PER_STEP == 0
    past = kc.shape[1]
    whole = pl.BlockSpec((1, L, NA_WIDTH), lambda b, r: (b, 0, 0))
    rows = pl.BlockSpec((1, NA_ROWS_PER_STEP * GRID_W, NA_WIDTH), lambda b, r: (b, r, 0))
    ctx = pl.BlockSpec((1, past, NA_WIDTH), lambda b, r: (b, 0, 0))
    bias = pl.BlockSpec(bias_tiles.shape, lambda b, r: (0, 0, 0, 0))
    return pl.pallas_call(
        functools.partial(_lat_attn_kernel, n_rows=n_rows),
        grid=(B, n_rows // NA_ROWS_PER_STEP),
        in_specs=[rows, whole, whole, ctx, ctx, bias],
        out_specs=rows,
        out_shape=jax.ShapeDtypeStruct((B, L, NA_WIDTH), BF16),
        compiler_params=pltpu.CompilerParams(vmem_limit_bytes=VMEM_LIMIT),
        name="lat_attn",
    )(q, k, v, kc, vc, bias_tiles)


def _bias_tile_constants():
    n_dc = 2 * NA_WIN_COLS
    w = np.arange(GRID_W)[:, None]
    cc = np.arange(2 * GRID_W)[None, :]
    c = cc % GRID_W
    cs = np.clip(w - NA_WIN_COLS // 2, 0, GRID_W - NA_WIN_COLS)
    valid = (c >= cs) & (c < cs + NA_WIN_COLS)
    k_idx = (cc // GRID_W) * n_dc + (c - w + NA_WIN_COLS - 1)
    onehot = (np.arange(2 * n_dc)[:, None, None] == k_idx[None]) & valid[None]
    onehot = onehot.reshape(2 * n_dc, GRID_W * 2 * GRID_W)
    mask = np.where(valid, 0.0, MASK_VALUE).reshape(1, GRID_W * 2 * GRID_W)
    return jnp.asarray(onehot, BF16), jnp.asarray(mask, F32)


def _bias_tiles_kernel(rb_ref, oh_ref, mask_ref, o_ref):
    hi, mid, lo = _split3(rb_ref[...])
    oh = oh_ref[...]
    o_ref[...] = (_dot(hi, oh) + _dot(mid, oh) + _dot(lo, oh)) * LOG2_E + mask_ref[...]


def _bias_tiles(rel_bias_l):
    n_dr = 2 * NA_WIN_ROWS - 1
    padded = jnp.pad(rel_bias_l, ((0, 0), (0, 0), (0, 1)))
    pairs = jnp.concatenate([padded[:, :-1], padded[:, 1:]], axis=-1)
    pairs = pairs.reshape(NA_HEADS * (n_dr - 1), 4 * NA_WIN_COLS)
    onehot, mask = _bias_tile_constants()
    full = lambda shape: pl.BlockSpec(shape, lambda: (0,) * len(shape))
    out = pl.pallas_call(
        _bias_tiles_kernel,
        in_specs=[full(pairs.shape), full(onehot.shape), full(mask.shape)],
        out_specs=full((pairs.shape[0], onehot.shape[1])),
        out_shape=jax.ShapeDtypeStruct((pairs.shape[0], onehot.shape[1]), F32),
        name="bias_tiles",
    )(pairs, onehot, mask)
    return out.reshape(NA_HEADS, n_dr - 1, GRID_W, 2 * GRID_W)


def _gla_block(q_ref, k_ref, v_ref, b_ref, st_ref, o_ref, expand, head_mask, bi, r0, reverse):
    half = GLA_BLOCK // 2
    rows = pl.ds(r0, GLA_BLOCK)
    qb = q_ref[bi, rows, :]
    kb = k_ref[bi, rows, :]
    vb = v_ref[bi, rows, :]
    bb = b_ref[bi, rows, :]
    end_row = 0 if reverse else GLA_BLOCK - 1
    b_end = bb[end_row:end_row + 1, :]
    st = st_ref[bi]

    o_inter = _dot_nt((qb * jnp.exp2(bb)).astype(BF16), st.astype(BF16))

    t_in = lax.broadcasted_iota(jnp.int32, (half, 1), 0)
    q_half = (qb[:half], qb[half:])
    b_half = (bb[:half], bb[half:])
    pieces, owners = [], []
    for s in range(GLA_BLOCK):
        hs = s // half
        k_s = kb[s:s + 1, :]
        b_s = bb[s:s + 1, :]
        for ht in range(2):
            if (ht > hs) if reverse else (ht < hs):
                continue
            decay = jnp.exp2(b_half[ht] - b_s)
            if ht == hs:
                seen = (t_in <= s - hs * half) if reverse else (t_in >= s - hs * half)
                decay = jnp.where(seen, decay, 0.0)
            pieces.append(q_half[ht] * k_s * decay)
            owners.append((ht, s))
    a_exp = _dot(jnp.concatenate(pieces, axis=0).astype(BF16), expand)
    o_intra = [jnp.zeros((half, GLA_WIDTH), F32), jnp.zeros((half, GLA_WIDTH), F32)]
    for i, (ht, s) in enumerate(owners):
        o_intra[ht] = o_intra[ht] + a_exp[i * half:(i + 1) * half, :] * vb[s:s + 1, :]
    o_ref[bi, rows, :] = o_inter + jnp.concatenate(o_intra, axis=0)

    k_hat = (kb * jnp.exp2(b_end - bb)).astype(BF16)
    upd = _dot_tn(vb.astype(BF16), k_hat)
    st_ref[bi] = st * jnp.exp2(b_end) + upd * head_mask


def _gla_kernel(qf_ref, kf_ref, vf_ref, laf_ref, qb_ref, kb_ref, vb_ref, lab_ref, s0f_ref, s0b_ref,
                trif_ref, trib_ref, e_ref, of_ref, ob_ref, sff_ref, sfb_ref,
                stf_ref, stb_ref, bf_ref, bb_ref, *, tm, nb):
    n = pl.program_id(1)
    n_blocks = tm // GLA_BLOCK

    @pl.when(n == 0)
    def _():
        stf_ref[...] = s0f_ref[...]
        stb_ref[...] = s0b_ref[...]

    for la_ref, tri_ref, b_ref in ((laf_ref, trif_ref, bf_ref), (lab_ref, trib_ref, bb_ref)):
        tri = tri_ref[...]
        for bi in range(nb):
            g_hi, g_mid, _ = _split3(la_ref[bi])
            sums = _dot(tri, jnp.concatenate([g_hi, g_mid], axis=-1))
            b_ref[bi] = sums[:, :GLA_KEY_WIDTH] + sums[:, GLA_KEY_WIDTH:]

    expand = e_ref[...]
    head_v = lax.broadcasted_iota(jnp.int32, (GLA_WIDTH, GLA_KEY_WIDTH), 0) // GLA_DV
    head_k = lax.broadcasted_iota(jnp.int32, (GLA_WIDTH, GLA_KEY_WIDTH), 1) // GLA_DK
    head_mask = jnp.where(head_v == head_k, 1.0, 0.0)

    def block(j, carry):
        r_f = pl.multiple_of(j * GLA_BLOCK, GLA_BLOCK)
        r_b = pl.multiple_of((n_blocks - 1 - j) * GLA_BLOCK, GLA_BLOCK)
        for bi in range(nb):
            _gla_block(qf_ref, kf_ref, vf_ref, bf_ref, stf_ref, of_ref, expand, head_mask, bi, r_f, reverse=False)
            _gla_block(qb_ref, kb_ref, vb_ref, bb_ref, stb_ref, ob_ref, expand, head_mask, bi, r_b, reverse=True)
        return carry

    lax.fori_loop(0, n_blocks, block, 0)

    @pl.when(n == pl.num_programs(1) - 1)
    def _():
        sff_ref[...] = stf_ref[...]
        sfb_ref[...] = stb_ref[...]


def _block_tri(tm, reverse):
    t = np.arange(tm)[:, None]
    s = np.arange(tm)[None, :]
    same = (t // GLA_BLOCK) == (s // GLA_BLOCK)
    return jnp.asarray(same & ((s >= t) if reverse else (s <= t)), BF16)


def _gla_scan(q, k, v, la, s0f_t, s0b_t, expand):
    B, L, _ = q.shape
    tm = min(GLA_TILE, L)
    n_tiles = L // tm
    nb = math.gcd(B, GLA_BATCH_PER_STEP)
    fwd = lambda b, n: (b, n, 0)
    bwd = lambda b, n: (b, n_tiles - 1 - n, 0)
    state = pl.BlockSpec((nb, GLA_WIDTH, GLA_KEY_WIDTH), lambda b, n: (b, 0, 0))
    const = lambda shape: pl.BlockSpec(shape, lambda b, n: (0, 0))

    def operands(tmap, la_col):
        return [pl.BlockSpec((nb, tm, GLA_KEY_WIDTH), tmap), pl.BlockSpec((nb, tm, GLA_KEY_WIDTH), tmap),
                pl.BlockSpec((nb, tm, GLA_WIDTH), tmap),
                pl.BlockSpec((nb, tm, GLA_KEY_WIDTH), lambda b, n: tmap(b, n)[:2] + (la_col,))]

    return pl.pallas_call(
        functools.partial(_gla_kernel, tm=tm, nb=nb),
        grid=(B // nb, n_tiles),
        in_specs=operands(fwd, 0) + operands(bwd, 1) + [state, state, const((tm, tm)), const((tm, tm)),
                                                        const((GLA_KEY_WIDTH, GLA_WIDTH))],
        out_specs=[pl.BlockSpec((nb, tm, GLA_WIDTH), fwd), pl.BlockSpec((nb, tm, GLA_WIDTH), bwd), state, state],
        out_shape=[jax.ShapeDtypeStruct((B, L, GLA_WIDTH), F32), jax.ShapeDtypeStruct((B, L, GLA_WIDTH), F32),
                   jax.ShapeDtypeStruct((B, GLA_WIDTH, GLA_KEY_WIDTH), F32),
                   jax.ShapeDtypeStruct((B, GLA_WIDTH, GLA_KEY_WIDTH), F32)],
        scratch_shapes=[pltpu.VMEM((nb, GLA_WIDTH, GLA_KEY_WIDTH), F32), pltpu.VMEM((nb, GLA_WIDTH, GLA_KEY_WIDTH), F32),
                        pltpu.VMEM((nb, tm, GLA_KEY_WIDTH), F32), pltpu.VMEM((nb, tm, GLA_KEY_WIDTH), F32)],
        compiler_params=pltpu.CompilerParams(vmem_limit_bytes=VMEM_LIMIT),
        name="gla",
    )(q, k, v, la, q, k, v, la, s0f_t, s0b_t, _block_tri(tm, False), _block_tri(tm, True), expand)


def _state_to_kernel(s):
    B = s.shape[0]
    same_head = np.eye(GLA_HEADS, dtype=bool)[None, :, None, :, None]
    st = jnp.where(same_head, s.transpose(0, 1, 3, 2)[:, :, :, None, :], 0.0)
    return st.reshape(B, GLA_WIDTH, GLA_KEY_WIDTH)


def _state_from_kernel(st):
    B = st.shape[0]
    blocks = st.reshape(B, GLA_HEADS, GLA_DV, GLA_HEADS, GLA_DK)
    diag = jnp.stack([blocks[:, h, :, h, :] for h in range(GLA_HEADS)], axis=1)
    return diag.transpose(0, 1, 3, 2)


def _post_kernel(x_ref, m_ref, yp_ref, yn_ref, of_ref, ob_ref, lg_ref, gg_ref, g2_ref,
                 wo_ref, wi_ref, wf_ref, o_ref, acc_ref):
    gate1 = m_ref[0, 2:3, :]
    shift2 = m_ref[0, 3:4, :]
    scale2 = m_ref[0, 4:5, :]
    gate2 = m_ref[0, 5:6, :]

    o_la = _half_lane_rms(of_ref[0] + ob_ref[0], NORM_EPS) * gg_ref[...]
    y_la = (o_la * _silu(lg_ref[0])).astype(BF16)
    mixed = (_dot(yp_ref[0], wo_ref[0:POOL_WIDTH, :])
             + _dot(yn_ref[0], wo_ref[POOL_WIDTH:POOL_WIDTH + NA_WIDTH, :])
             + _dot(y_la, wo_ref[POOL_WIDTH + NA_WIDTH:D_MODEL, :]))
    x1 = x_ref[0] + gate1 * mixed

    h2 = (_row_rms(x1, NORM_EPS) * g2_ref[...] * (1.0 + scale2) + shift2).astype(BF16)
    for c0 in range(0, D_FF, FF_CHUNK):
        hg = _dot(h2, wi_ref[:, c0:c0 + FF_CHUNK])
        hu = _dot(h2, wi_ref[:, D_FF + c0:D_FF + c0 + FF_CHUNK])
        acc_ref[:, c0:c0 + FF_CHUNK] = (_silu(hg) * hu).astype(BF16)
    o_ref[0] = x1 + gate2 * _dot(acc_ref[...], wf_ref[...])


def _post(x, mods_l, row_of_batch, y_pool, y_na, o_f, o_b, lg, lw):
    B, L, _ = x.shape
    tm = min(ROW_TILE, L)
    const = lambda b, i: (0, 0)
    tile = lambda b, i: (b, i, 0)

    def resident(shape):
        return pl.BlockSpec(shape, const, pipeline_mode=pl.Buffered(1))

    return pl.pallas_call(
        _post_kernel,
        grid=(B, L // tm),
        in_specs=[
            pl.BlockSpec((1, tm, D_MODEL), tile),
            pl.BlockSpec((1, N_MODS, D_MODEL), lambda b, i: (row_of_batch(b), 0, 0)),
            pl.BlockSpec((1, tm, POOL_WIDTH), tile),
            pl.BlockSpec((1, tm, NA_WIDTH), tile),
            pl.BlockSpec((1, tm, GLA_WIDTH), tile),
            pl.BlockSpec((1, tm, GLA_WIDTH), tile),
            pl.BlockSpec((1, tm, GLA_WIDTH), tile),
            pl.BlockSpec((1, GLA_WIDTH), const),
            pl.BlockSpec((1, D_MODEL), const),
            resident((D_MODEL, D_MODEL)),
            resident((D_MODEL, 2 * D_FF)),
            resident((D_FF, D_MODEL)),
        ],
        out_specs=pl.BlockSpec((1, tm, D_MODEL), tile),
        out_shape=jax.ShapeDtypeStruct((B, L, D_MODEL), F32),
        scratch_shapes=[pltpu.VMEM((tm, D_FF), BF16)],
        compiler_params=pltpu.CompilerParams(vmem_limit_bytes=VMEM_LIMIT),
        name="post",
    )(x, mods_l, y_pool, y_na, o_f, o_b, lg, lw["gla_gain"], lw["norm2_gain"],
      lw["w_out"], lw["w_ffn_in"], lw["w_ffn_out"])


def _rope_tables(L):
    t = jnp.arange(L)
    row = (t // GRID_W).astype(F32)
    col = (t % GRID_W).astype(F32)
    half = GLA_DK // 2
    inv_freq = ROPE_THETA ** (-jnp.arange(0, half, 2, dtype=F32) / half)
    ang_r = row[:, None] * inv_freq
    ang_c = col[:, None] * inv_freq
    sign = jnp.concatenate([-jnp.ones((half // 2,), F32), jnp.ones((half // 2,), F32)])

    def lanes(fn, signed):
        per_axis = []
        for ang in (ang_r, ang_c):
            v = jnp.concatenate([fn(ang), fn(ang)], axis=-1)
            per_axis.append(v * sign if signed else v)
        return jnp.tile(jnp.concatenate(per_axis, axis=-1), (1, GLA_HEADS))

    return lanes(jnp.cos, False), lanes(jnp.sin, True)


def _layer_weights(l, w):
    tile_heads = lambda g, n: jnp.tile(g, n)[None, :]
    return {
        "norm1_gain": w["norm1_gain"][l][None, :],
        "norm2_gain": w["norm2_gain"][l][None, :],
        "w_in": w["w_in"][l].astype(BF16),
        "q_gain": tile_heads(w["q_norm_gain"][l], NA_HEADS),
        "k_gain": tile_heads(w["k_norm_gain"][l], NA_HEADS),
        "w_gate": jax.scipy.linalg.block_diag(w["w_gate_f"][l], w["w_gate_b"][l]).astype(BF16),
        "b_gate": jnp.concatenate([w["b_gate_f"][l], w["b_gate_b"][l]])[None, :],
        "w_pool_bd": jax.scipy.linalg.block_diag(*[w["w_pool"][l, g] for g in range(len(POOL_WINDOWS))]).astype(BF16),
        "pool_scale": w["pool_scale"][l][None, :],
        "gla_gain": tile_heads(w["gla_norm_gain"][l], GLA_HEADS),
        "w_out": w["w_out"][l].astype(BF16),
        "w_ffn_in": w["w_ffn_in"][l].astype(BF16),
        "w_ffn_out": w["w_ffn_out"][l].astype(BF16),
    }


def _trunk_layer(x, mods_l, row_of_batch, lw, expand, s0_f, s0_b, latent, kv_out=None):
    is_ctx = latent is None
    B, L, _ = x.shape
    per_token = (lambda a: a.reshape(1, B * L, a.shape[-1])) if is_ctx else (lambda a: a)
    per_seq = lambda a: a.reshape(B, L, a.shape[-1])
    outs = _in_proj(per_token(x), mods_l, row_of_batch, lw, None if is_ctx else latent[0],
                    kv_out + (L,) if is_ctx else None)
    u, q, k, v, lq, lk, lv, lg, la = [per_seq(a) for a in outs[:9]]
    y_pool = _pool(u, lw)
    if is_ctx:
        y_na = _ctx_attn(q, k, v)
    else:
        y_na = _lat_attn(q, k, v, latent[1], latent[2], latent[3])
    o_f, o_b, s_f, s_b = _gla_scan(lq, lk, lv, la, s0_f, s0_b, expand)
    x = per_seq(_post(per_token(x), mods_l, row_of_batch,
                      *[per_token(a) for a in (y_pool, y_na, o_f, o_b, lg)], lw))
    return x, s_f, s_b, tuple(outs[9:])


def kernel(x_prompt, x_sample, c, cache_na_k, cache_na_v, state_gla_fwd, state_gla_bwd, c_ctx, w_ada, b_ada, norm1_gain, norm2_gain, w_in, w_pool, pool_scale, q_norm_gain, k_norm_gain, rel_bias, w_gate_f, b_gate_f, w_gate_b, b_gate_b, gla_norm_gain, w_out, w_ffn_in, w_ffn_out):
    weights = dict(norm1_gain=norm1_gain, norm2_gain=norm2_gain, w_in=w_in, w_pool=w_pool, pool_scale=pool_scale,
                   q_norm_gain=q_norm_gain, k_norm_gain=k_norm_gain, w_gate_f=w_gate_f, b_gate_f=b_gate_f,
                   w_gate_b=w_gate_b, b_gate_b=b_gate_b, gla_norm_gain=gla_norm_gain, w_out=w_out,
                   w_ffn_in=w_ffn_in, w_ffn_out=w_ffn_out)
    B_ctx, L_ctx, _ = x_prompt.shape
    B_lat, L_lat, _ = x_sample.shape
    assert 1 + B_lat <= COND_ROWS

    conds = jnp.zeros((COND_ROWS, D_MODEL), F32).at[0].set(c_ctx).at[1:1 + B_lat].set(c)
    mods = _ada(conds, w_ada, b_ada).reshape(DEPTH, COND_ROWS, N_MODS, D_MODEL)
    lws = [_layer_weights(l, weights) for l in range(DEPTH)]
    head_of_k = np.arange(GLA_KEY_WIDTH) // GLA_DK
    head_of_v = np.arange(GLA_WIDTH) // GLA_DV
    expand = jnp.asarray(head_of_k[:, None] == head_of_v[None, :], BF16)

    xp = x_prompt
    zero_state = jnp.zeros((B_ctx, GLA_WIDTH, GLA_KEY_WIDTH), F32)
    new_k = jnp.zeros((B_ctx, DEPTH, NA_HEADS, L_ctx, NA_HEAD_DIM), F32)
    new_v = jnp.zeros((B_ctx, DEPTH, NA_HEADS, L_ctx, NA_HEAD_DIM), F32)
    sfs, sbs = [], []
    for l in range(DEPTH):
        xp, s_f, s_b, (new_k, new_v) = _trunk_layer(xp, mods[l], lambda b: 0, lws[l], expand,
                                                     zero_state, zero_state, None, (new_k, new_v, l))
        sfs.append(_state_from_kernel(s_f))
        sbs.append(_state_from_kernel(s_b))

    xs = x_sample
    rope_tabs = _rope_tables(L_lat)
    from_heads = lambda a: a.transpose(0, 2, 1, 3).reshape(B_lat, a.shape[2], NA_WIDTH).astype(BF16)
    for l in range(DEPTH):
        latent = (rope_tabs, from_heads(cache_na_k[:, l]), from_heads(cache_na_v[:, l]), _bias_tiles(rel_bias[l]))
        xs, _, _, _ = _trunk_layer(xs, mods[l], lambda b: b + 1, lws[l], expand,
                                   _state_to_kernel(state_gla_fwd[:, l]), _state_to_kernel(state_gla_bwd[:, l]),
                                   latent)

    return (xp, xs, new_k, new_v, jnp.stack(sfs, axis=1), jnp.stack(sbs, axis=1))
```

```python
import functools
import math

import numpy as np
import jax
import jax.numpy as jnp
from jax import lax
from jax.experimental import pallas as pl
from jax.experimental.pallas import tpu as pltpu

F32 = jnp.float32
BF16 = jnp.bfloat16

D_MODEL = 1024
DEPTH = 2
GRID_W = 64
POOL_WIDTH = 256
POOL_GROUP_DIM = 64
POOL_WINDOWS = (2, 4, 8, 16)
POOL_HALO = max(POOL_WINDOWS) // 2
POOL_PAD = 8 * len(POOL_WINDOWS)
NA_HEADS = 8
NA_HEAD_DIM = 64
NA_WIDTH = NA_HEADS * NA_HEAD_DIM
NA_WIN_ROWS = 8
NA_WIN_COLS = 16
NA_ROWS_PER_STEP = 8
GLA_HEADS = 4
GLA_DV = 64
GLA_DK = 32
GLA_WIDTH = GLA_HEADS * GLA_DV
GLA_KEY_WIDTH = GLA_HEADS * GLA_DK
GLA_GATE_RANK = 16
GLA_GATE_TAU = 16.0
GLA_BLOCK = 16
ROPE_THETA = 10000.0
D_FF = 2816
NORM_EPS = 1e-6
N_MODS = 6
COND_ROWS = 16
MASK_VALUE = -1e30
LOG2_E = 1.4426950408889634

OFF_POOL = 0
OFF_NA_Q = OFF_POOL + POOL_WIDTH
OFF_NA_K = OFF_NA_Q + NA_WIDTH
OFF_NA_V = OFF_NA_K + NA_WIDTH
OFF_LA_Q = OFF_NA_V + NA_WIDTH
OFF_LA_K = OFF_LA_Q + GLA_KEY_WIDTH
OFF_LA_V = OFF_LA_K + GLA_KEY_WIDTH
OFF_LA_G = OFF_LA_V + GLA_WIDTH
OFF_LR = OFF_LA_G + GLA_WIDTH
IN_WIDTH = OFF_LR + 2 * GLA_GATE_RANK

LANES = 128
ROW_TILE = 512
IN_ROW_TILE = 1024
IN_SUB_TILE = 256
GLA_TILE = 256
GLA_BATCH_PER_STEP = 8
FF_CHUNK = 256
ADA_COL_TILE = 512
VMEM_LIMIT = 56 * 1024 * 1024


def _dot(a, b):
    return jnp.dot(a, b, preferred_element_type=F32)


def _dot_nt(a, b):
    return lax.dot_general(a, b, (((1,), (1,)), ((), ())), preferred_element_type=F32)


def _dot_tn(a, b):
    return lax.dot_general(a, b, (((0,), (0,)), ((), ())), preferred_element_type=F32)


def _silu(x):
    return x * (1.0 / (1.0 + jnp.exp(-x)))


def _split3(x):
    hi = x.astype(BF16)
    r1 = x - hi.astype(F32)
    mid = r1.astype(BF16)
    lo = (r1 - mid.astype(F32)).astype(BF16)
    return hi, mid, lo


def _half_lane_rms(x, eps):
    lane = lax.broadcasted_iota(jnp.int32, (1, LANES), 1)
    low = lane < NA_HEAD_DIM
    cols = []
    for j in range(x.shape[-1] // LANES):
        blk = x[:, j * LANES:(j + 1) * LANES]
        sq = blk * blk
        s_lo = jnp.sum(jnp.where(low, sq, 0.0), axis=-1, keepdims=True)
        s_hi = jnp.sum(jnp.where(low, 0.0, sq), axis=-1, keepdims=True)
        r_lo = lax.rsqrt(s_lo * (1.0 / NA_HEAD_DIM) + eps)
        r_hi = lax.rsqrt(s_hi * (1.0 / NA_HEAD_DIM) + eps)
        cols.append(blk * jnp.where(low, r_lo, r_hi))
    return jnp.concatenate(cols, axis=-1)


def _row_rms(x, eps):
    return x * lax.rsqrt(jnp.mean(x * x, axis=-1, keepdims=True) + eps)


def _ada_kernel(c_ref, w_ref, b_ref, o_ref):
    s_hi, s_mid, _ = _split3(_silu(c_ref[...]))
    w = w_ref[0]
    w_hi, w_mid, _ = _split3(w)
    acc = _dot(s_hi, w_hi) + _dot(s_mid, w_hi) + _dot(s_hi, w_mid)
    o_ref[0] = acc + b_ref[0]


def _ada(conds, w_ada, b_ada):
    n_out = w_ada.shape[-1]
    return pl.pallas_call(
        _ada_kernel,
        grid=(DEPTH, n_out // ADA_COL_TILE),
        in_specs=[
            pl.BlockSpec((COND_ROWS, D_MODEL), lambda l, j: (0, 0)),
            pl.BlockSpec((1, D_MODEL, ADA_COL_TILE), lambda l, j: (l, 0, j)),
            pl.BlockSpec((1, 1, ADA_COL_TILE), lambda l, j: (l, 0, j)),
        ],
        out_specs=pl.BlockSpec((1, COND_ROWS, ADA_COL_TILE), lambda l, j: (l, 0, j)),
        out_shape=jax.ShapeDtypeStruct((DEPTH, COND_ROWS, n_out), F32),
        name="ada",
    )(conds, w_ada, b_ada.reshape(DEPTH, 1, n_out))


def _inproj_kernel(*refs, rope, kv_seq):
    x_ref, m_ref, g1_ref, w_ref, qg_ref, kg_ref, wg_ref, bg_ref = refs[:8]
    pos = 8
    if rope:
        cos_ref, sin_ref = refs[pos:pos + 2]
        pos += 2
    if kv_seq:
        pos += 2
    u_ref, q_ref, k_ref, v_ref, lq_ref, lk_ref, lv_ref, lg_ref, la_ref = refs[pos:pos + 9]
    pos += 9
    if kv_seq:
        kf_ref, vf_ref = refs[pos:pos + 2]

    shift1 = m_ref[0, 0:1, :]
    scale1 = m_ref[0, 1:2, :]
    tm = x_ref.shape[1]
    sub = min(IN_SUB_TILE, tm)

    def normed(i):
        rows = slice(i * sub, (i + 1) * sub)
        h = _row_rms(x_ref[0, rows, :], NORM_EPS) * g1_ref[...] * (1.0 + scale1) + shift1
        return h.astype(BF16)

    def project(i, hb):
        rows = slice(i * sub, (i + 1) * sub)

        def proj(off, width):
            return _dot(hb, w_ref[:, off:off + width])

        u_ref[0, rows, :] = proj(OFF_POOL, POOL_WIDTH)

        qn = _half_lane_rms(proj(OFF_NA_Q, NA_WIDTH), NORM_EPS) * qg_ref[...]
        q_ref[0, rows, :] = (qn * (NA_HEAD_DIM ** -0.5 * LOG2_E)).astype(BF16)
        kn = _half_lane_rms(proj(OFF_NA_K, NA_WIDTH), NORM_EPS) * kg_ref[...]
        k_ref[0, rows, :] = kn.astype(BF16)
        vn = proj(OFF_NA_V, NA_WIDTH)
        v_ref[0, rows, :] = vn.astype(BF16)
        if kv_seq:
            piece = min(sub, kv_seq)
            for j in range(sub // piece):
                first = i * sub + j * piece
                seq, off = (first // kv_seq, first % kv_seq) if tm >= kv_seq else (0, first)
                for hh in range(NA_HEADS):
                    cols = slice(hh * NA_HEAD_DIM, (hh + 1) * NA_HEAD_DIM)
                    kf_ref[seq, 0, hh, off:off + piece, :] = kn[j * piece:(j + 1) * piece, cols]
                    vf_ref[seq, 0, hh, off:off + piece, :] = vn[j * piece:(j + 1) * piece, cols]

        lqk = proj(OFF_LA_Q, 2 * GLA_KEY_WIDTH)
        lq = lqk[:, :GLA_KEY_WIDTH] * (GLA_DK ** -0.5)
        lk = lqk[:, GLA_KEY_WIDTH:]
        if rope:
            lane = lax.broadcasted_iota(jnp.int32, (1, LANES), 1)
            first = (lane % 16) < 8
            cos = cos_ref[rows, :]
            sin = sin_ref[rows, :]

            def rot(t):
                partner = jnp.where(first, pltpu.roll(t, LANES - 8, axis=1), pltpu.roll(t, 8, axis=1))
                return t * cos + partner * sin

            lq = rot(lq)
            lk = rot(lk)
        lq_ref[0, rows, :] = lq
        lk_ref[0, rows, :] = lk
        lv_ref[0, rows, :] = proj(OFF_LA_V, GLA_WIDTH)
        lg_ref[0, rows, :] = proj(OFF_LA_G, GLA_WIDTH)

        lr = proj(OFF_LR, 2 * GLA_GATE_RANK)
        z = _dot(lr.astype(BF16), wg_ref[...]) + bg_ref[...]
        log_sig = jnp.minimum(z, 0.0) - jnp.log1p(jnp.exp(-jnp.abs(z)))
        la_ref[0, rows, :] = log_sig * (LOG2_E / GLA_GATE_TAU)

    hb_next = normed(0)
    for i in range(tm // sub):
        hb = hb_next
        hb_next = normed(i + 1) if (i + 1) * sub < tm else None
        project(i, hb)


def _in_proj(x, mods_l, row_of_batch, lw, rope_tabs, kv_out):
    B, L, _ = x.shape
    tm = min(IN_ROW_TILE, L)
    rope = rope_tabs is not None
    const = lambda b, i: (0, 0)
    tile = lambda b, i: (b, i, 0)

    in_specs = [
        pl.BlockSpec((1, tm, D_MODEL), tile),
        pl.BlockSpec((1, N_MODS, D_MODEL), lambda b, i: (row_of_batch(b), 0, 0)),
        pl.BlockSpec((1, D_MODEL), const),
        pl.BlockSpec((D_MODEL, IN_WIDTH), const),
        pl.BlockSpec((1, NA_WIDTH), const),
        pl.BlockSpec((1, NA_WIDTH), const),
        pl.BlockSpec((2 * GLA_GATE_RANK, 2 * GLA_KEY_WIDTH), const),
        pl.BlockSpec((1, 2 * GLA_KEY_WIDTH), const),
    ]
    args = [x, mods_l, lw["norm1_gain"], lw["w_in"], lw["q_gain"], lw["k_gain"], lw["w_gate"], lw["b_gate"]]
    if rope:
        in_specs += [pl.BlockSpec((tm, GLA_KEY_WIDTH), lambda b, i: (i, 0))] * 2
        args += list(rope_tabs)

    widths = [(POOL_WIDTH, F32), (NA_WIDTH, BF16), (NA_WIDTH, BF16), (NA_WIDTH, BF16),
              (GLA_KEY_WIDTH, F32), (GLA_KEY_WIDTH, F32), (GLA_WIDTH, F32), (GLA_WIDTH, F32),
              (2 * GLA_KEY_WIDTH, F32)]
    out_specs = [pl.BlockSpec((1, tm, w), tile) for w, _ in widths]
    out_shape = [jax.ShapeDtypeStruct((B, L, w), dt) for w, dt in widths]
    aliases = {}
    kv_seq = None
    if kv_out is not None:
        k_buf, v_buf, layer, kv_seq = kv_out
        assert B == 1 and (tm % kv_seq == 0 or kv_seq % tm == 0)
        if tm >= kv_seq:
            kv_spec = pl.BlockSpec((tm // kv_seq, 1, NA_HEADS, kv_seq, NA_HEAD_DIM), lambda b, i: (i, layer, 0, 0, 0))
        else:
            per_seq = kv_seq // tm
            kv_spec = pl.BlockSpec((1, 1, NA_HEADS, tm, NA_HEAD_DIM),
                                   lambda b, i: (i // per_seq, layer, 0, i % per_seq, 0))
        for buf in (k_buf, v_buf):
            aliases[len(args)] = len(out_specs)
            in_specs.append(pl.BlockSpec(memory_space=pl.ANY))
            args.append(buf)
            out_specs.append(kv_spec)
            out_shape.append(jax.ShapeDtypeStruct(buf.shape, buf.dtype))

    return pl.pallas_call(
        functools.partial(_inproj_kernel, rope=rope, kv_seq=kv_seq),
        grid=(B, L // tm),
        in_specs=in_specs,
        out_specs=out_specs,
        out_shape=out_shape,
        input_output_aliases=aliases,
        compiler_params=pltpu.CompilerParams(vmem_limit_bytes=VMEM_LIMIT),
        name="in_proj",
    )(*args)


def _pool_kernel(u_ref, w_ref, sc_ref, o_ref, pad_ref, s2_ref, s4_ref, s8_ref, *, L, tp):
    P = POOL_PAD
    zeros = jnp.zeros((P, POOL_WIDTH), F32)
    pad_ref[0:P, :] = zeros
    pad_ref[P + L:P + L + P, :] = zeros
    pad_ref[P:P + L, :] = u_ref[0]
    narrow = slice(0, LANES)
    wide = slice(LANES, 2 * LANES)
    lane = lax.broadcasted_iota(jnp.int32, (1, LANES), 1)
    low = lane < POOL_GROUP_DIM

    def chunks(level):
        lo, hi = 8 * level, L + 2 * P - 8 * level
        return [(a, min(a + tp, hi)) for a in range(lo, hi, tp)]

    for a, b in chunks(1):
        s2_ref[a:b, :] = pad_ref[a - 1:b - 1, wide] + pad_ref[a:b, wide]
    for a, b in chunks(2):
        s4_ref[a:b, :] = s2_ref[a - 1:b - 1, :] + s2_ref[a + 1:b + 1, :]
    for a, b in chunks(3):
        s8_ref[a:b, :] = s4_ref[a - 2:b - 2, :] + s4_ref[a + 2:b + 2, :]

    for base in range(0, L, tp):
        p0 = P + base

        def rows(ref, off, cols):
            return ref[p0 + off:p0 + off + tp, cols]

        u_a = rows(pad_ref, 0, narrow)
        w2 = rows(pad_ref, -1, narrow) + u_a
        w4 = w2 + rows(pad_ref, -2, narrow) + rows(pad_ref, 1, narrow)
        u_b = rows(pad_ref, 0, wide)
        w8 = rows(s8_ref, 0, slice(None))
        w16 = rows(s8_ref, -4, slice(None)) + rows(s8_ref, 4, slice(None))

        if base < POOL_HALO or base + tp > L - POOL_HALO:
            t = base + lax.broadcasted_iota(jnp.int32, (tp, LANES), 0)

            def mean(total, win):
                count = jnp.minimum(t + win // 2, L) - jnp.maximum(t - win // 2, 0)
                return total / count.astype(F32)
        else:
            def mean(total, win):
                return total * (1.0 / win)

        mean_a = jnp.where(low, mean(w2, 2), mean(w4, 4))
        mean_b = jnp.where(low, mean(w8, 8), mean(w16, 16))
        d = jnp.concatenate([mean_a - u_a, mean_b - u_b], axis=-1).astype(BF16)
        y = _dot(d, w_ref[...]) * sc_ref[...]
        o_ref[0, base:base + tp, :] = y.astype(BF16)


def _pool(u, lw):
    B, L, _ = u.shape
    tp = min(512, L)
    return pl.pallas_call(
        functools.partial(_pool_kernel, L=L, tp=tp),
        grid=(B,),
        in_specs=[
            pl.BlockSpec((1, L, POOL_WIDTH), lambda b: (b, 0, 0)),
            pl.BlockSpec((POOL_WIDTH, POOL_WIDTH), lambda b: (0, 0)),
            pl.BlockSpec((1, POOL_WIDTH), lambda b: (0, 0)),
        ],
        out_specs=pl.BlockSpec((1, L, POOL_WIDTH), lambda b: (b, 0, 0)),
        out_shape=jax.ShapeDtypeStruct((B, L, POOL_WIDTH), BF16),
        scratch_shapes=[pltpu.VMEM((L + 2 * POOL_PAD, POOL_WIDTH), F32)]
                       + [pltpu.VMEM((L + 2 * POOL_PAD, LANES), F32)] * 3,
        compiler_params=pltpu.CompilerParams(vmem_limit_bytes=VMEM_LIMIT),
        name="pool",
    )(u, lw["w_pool_bd"], lw["pool_scale"])


def _ctx_attn_kernel(q_ref, k_ref, v_ref, o_ref):
    L = q_ref.shape[1]
    lane = lax.broadcasted_iota(jnp.int32, (1, LANES), 1)
    even = lane < NA_HEAD_DIM
    zero = jnp.zeros((), BF16)

    def scores(j):
        cols = slice(j * LANES, (j + 1) * LANES)
        qp = q_ref[0, :, cols]
        q2 = jnp.concatenate([jnp.where(even, qp, zero), jnp.where(even, zero, qp)], axis=0)
        return _dot_nt(q2, k_ref[0, :, cols])

    def finish(j, s):
        cols = slice(j * LANES, (j + 1) * LANES)
        p = jnp.exp2(s - jnp.max(s, axis=-1, keepdims=True))
        denom = jnp.sum(p, axis=-1, keepdims=True)
        o2 = _dot(p.astype(BF16), v_ref[0, :, cols]) / denom
        o_ref[0, :, cols] = jnp.where(even, o2[:L], o2[L:]).astype(BF16)

    n_pairs = NA_HEADS // 2
    s_next = scores(0)
    for j in range(n_pairs):
        s_cur = s_next
        s_next = scores(j + 1) if j + 1 < n_pairs else None
        finish(j, s_cur)


def _ctx_attn(q, k, v):
    B, L, _ = q.shape
    spec = pl.BlockSpec((1, L, NA_WIDTH), lambda b: (b, 0, 0))
    return pl.pallas_call(
        _ctx_attn_kernel,
        grid=(B,),
        in_specs=[spec, spec, spec],
        out_specs=spec,
        out_shape=jax.ShapeDtypeStruct((B, L, NA_WIDTH), BF16),
        name="ctx_attn",
    )(q, k, v)


def _band_start(r, n_rows):
    return jnp.clip(r - NA_WIN_ROWS // 2, 0, n_rows - NA_WIN_ROWS)


def _lat_attn_kernel(q_ref, k_ref, v_ref, kc_ref, vc_ref, bias_ref, o_ref, *, n_rows):
    band = NA_WIN_ROWS * GRID_W
    lane = lax.broadcasted_iota(jnp.int32, (1, LANES), 1)
    even = lane < NA_HEAD_DIM
    zero = jnp.zeros((), BF16)

    def band_of(rr):
        r = pl.program_id(1) * NA_ROWS_PER_STEP + rr
        first = _band_start(r, n_rows)
        return pl.multiple_of(first * GRID_W, GRID_W), NA_WIN_ROWS - 1 - (r - first)

    def scores(rr, j):
        start, tile0 = band_of(rr)
        cols = slice(j * LANES, (j + 1) * LANES)
        qp = q_ref[0, rr * GRID_W:(rr + 1) * GRID_W, cols]
        q2 = jnp.concatenate([jnp.where(even, qp, zero), jnp.where(even, zero, qp)], axis=0)
        bias = jnp.concatenate(
            [jnp.concatenate([bias_ref[2 * j + hh, tile0 + 2 * ii] for ii in range(NA_WIN_ROWS // 2)], axis=-1)
             for hh in range(2)], axis=0)
        return _dot_nt(q2, k_ref[0, pl.ds(start, band), cols]) + bias, _dot_nt(q2, kc_ref[0, :, cols])

    def softmax(s_loc, s_ctx):
        m = jnp.maximum(jnp.max(s_loc, axis=-1, keepdims=True), jnp.max(s_ctx, axis=-1, keepdims=True))
        p_loc = jnp.exp2(s_loc - m)
        p_ctx = jnp.exp2(s_ctx - m)
        denom = jnp.sum(p_loc, axis=-1, keepdims=True) + jnp.sum(p_ctx, axis=-1, keepdims=True)
        return p_loc.astype(BF16), p_ctx.astype(BF16), denom

    def values(rr, j, p_loc, p_ctx, denom):
        start, _ = band_of(rr)
        cols = slice(j * LANES, (j + 1) * LANES)
        o2 = (_dot(p_loc, v_ref[0, pl.ds(start, band), cols]) + _dot(p_ctx, vc_ref[0, :, cols])) / denom
        o_ref[0, rr * GRID_W:(rr + 1) * GRID_W, cols] = jnp.where(even, o2[:GRID_W], o2[GRID_W:]).astype(BF16)

    chains = [(rr, j) for rr in range(NA_ROWS_PER_STEP) for j in range(NA_HEADS // 2)]
    s_next = scores(*chains[0])
    p_prev = None
    for i, chain in enumerate(chains):
        s_cur = s_next
        s_next = scores(*chains[i + 1]) if i + 1 < len(chains) else None
        p_cur = softmax(*s_cur)
        if p_prev is not None:
            values(*chains[i - 1], *p_prev)
        p_prev = p_cur
    values(*chains[-1], *p_prev)


def _lat_attn(q, k, v, kc, vc, bias_tiles):
    B, L, _ = q.shape
    n_rows = L // GRID_W
    assert n_rows >= NA_WIN_ROWS and n_rows % NA_ROWS_PER_STEP == 0
    past = kc.shape[1]
    whole = pl.BlockSpec((1, L, NA_WIDTH), lambda b, r: (b, 0, 0))
    rows = pl.BlockSpec((1, NA_ROWS_PER_STEP * GRID_W, NA_WIDTH), lambda b, r: (b, r, 0))
    ctx = pl.BlockSpec((1, past, NA_WIDTH), lambda b, r: (b, 0, 0))
    bias = pl.BlockSpec(bias_tiles.shape, lambda b, r: (0, 0, 0, 0))
    return pl.pallas_call(
        functools.partial(_lat_attn_kernel, n_rows=n_rows),
        grid=(B, n_rows // NA_ROWS_PER_STEP),
        in_specs=[rows, whole, whole, ctx, ctx, bias],
        out_specs=rows,
        out_shape=jax.ShapeDtypeStruct((B, L, NA_WIDTH), BF16),
        compiler_params=pltpu.CompilerParams(vmem_limit_bytes=VMEM_LIMIT),
        name="lat_attn",
    )(q, k, v, kc, vc, bias_tiles)


def _bias_tile_constants():
    n_dc = 2 * NA_WIN_COLS
    w = np.arange(GRID_W)[:, None]
    cc = np.arange(2 * GRID_W)[None, :]
    c = cc % GRID_W
    cs = np.clip(w - NA_WIN_COLS // 2, 0, GRID_W - NA_WIN_COLS)
    valid = (c >= cs) & (c < cs + NA_WIN_COLS)
    k_idx = (cc // GRID_W) * n_dc + (c - w + NA_WIN_COLS - 1)
    onehot = (np.arange(2 * n_dc)[:, None, None] == k_idx[None]) & valid[None]
    onehot = onehot.reshape(2 * n_dc, GRID_W * 2 * GRID_W)
    mask = np.where(valid, 0.0, MASK_VALUE).reshape(1, GRID_W * 2 * GRID_W)
    return jnp.asarray(onehot, BF16), jnp.asarray(mask, F32)


def _bias_tiles_kernel(rb_ref, oh_ref, mask_ref, o_ref):
    hi, mid, lo = _split3(rb_ref[...])
    oh = oh_ref[...]
    o_ref[...] = (_dot(hi, oh) + _dot(mid, oh) + _dot(lo, oh)) * LOG2_E + mask_ref[...]


def _bias_tiles(rel_bias_l):
    n_dr = 2 * NA_WIN_ROWS - 1
    padded = jnp.pad(rel_bias_l, ((0, 0), (0, 0), (0, 1)))
    pairs = jnp.concatenate([padded[:, :-1], padded[:, 1:]], axis=-1)
    pairs = pairs.reshape(NA_HEADS * (n_dr - 1), 4 * NA_WIN_COLS)
    onehot, mask = _bias_tile_constants()
    full = lambda shape: pl.BlockSpec(shape, lambda: (0,) * len(shape))
    out = pl.pallas_call(
        _bias_tiles_kernel,
        in_specs=[full(pairs.shape), full(onehot.shape), full(mask.shape)],
        out_specs=full((pairs.shape[0], onehot.shape[1])),
        out_shape=jax.ShapeDtypeStruct((pairs.shape[0], onehot.shape[1]), F32),
        name="bias_tiles",
    )(pairs, onehot, mask)
    return out.reshape(NA_HEADS, n_dr - 1, GRID_W, 2 * GRID_W)


def _gla_block(q_ref, k_ref, v_ref, b_ref, st_ref, o_ref, expand, head_mask, bi, r0, reverse):
    half = GLA_BLOCK // 2
    rows = pl.ds(r0, GLA_BLOCK)
    qb = q_ref[bi, rows, :]
    kb = k_ref[bi, rows, :]
    vb = v_ref[bi, rows, :]
    bb = b_ref[bi, rows, :]
    end_row = 0 if reverse else GLA_BLOCK - 1
    b_end = bb[end_row:end_row + 1, :]
    st = st_ref[bi]

    o_inter = _dot_nt((qb * jnp.exp2(bb)).astype(BF16), st.astype(BF16))

    t_in = lax.broadcasted_iota(jnp.int32, (half, 1), 0)
    q_half = (qb[:half], qb[half:])
    b_half = (bb[:half], bb[half:])
    pieces, owners = [], []
    for s in range(GLA_BLOCK):
        hs = s // half
        k_s = kb[s:s + 1, :]
        b_s = bb[s:s + 1, :]
        for ht in range(2):
            if (ht > hs) if reverse else (ht < hs):
                continue
            decay = jnp.exp2(b_half[ht] - b_s)
            if ht == hs:
                seen = (t_in <= s - hs * half) if reverse else (t_in >= s - hs * half)
                decay = jnp.where(seen, decay, 0.0)
            pieces.append(q_half[ht] * k_s * decay)
            owners.append((ht, s))
    a_exp = _dot(jnp.concatenate(pieces, axis=0).astype(BF16), expand)
    o_intra = [jnp.zeros((half, GLA_WIDTH), F32), jnp.zeros((half, GLA_WIDTH), F32)]
    for i, (ht, s) in enumerate(owners):
        o_intra[ht] = o_intra[ht] + a_exp[i * half:(i + 1) * half, :] * vb[s:s + 1, :]
    o_ref[bi, rows, :] = o_inter + jnp.concatenate(o_intra, axis=0)

    k_hat = (kb * jnp.exp2(b_end - bb)).astype(BF16)
    upd = _dot_tn(vb.astype(BF16), k_hat)
    st_ref[bi] = st * jnp.exp2(b_end) + upd * head_mask


def _gla_kernel(qf_ref, kf_ref, vf_ref, laf_ref, qb_ref, kb_ref, vb_ref, lab_ref, s0f_ref, s0b_ref,
                trif_ref, trib_ref, e_ref, of_ref, ob_ref, sff_ref, sfb_ref,
                stf_ref, stb_ref, bf_ref, bb_ref, *, tm, nb):
    n = pl.program_id(1)
    n_blocks = tm // GLA_BLOCK

    @pl.when(n == 0)
    def _():
        stf_ref[...] = s0f_ref[...]
        stb_ref[...] = s0b_ref[...]

    for la_ref, tri_ref, b_ref in ((laf_ref, trif_ref, bf_ref), (lab_ref, trib_ref, bb_ref)):
        tri = tri_ref[...]
        for bi in range(nb):
            g_hi, g_mid, _ = _split3(la_ref[bi])
            sums = _dot(tri, jnp.concatenate([g_hi, g_mid], axis=-1))
            b_ref[bi] = sums[:, :GLA_KEY_WIDTH] + sums[:, GLA_KEY_WIDTH:]

    expand = e_ref[...]
    head_v = lax.broadcasted_iota(jnp.int32, (GLA_WIDTH, GLA_KEY_WIDTH), 0) // GLA_DV
    head_k = lax.broadcasted_iota(jnp.int32, (GLA_WIDTH, GLA_KEY_WIDTH), 1) // GLA_DK
    head_mask = jnp.where(head_v == head_k, 1.0, 0.0)

    def block(j, carry):
        r_f = pl.multiple_of(j * GLA_BLOCK, GLA_BLOCK)
        r_b = pl.multiple_of((n_blocks - 1 - j) * GLA_BLOCK, GLA_BLOCK)
        for bi in range(nb):
            _gla_block(qf_ref, kf_ref, vf_ref, bf_ref, stf_ref, of_ref, expand, head_mask, bi, r_f, reverse=False)
            _gla_block(qb_ref, kb_ref, vb_ref, bb_ref, stb_ref, ob_ref, expand, head_mask, bi, r_b, reverse=True)
        return carry

    lax.fori_loop(0, n_blocks, block, 0)

    @pl.when(n == pl.num_programs(1) - 1)
    def _():
        sff_ref[...] = stf_ref[...]
        sfb_ref[...] = stb_ref[...]


def _block_tri(tm, reverse):
    t = np.arange(tm)[:, None]
    s = np.arange(tm)[None, :]
    same = (t // GLA_BLOCK) == (s // GLA_BLOCK)
    return jnp.asarray(same & ((s >= t) if reverse else (s <= t)), BF16)


def _gla_scan(q, k, v, la, s0f_t, s0b_t, expand):
    B, L, _ = q.shape
    tm = min(GLA_TILE, L)
    n_tiles = L // tm
    nb = math.gcd(B, GLA_BATCH_PER_STEP)
    fwd = lambda b, n: (b, n, 0)
    bwd = lambda b, n: (b, n_tiles - 1 - n, 0)
    state = pl.BlockSpec((nb, GLA_WIDTH, GLA_KEY_WIDTH), lambda b, n: (b, 0, 0))
    const = lambda shape: pl.BlockSpec(shape, lambda b, n: (0, 0))

    def operands(tmap, la_col):
        return [pl.BlockSpec((nb, tm, GLA_KEY_WIDTH), tmap), pl.BlockSpec((nb, tm, GLA_KEY_WIDTH), tmap),
                pl.BlockSpec((nb, tm, GLA_WIDTH), tmap),
                pl.BlockSpec((nb, tm, GLA_KEY_WIDTH), lambda b, n: tmap(b, n)[:2] + (la_col,))]

    return pl.pallas_call(
        functools.partial(_gla_kernel, tm=tm, nb=nb),
        grid=(B // nb, n_tiles),
        in_specs=operands(fwd, 0) + operands(bwd, 1) + [state, state, const((tm, tm)), const((tm, tm)),
                                                        const((GLA_KEY_WIDTH, GLA_WIDTH))],
        out_specs=[pl.BlockSpec((nb, tm, GLA_WIDTH), fwd), pl.BlockSpec((nb, tm, GLA_WIDTH), bwd), state, state],
        out_shape=[jax.ShapeDtypeStruct((B, L, GLA_WIDTH), F32), jax.ShapeDtypeStruct((B, L, GLA_WIDTH), F32),
                   jax.ShapeDtypeStruct((B, GLA_WIDTH, GLA_KEY_WIDTH), F32),
                   jax.ShapeDtypeStruct((B, GLA_WIDTH, GLA_KEY_WIDTH), F32)],
        scratch_shapes=[pltpu.VMEM((nb, GLA_WIDTH, GLA_KEY_WIDTH), F32), pltpu.VMEM((nb, GLA_WIDTH, GLA_KEY_WIDTH), F32),
                        pltpu.VMEM((nb, tm, GLA_KEY_WIDTH), F32), pltpu.VMEM((nb, tm, GLA_KEY_WIDTH), F32)],
        compiler_params=pltpu.CompilerParams(vmem_limit_bytes=VMEM_LIMIT),
        name="gla",
    )(q, k, v, la, q, k, v, la, s0f_t, s0b_t, _block_tri(tm, False), _block_tri(tm, True), expand)


def _state_to_kernel(s):
    B = s.shape[0]
    same_head = np.eye(GLA_HEADS, dtype=bool)[None, :, None, :, None]
    st = jnp.where(same_head, s.transpose(0, 1, 3, 2)[:, :, :, None, :], 0.0)
    return st.reshape(B, GLA_WIDTH, GLA_KEY_WIDTH)


def _state_from_kernel(st):
    B = st.shape[0]
    blocks = st.reshape(B, GLA_HEADS, GLA_DV, GLA_HEADS, GLA_DK)
    diag = jnp.stack([blocks[:, h, :, h, :] for h in range(GLA_HEADS)], axis=1)
    return diag.transpose(0, 1, 3, 2)


def _post_kernel(x_ref, m_ref, yp_ref, yn_ref, of_ref, ob_ref, lg_ref, gg_ref, g2_ref,
                 wo_ref, wi_ref, wf_ref, o_ref, acc_ref):
    gate1 = m_ref[0, 2:3, :]
    shift2 = m_ref[0, 3:4, :]
    scale2 = m_ref[0, 4:5, :]
    gate2 = m_ref[0, 5:6, :]

    o_la = _half_lane_rms(of_ref[0] + ob_ref[0], NORM_EPS) * gg_ref[...]
    y_la = (o_la * _silu(lg_ref[0])).astype(BF16)
    mixed = (_dot(yp_ref[0], wo_ref[0:POOL_WIDTH, :])
             + _dot(yn_ref[0], wo_ref[POOL_WIDTH:POOL_WIDTH + NA_WIDTH, :])
             + _dot(y_la, wo_ref[POOL_WIDTH + NA_WIDTH:D_MODEL, :]))
    x1 = x_ref[0] + gate1 * mixed

    h2 = (_row_rms(x1, NORM_EPS) * g2_ref[...] * (1.0 + scale2) + shift2).astype(BF16)
    for c0 in range(0, D_FF, FF_CHUNK):
        hg = _dot(h2, wi_ref[:, c0:c0 + FF_CHUNK])
        hu = _dot(h2, wi_ref[:, D_FF + c0:D_FF + c0 + FF_CHUNK])
        acc_ref[:, c0:c0 + FF_CHUNK] = (_silu(hg) * hu).astype(BF16)
    o_ref[0] = x1 + gate2 * _dot(acc_ref[...], wf_ref[...])


def _post(x, mods_l, row_of_batch, y_pool, y_na, o_f, o_b, lg, lw):
    B, L, _ = x.shape
    tm = min(ROW_TILE, L)
    const = lambda b, i: (0, 0)
    tile = lambda b, i: (b, i, 0)

    def resident(shape):
        return pl.BlockSpec(shape, const, pipeline_mode=pl.Buffered(1))

    return pl.pallas_call(
        _post_kernel,
        grid=(B, L // tm),
        in_specs=[
            pl.BlockSpec((1, tm, D_MODEL), tile),
            pl.BlockSpec((1, N_MODS, D_MODEL), lambda b, i: (row_of_batch(b), 0, 0)),
            pl.BlockSpec((1, tm, POOL_WIDTH), tile),
            pl.BlockSpec((1, tm, NA_WIDTH), tile),
            pl.BlockSpec((1, tm, GLA_WIDTH), tile),
            pl.BlockSpec((1, tm, GLA_WIDTH), tile),
            pl.BlockSpec((1, tm, GLA_WIDTH), tile),
            pl.BlockSpec((1, GLA_WIDTH), const),
            pl.BlockSpec((1, D_MODEL), const),
            resident((D_MODEL, D_MODEL)),
            resident((D_MODEL, 2 * D_FF)),
            resident((D_FF, D_MODEL)),
        ],
        out_specs=pl.BlockSpec((1, tm, D_MODEL), tile),
        out_shape=jax.ShapeDtypeStruct((B, L, D_MODEL), F32),
        scratch_shapes=[pltpu.VMEM((tm, D_FF), BF16)],
        compiler_params=pltpu.CompilerParams(vmem_limit_bytes=VMEM_LIMIT),
        name="post",
    )(x, mods_l, y_pool, y_na, o_f, o_b, lg, lw["gla_gain"], lw["norm2_gain"],
      lw["w_out"], lw["w_ffn_in"], lw["w_ffn_out"])


def _rope_tables(L):
    t = jnp.arange(L)
    row = (t // GRID_W).astype(F32)
    col = (t % GRID_W).astype(F32)
    half = GLA_DK // 2
    inv_freq = ROPE_THETA ** (-jnp.arange(0, half, 2, dtype=F32) / half)
    ang_r = row[:, None] * inv_freq
    ang_c = col[:, None] * inv_freq
    sign = jnp.concatenate([-jnp.ones((half // 2,), F32), jnp.ones((half // 2,), F32)])

    def lanes(fn, signed):
        per_axis = []
        for ang in (ang_r, ang_c):
            v = jnp.concatenate([fn(ang), fn(ang)], axis=-1)
            per_axis.append(v * sign if signed else v)
        return jnp.tile(jnp.concatenate(per_axis, axis=-1), (1, GLA_HEADS))

    return lanes(jnp.cos, False), lanes(jnp.sin, True)


def _layer_weights(l, w):
    tile_heads = lambda g, n: jnp.tile(g, n)[None, :]
    return {
        "norm1_gain": w["norm1_gain"][l][None, :],
        "norm2_gain": w["norm2_gain"][l][None, :],
        "w_in": w["w_in"][l].astype(BF16),
        "q_gain": tile_heads(w["q_norm_gain"][l], NA_HEADS),
        "k_gain": tile_heads(w["k_norm_gain"][l], NA_HEADS),
        "w_gate": jax.scipy.linalg.block_diag(w["w_gate_f"][l], w["w_gate_b"][l]).astype(BF16),
        "b_gate": jnp.concatenate([w["b_gate_f"][l], w["b_gate_b"][l]])[None, :],
        "w_pool_bd": jax.scipy.linalg.block_diag(*[w["w_pool"][l, g] for g in range(len(POOL_WINDOWS))]).astype(BF16),
        "pool_scale": w["pool_scale"][l][None, :],
        "gla_gain": tile_heads(w["gla_norm_gain"][l], GLA_HEADS),
        "w_out": w["w_out"][l].astype(BF16),
        "w_ffn_in": w["w_ffn_in"][l].astype(BF16),
        "w_ffn_out": w["w_ffn_out"][l].astype(BF16),
    }


def _trunk_layer(x, mods_l, row_of_batch, lw, expand, s0_f, s0_b, latent, kv_out=None):
    is_ctx = latent is None
    B, L, _ = x.shape
    per_token = (lambda a: a.reshape(1, B * L, a.shape[-1])) if is_ctx else (lambda a: a)
    per_seq = lambda a: a.reshape(B, L, a.shape[-1])
    outs = _in_proj(per_token(x), mods_l, row_of_batch, lw, None if is_ctx else latent[0],
                    kv_out + (L,) if is_ctx else None)
    u, q, k, v, lq, lk, lv, lg, la = [per_seq(a) for a in outs[:9]]
    y_pool = _pool(u, lw)
    if is_ctx:
        y_na = _ctx_attn(q, k, v)
    else:
        y_na = _lat_attn(q, k, v, latent[1], latent[2], latent[3])
    o_f, o_b, s_f, s_b = _gla_scan(lq, lk, lv, la, s0_f, s0_b, expand)
    x = per_seq(_post(per_token(x), mods_l, row_of_batch,
                      *[per_token(a) for a in (y_pool, y_na, o_f, o_b, lg)], lw))
    return x, s_f, s_b, tuple(outs[9:])


def kernel(x_prompt, x_sample, c, cache_na_k, cache_na_v, state_gla_fwd, state_gla_bwd, c_ctx, w_ada, b_ada, norm1_gain, norm2_gain, w_in, w_pool, pool_scale, q_norm_gain, k_norm_gain, rel_bias, w_gate_f, b_gate_f, w_gate_b, b_gate_b, gla_norm_gain, w_out, w_ffn_in, w_ffn_out):
    weights = dict(norm1_gain=norm1_gain, norm2_gain=norm2_gain, w_in=w_in, w_pool=w_pool, pool_scale=pool_scale,
                   q_norm_gain=q_norm_gain, k_norm_gain=k_norm_gain, w_gate_f=w_gate_f, b_gate_f=b_gate_f,
                   w_gate_b=w_gate_b, b_gate_b=b_gate_b, gla_norm_gain=gla_norm_gain, w_out=w_out,
                   w_ffn_in=w_ffn_in, w_ffn_out=w_ffn_out)
    B_ctx, L_ctx, _ = x_prompt.shape
    B_lat, L_lat, _ = x_sample.shape
    assert 1 + B_lat <= COND_ROWS

    conds = jnp.zeros((COND_ROWS, D_MODEL), F32).at[0].set(c_ctx).at[1:1 + B_lat].set(c)
    mods = _ada(conds, w_ada, b_ada).reshape(DEPTH, COND_ROWS, N_MODS, D_MODEL)
    lws = [_layer_weights(l, weights) for l in range(DEPTH)]
    head_of_k = np.arange(GLA_KEY_WIDTH) // GLA_DK
    head_of_v = np.arange(GLA_WIDTH) // GLA_DV
    expand = jnp.asarray(head_of_k[:, None] == head_of_v[None, :], BF16)

    xp = x_prompt
    zero_state = jnp.zeros((B_ctx, GLA_WIDTH, GLA_KEY_WIDTH), F32)
    new_k = jnp.zeros((B_ctx, DEPTH, NA_HEADS, L_ctx, NA_HEAD_DIM), F32)
    new_v = jnp.zeros((B_ctx, DEPTH, NA_HEADS, L_ctx, NA_HEAD_DIM), F32)
    sfs, sbs = [], []
    for l in range(DEPTH):
        xp, s_f, s_b, (new_k, new_v) = _trunk_layer(xp, mods[l], lambda b: 0, lws[l], expand,
                                                     zero_state, zero_state, None, (new_k, new_v, l))
        sfs.append(_state_from_kernel(s_f))
        sbs.append(_state_from_kernel(s_b))

    xs = x_sample
    rope_tabs = _rope_tables(L_lat)
    from_heads = lambda a: a.transpose(0, 2, 1, 3).reshape(B_lat, a.shape[2], NA_WIDTH).astype(BF16)
    for l in range(DEPTH):
        latent = (rope_tabs, from_heads(cache_na_k[:, l]), from_heads(cache_na_v[:, l]), _bias_tiles(rel_bias[l]))
        xs, _, _, _ = _trunk_layer(xs, mods[l], lambda b: b + 1, lws[l], expand,
                                   _state_to_kernel(state_gla_fwd[:, l]), _state_to_kernel(state_gla_bwd[:, l]),
                                   latent)

    return (xp, xs, new_k, new_v, jnp.stack(sfs, axis=1), jnp.stack(sbs, axis=1))
```

```python
import functools
import math

import numpy as np
import jax
import jax.numpy as jnp
from jax import lax
from jax.experimental import pallas as pl
from jax.experimental.pallas import tpu as pltpu

F32 = jnp.float32
BF16 = jnp.bfloat16

D_MODEL = 1024
DEPTH = 2
GRID_W = 64
POOL_WIDTH = 256
POOL_GROUP_DIM = 64
POOL_WINDOWS = (2, 4, 8, 16)
POOL_HALO = max(POOL_WINDOWS) // 2
POOL_PAD = 8 * len(POOL_WINDOWS)
NA_HEADS = 8
NA_HEAD_DIM = 64
NA_WIDTH = NA_HEADS * NA_HEAD_DIM
NA_WIN_ROWS = 8
NA_WIN_COLS = 16
NA_ROWS_PER_STEP = 8
GLA_HEADS = 4
GLA_DV = 64
GLA_DK = 32
GLA_WIDTH = GLA_HEADS * GLA_DV
GLA_KEY_WIDTH = GLA_HEADS * GLA_DK
GLA_GATE_RANK = 16
GLA_GATE_TAU = 16.0
GLA_BLOCK = 16
ROPE_THETA = 10000.0
D_FF = 2816
NORM_EPS = 1e-6
N_MODS = 6
COND_ROWS = 16
MASK_VALUE = -1e30
LOG2_E = 1.4426950408889634

OFF_POOL = 0
OFF_NA_Q = OFF_POOL + POOL_WIDTH
OFF_NA_K = OFF_NA_Q + NA_WIDTH
OFF_NA_V = OFF_NA_K + NA_WIDTH
OFF_LA_Q = OFF_NA_V + NA_WIDTH
OFF_LA_K = OFF_LA_Q + GLA_KEY_WIDTH
OFF_LA_V = OFF_LA_K + GLA_KEY_WIDTH
OFF_LA_G = OFF_LA_V + GLA_WIDTH
OFF_LR = OFF_LA_G + GLA_WIDTH
IN_WIDTH = OFF_LR + 2 * GLA_GATE_RANK

LANES = 128
HALF_LANES = LANES // 2
assert NA_HEAD_DIM == GLA_DV == HALF_LANES
POOL_TILE = 512
ROW_TILE = 512
POST_SUB_TILE = 256
IN_ROW_TILE = 1024
IN_SUB_TILE = 256
GLA_TILE = 256
GLA_BATCH_PER_STEP = 8
FF_CHUNK = 256
ADA_COL_TILE = 512
VMEM_LIMIT = 56 * 1024 * 1024


def _dot(a, b):
    return jnp.dot(a, b, preferred_element_type=F32)


def _dot_nt(a, b):
    return lax.dot_general(a, b, (((1,), (1,)), ((), ())), preferred_element_type=F32)


def _dot_tn(a, b):
    return lax.dot_general(a, b, (((0,), (0,)), ((), ())), preferred_element_type=F32)


def _silu(x):
    return x * (1.0 / (1.0 + jnp.exp(-x)))


def _split3(x):
    hi = x.astype(BF16)
    r1 = x - hi.astype(F32)
    mid = r1.astype(BF16)
    lo = (r1 - mid.astype(F32)).astype(BF16)
    return hi, mid, lo


def _half_lane_rms(x, eps):
    lane = lax.broadcasted_iota(jnp.int32, (1, LANES), 1)
    low = lane < HALF_LANES
    cols = []
    for j in range(x.shape[-1] // LANES):
        blk = x[:, j * LANES:(j + 1) * LANES]
        sq = blk * blk
        s_lo = jnp.sum(jnp.where(low, sq, 0.0), axis=-1, keepdims=True)
        s_hi = jnp.sum(jnp.where(low, 0.0, sq), axis=-1, keepdims=True)
        r_lo = lax.rsqrt(s_lo * (1.0 / HALF_LANES) + eps)
        r_hi = lax.rsqrt(s_hi * (1.0 / HALF_LANES) + eps)
        cols.append(blk * jnp.where(low, r_lo, r_hi))
    return jnp.concatenate(cols, axis=-1)


def _row_rms(x, eps):
    return x * lax.rsqrt(jnp.mean(x * x, axis=-1, keepdims=True) + eps)


def _ada_kernel(c_ref, w_ref, b_ref, o_ref):
    s_hi, s_mid, _ = _split3(_silu(c_ref[...]))
    w = w_ref[0]
    w_hi, w_mid, _ = _split3(w)
    acc = _dot(s_hi, w_hi) + _dot(s_mid, w_hi) + _dot(s_hi, w_mid)
    o_ref[0] = acc + b_ref[0]


def _ada(conds, w_ada, b_ada):
    n_out = w_ada.shape[-1]
    return pl.pallas_call(
        _ada_kernel,
        grid=(DEPTH, n_out // ADA_COL_TILE),
        in_specs=[
            pl.BlockSpec((COND_ROWS, D_MODEL), lambda l, j: (0, 0)),
            pl.BlockSpec((1, D_MODEL, ADA_COL_TILE), lambda l, j: (l, 0, j)),
            pl.BlockSpec((1, 1, ADA_COL_TILE), lambda l, j: (l, 0, j)),
        ],
        out_specs=pl.BlockSpec((1, COND_ROWS, ADA_COL_TILE), lambda l, j: (l, 0, j)),
        out_shape=jax.ShapeDtypeStruct((DEPTH, COND_ROWS, n_out), F32),
        name="ada",
    )(conds, w_ada, b_ada.reshape(DEPTH, 1, n_out))


def _inproj_kernel(*refs, rope, kv_seq):
    x_ref, m_ref, g1_ref, w_ref, qg_ref, kg_ref, wg_ref, bg_ref = refs[:8]
    pos = 8
    if rope:
        cos_ref, sin_ref = refs[pos:pos + 2]
        pos += 2
    if kv_seq:
        pos += 2
    u_ref, q_ref, k_ref, v_ref, lq_ref, lk_ref, lv_ref, lg_ref, la_ref = refs[pos:pos + 9]
    pos += 9
    if kv_seq:
        kf_ref, vf_ref = refs[pos:pos + 2]

    shift1 = m_ref[0, 0:1, :]
    scale1 = m_ref[0, 1:2, :]
    tm = x_ref.shape[1]
    sub = min(IN_SUB_TILE, tm)

    def normed(i):
        rows = slice(i * sub, (i + 1) * sub)
        h = _row_rms(x_ref[0, rows, :], NORM_EPS) * g1_ref[...] * (1.0 + scale1) + shift1
        return h.astype(BF16)

    def project(i, hb):
        rows = slice(i * sub, (i + 1) * sub)

        def proj(off, width):
            return _dot(hb, w_ref[:, off:off + width])

        lr = proj(OFF_LR, 2 * GLA_GATE_RANK).astype(BF16)
        u_ref[0, rows, :] = proj(OFF_POOL, POOL_WIDTH)

        qn = _half_lane_rms(proj(OFF_NA_Q, NA_WIDTH), NORM_EPS) * qg_ref[...]
        q_ref[0, rows, :] = (qn * (NA_HEAD_DIM ** -0.5 * LOG2_E)).astype(BF16)
        kn = _half_lane_rms(proj(OFF_NA_K, NA_WIDTH), NORM_EPS) * kg_ref[...]
        k_ref[0, rows, :] = kn.astype(BF16)
        vn = proj(OFF_NA_V, NA_WIDTH)
        v_ref[0, rows, :] = vn.astype(BF16)
        if kv_seq:
            piece = min(sub, kv_seq)
            for j in range(sub // piece):
                first = i * sub + j * piece
                seq, off = (first // kv_seq, first % kv_seq) if tm >= kv_seq else (0, first)
                for hh in range(NA_HEADS):
                    cols = slice(hh * NA_HEAD_DIM, (hh + 1) * NA_HEAD_DIM)
                    kf_ref[seq, 0, hh, off:off + piece, :] = kn[j * piece:(j + 1) * piece, cols]
                    vf_ref[seq, 0, hh, off:off + piece, :] = vn[j * piece:(j + 1) * piece, cols]

        z = _dot(lr, wg_ref[...]) + bg_ref[...]
        log_sig = jnp.minimum(z, 0.0) - jnp.log1p(jnp.exp(-jnp.abs(z)))
        la_ref[0, rows, :] = log_sig * (LOG2_E / GLA_GATE_TAU)

        lqk = proj(OFF_LA_Q, 2 * GLA_KEY_WIDTH)
        lq = lqk[:, :GLA_KEY_WIDTH] * (GLA_DK ** -0.5)
        lk = lqk[:, GLA_KEY_WIDTH:]
        if rope:
            lane = lax.broadcasted_iota(jnp.int32, (1, LANES), 1)
            first = (lane % 16) < 8
            cos = cos_ref[rows, :]
            sin = sin_ref[rows, :]

            def rot(t):
                partner = jnp.where(first, pltpu.roll(t, LANES - 8, axis=1), pltpu.roll(t, 8, axis=1))
                return t * cos + partner * sin

            lq = rot(lq)
            lk = rot(lk)
        lq_ref[0, rows, :] = lq
        lk_ref[0, rows, :] = lk
        lv_ref[0, rows, :] = proj(OFF_LA_V, GLA_WIDTH)
        lg_ref[0, rows, :] = proj(OFF_LA_G, GLA_WIDTH)

    hb_next = normed(0)
    for i in range(tm // sub):
        hb = hb_next
        hb_next = normed(i + 1) if (i + 1) * sub < tm else None
        project(i, hb)


def _in_proj(x, mods_l, row_of_batch, lw, rope_tabs, kv_out):
    B, L, _ = x.shape
    tm = min(IN_ROW_TILE, L)
    rope = rope_tabs is not None
    const = lambda b, i: (0, 0)
    tile = lambda b, i: (b, i, 0)

    in_specs = [
        pl.BlockSpec((1, tm, D_MODEL), tile),
        pl.BlockSpec((1, N_MODS, D_MODEL), lambda b, i: (row_of_batch(b), 0, 0)),
        pl.BlockSpec((1, D_MODEL), const),
        pl.BlockSpec((D_MODEL, IN_WIDTH), const),
        pl.BlockSpec((1, NA_WIDTH), const),
        pl.BlockSpec((1, NA_WIDTH), const),
        pl.BlockSpec((2 * GLA_GATE_RANK, 2 * GLA_KEY_WIDTH), const),
        pl.BlockSpec((1, 2 * GLA_KEY_WIDTH), const),
    ]
    args = [x, mods_l, lw["norm1_gain"], lw["w_in"], lw["q_gain"], lw["k_gain"], lw["w_gate"], lw["b_gate"]]
    if rope:
        in_specs += [pl.BlockSpec((tm, GLA_KEY_WIDTH), lambda b, i: (i, 0))] * 2
        args += list(rope_tabs)

    widths = [(POOL_WIDTH, F32), (NA_WIDTH, BF16), (NA_WIDTH, BF16), (NA_WIDTH, BF16),
              (GLA_KEY_WIDTH, F32), (GLA_KEY_WIDTH, F32), (GLA_WIDTH, F32), (GLA_WIDTH, F32),
              (2 * GLA_KEY_WIDTH, F32)]
    out_specs = [pl.BlockSpec((1, tm, w), tile) for w, _ in widths]
    out_shape = [jax.ShapeDtypeStruct((B, L, w), dt) for w, dt in widths]
    aliases = {}
    kv_seq = None
    if kv_out is not None:
        k_buf, v_buf, layer, kv_seq = kv_out
        assert B == 1 and (tm % kv_seq == 0 or kv_seq % tm == 0)
        if tm >= kv_seq:
            kv_spec = pl.BlockSpec((tm // kv_seq, 1, NA_HEADS, kv_seq, NA_HEAD_DIM), lambda b, i: (i, layer, 0, 0, 0))
        else:
            per_seq = kv_seq // tm
            kv_spec = pl.BlockSpec((1, 1, NA_HEADS, tm, NA_HEAD_DIM),
                                   lambda b, i: (i // per_seq, layer, 0, i % per_seq, 0))
        for buf in (k_buf, v_buf):
            aliases[len(args)] = len(out_specs)
            in_specs.append(pl.BlockSpec(memory_space=pl.ANY))
            args.append(buf)
            out_specs.append(kv_spec)
            out_shape.append(jax.ShapeDtypeStruct(buf.shape, buf.dtype))

    return pl.pallas_call(
        functools.partial(_inproj_kernel, rope=rope, kv_seq=kv_seq),
        grid=(B, L // tm),
        in_specs=in_specs,
        out_specs=out_specs,
        out_shape=out_shape,
        input_output_aliases=aliases,
        compiler_params=pltpu.CompilerParams(vmem_limit_bytes=VMEM_LIMIT),
        name="in_proj",
    )(*args)


def _pool_kernel(u_ref, w_ref, sc_ref, o_ref, pad_ref, s2_ref, s4_ref, s8_ref, *, L, tp):
    P = POOL_PAD
    zeros = jnp.zeros((P, POOL_WIDTH), F32)
    pad_ref[0:P, :] = zeros
    pad_ref[P + L:P + L + P, :] = zeros
    pad_ref[P:P + L, :] = u_ref[0]
    narrow = slice(0, LANES)
    wide = slice(LANES, 2 * LANES)
    lane = lax.broadcasted_iota(jnp.int32, (1, LANES), 1)
    low = lane < POOL_GROUP_DIM

    def chunks(level):
        lo, hi = 8 * level, L + 2 * P - 8 * level
        return [(a, min(a + tp, hi)) for a in range(lo, hi, tp)]

    for a, b in chunks(1):
        s2_ref[a:b, :] = pad_ref[a - 1:b - 1, wide] + pad_ref[a:b, wide]
    for a, b in chunks(2):
        s4_ref[a:b, :] = s2_ref[a - 1:b - 1, :] + s2_ref[a + 1:b + 1, :]
    for a, b in chunks(3):
        s8_ref[a:b, :] = s4_ref[a - 2:b - 2, :] + s4_ref[a + 2:b + 2, :]

    for base in range(0, L, tp):
        p0 = P + base

        def rows(ref, off, cols):
            return ref[p0 + off:p0 + off + tp, cols]

        u_a = rows(pad_ref, 0, narrow)
        w2 = rows(pad_ref, -1, narrow) + u_a
        w4 = w2 + rows(pad_ref, -2, narrow) + rows(pad_ref, 1, narrow)
        u_b = rows(pad_ref, 0, wide)
        w8 = rows(s8_ref, 0, slice(None))
        w16 = rows(s8_ref, -4, slice(None)) + rows(s8_ref, 4, slice(None))

        if base < POOL_HALO or base + tp > L - POOL_HALO:
            t = base + lax.broadcasted_iota(jnp.int32, (tp, LANES), 0)

            def mean(total, win):
                count = jnp.minimum(t + win // 2, L) - jnp.maximum(t - win // 2, 0)
                return total / count.astype(F32)
        else:
            def mean(total, win):
                return total * (1.0 / win)

        mean_a = jnp.where(low, mean(w2, 2), mean(w4, 4))
        mean_b = jnp.where(low, mean(w8, 8), mean(w16, 16))
        d = jnp.concatenate([mean_a - u_a, mean_b - u_b], axis=-1).astype(BF16)
        y = _dot(d, w_ref[...]) * sc_ref[...]
        o_ref[0, base:base + tp, :] = y.astype(BF16)


def _pool(u, lw):
    B, L, _ = u.shape
    tp = min(POOL_TILE, L)
    return pl.pallas_call(
        functools.partial(_pool_kernel, L=L, tp=tp),
        grid=(B,),
        in_specs=[
            pl.BlockSpec((1, L, POOL_WIDTH), lambda b: (b, 0, 0)),
            pl.BlockSpec((POOL_WIDTH, POOL_WIDTH), lambda b: (0, 0)),
            pl.BlockSpec((1, POOL_WIDTH), lambda b: (0, 0)),
        ],
        out_specs=pl.BlockSpec((1, L, POOL_WIDTH), lambda b: (b, 0, 0)),
        out_shape=jax.ShapeDtypeStruct((B, L, POOL_WIDTH), BF16),
        scratch_shapes=[pltpu.VMEM((L + 2 * POOL_PAD, POOL_WIDTH), F32)]
                       + [pltpu.VMEM((L + 2 * POOL_PAD, LANES), F32)] * 3,
        compiler_params=pltpu.CompilerParams(vmem_limit_bytes=VMEM_LIMIT),
        name="pool",
    )(u, lw["w_pool_bd"], lw["pool_scale"])


def _ctx_attn_kernel(q_ref, k_ref, v_ref, o_ref):
    L = q_ref.shape[1]
    lane = lax.broadcasted_iota(jnp.int32, (1, LANES), 1)
    even = lane < NA_HEAD_DIM
    zero = jnp.zeros((), BF16)

    def scores(j):
        cols = slice(j * LANES, (j + 1) * LANES)
        qp = q_ref[0, :, cols]
        q2 = jnp.concatenate([jnp.where(even, qp, zero), jnp.where(even, zero, qp)], axis=0)
        return _dot_nt(q2, k_ref[0, :, cols])

    def finish(j, s):
        cols = slice(j * LANES, (j + 1) * LANES)
        p = jnp.exp2(s - jnp.max(s, axis=-1, keepdims=True))
        denom = jnp.sum(p, axis=-1, keepdims=True)
        o2 = _dot(p.astype(BF16), v_ref[0, :, cols]) / denom
        o_ref[0, :, cols] = jnp.where(even, o2[:L], o2[L:]).astype(BF16)

    n_pairs = NA_HEADS // 2
    s_next = scores(0)
    for j in range(n_pairs):
        s_cur = s_next
        s_next = scores(j + 1) if j + 1 < n_pairs else None
        finish(j, s_cur)


def _ctx_attn(q, k, v):
    B, L, _ = q.shape
    spec = pl.BlockSpec((1, L, NA_WIDTH), lambda b: (b, 0, 0))
    return pl.pallas_call(
        _ctx_attn_kernel,
        grid=(B,),
        in_specs=[spec, spec, spec],
        out_specs=spec,
        out_shape=jax.ShapeDtypeStruct((B, L, NA_WIDTH), BF16),
        name="ctx_attn",
    )(q, k, v)


def _band_start(r, n_rows):
    return jnp.clip(r - NA_WIN_ROWS // 2, 0, n_rows - NA_WIN_ROWS)


def _lat_attn_kernel(q_ref, k_ref, v_ref, kc_ref, vc_ref, bias_ref, o_ref, *, n_rows):
    band = NA_WIN_ROWS * GRID_W
    lane = lax.broadcasted_iota(jnp.int32, (1, LANES), 1)
    even = lane < NA_HEAD_DIM
    zero = jnp.zeros((), BF16)

    def band_of(rr):
        r = pl.program_id(1) * NA_ROWS_PER_STEP + rr
        first = _band_start(r, n_rows)
        return pl.multiple_of(first * GRID_W, GRID_W), NA_WIN_ROWS - 1 - (r - first)

    def scores(rr, j):
        start, tile0 = band_of(rr)
        cols = slice(j * LANES, (j + 1) * LANES)
        qp = q_ref[0, rr * GRID_W:(rr + 1) * GRID_W, cols]
        q2 = jnp.concatenate([jnp.where(even, qp, zero), jnp.where(even, zero, qp)], axis=0)
        bias = jnp.concatenate(
            [jnp.concatenate([bias_ref[2 * j + hh, tile0 + 2 * ii] for ii in range(NA_WIN_ROWS // 2)], axis=-1)
             for hh in range(2)], axis=0)
        return _dot_nt(q2, k_ref[0, pl.ds(start, band), cols]) + bias, _dot_nt(q2, kc_ref[0, :, cols])

    def softmax(s_loc, s_ctx):
        m = jnp.maximum(jnp.max(s_loc, axis=-1, keepdims=True), jnp.max(s_ctx, axis=-1, keepdims=True))
        p_loc = jnp.exp2(s_loc - m)
        p_ctx = jnp.exp2(s_ctx - m)
        denom = jnp.sum(p_loc, axis=-1, keepdims=True) + jnp.sum(p_ctx, axis=-1, keepdims=True)
        return p_loc.astype(BF16), p_ctx.astype(BF16), denom

    def values(rr, j, p_loc, p_ctx, denom):
        start, _ = band_of(rr)
        cols = slice(j * LANES, (j + 1) * LANES)
        o2 = (_dot(p_loc, v_ref[0, pl.ds(start, band), cols]) + _dot(p_ctx, vc_ref[0, :, cols])) / denom
        o_ref[0, rr * GRID_W:(rr + 1) * GRID_W, cols] = jnp.where(even, o2[:GRID_W], o2[GRID_W:]).astype(BF16)

    chains = [(rr, j) for rr in range(NA_ROWS_PER_STEP) for j in range(NA_HEADS // 2)]
    s_next = scores(*chains[0])
    p_prev = None
    for i, chain in enumerate(chains):
        s_cur = s_next
        s_next = scores(*chains[i + 1]) if i + 1 < len(chains) else None
        p_cur = softmax(*s_cur)
        if p_prev is not None:
            values(*chains[i - 1], *p_prev)
        p_prev = p_cur
    values(*chains[-1], *p_prev)


def _lat_attn(q, k, v, kc, vc, bias_tiles):
    B, L, _ = q.shape
    n_rows = L // GRID_W
    assert n_rows >= NA_WIN_ROWS and n_rows % NA_ROWS_PER_STEP == 0
    past = kc.shape[1]
    whole = pl.BlockSpec((1, L, NA_WIDTH), lambda b, r: (b, 0, 0))
    rows = pl.BlockSpec((1, NA_ROWS_PER_STEP * GRID_W, NA_WIDTH), lambda b, r: (b, r, 0))
    ctx = pl.BlockSpec((1, past, NA_WIDTH), lambda b, r: (b, 0, 0))
    bias = pl.BlockSpec(bias_tiles.shape, lambda b, r: (0, 0, 0, 0))
    return pl.pallas_call(
        functools.partial(_lat_attn_kernel, n_rows=n_rows),
        grid=(B, n_rows // NA_ROWS_PER_STEP),
        in_specs=[rows, whole, whole, ctx, ctx, bias],
        out_specs=rows,
        out_shape=jax.ShapeDtypeStruct((B, L, NA_WIDTH), BF16),
        compiler_params=pltpu.CompilerParams(vmem_limit_bytes=VMEM_LIMIT),
        name="lat_attn",
    )(q, k, v, kc, vc, bias_tiles)


def _bias_tile_constants():
    n_dc = 2 * NA_WIN_COLS
    w = np.arange(GRID_W)[:, None]
    cc = np.arange(2 * GRID_W)[None, :]
    c = cc % GRID_W
    cs = np.clip(w - NA_WIN_COLS // 2, 0, GRID_W - NA_WIN_COLS)
    valid = (c >= cs) & (c < cs + NA_WIN_COLS)
    k_idx = (cc // GRID_W) * n_dc + (c - w + NA_WIN_COLS - 1)
    onehot = (np.arange(2 * n_dc)[:, None, None] == k_idx[None]) & valid[None]
    onehot = onehot.reshape(2 * n_dc, GRID_W * 2 * GRID_W)
    mask = np.where(valid, 0.0, MASK_VALUE).reshape(1, GRID_W * 2 * GRID_W)
    return jnp.asarray(onehot, BF16), jnp.asarray(mask, F32)


def _bias_tiles_kernel(rb_ref, oh_ref, mask_ref, o_ref):
    hi, mid, lo = _split3(rb_ref[...])
    oh = oh_ref[...]
    o_ref[...] = (_dot(hi, oh) + _dot(mid, oh) + _dot(lo, oh)) * LOG2_E + mask_ref[...]


def _bias_tiles(rel_bias_l):
    n_dr = 2 * NA_WIN_ROWS - 1
    padded = jnp.pad(rel_bias_l, ((0, 0), (0, 0), (0, 1)))
    pairs = jnp.concatenate([padded[:, :-1], padded[:, 1:]], axis=-1)
    pairs = pairs.reshape(NA_HEADS * (n_dr - 1), 4 * NA_WIN_COLS)
    onehot, mask = _bias_tile_constants()
    full = lambda shape: pl.BlockSpec(shape, lambda: (0,) * len(shape))
    out = pl.pallas_call(
        _bias_tiles_kernel,
        in_specs=[full(pairs.shape), full(onehot.shape), full(mask.shape)],
        out_specs=full((pairs.shape[0], onehot.shape[1])),
        out_shape=jax.ShapeDtypeStruct((pairs.shape[0], onehot.shape[1]), F32),
        name="bias_tiles",
    )(pairs, onehot, mask)
    return out.reshape(NA_HEADS, n_dr - 1, GRID_W, 2 * GRID_W)


def _gla_block(q_ref, k_ref, v_ref, b_ref, st_ref, o_ref, expand, head_mask, bi, r0, reverse):
    half = GLA_BLOCK // 2
    rows = pl.ds(r0, GLA_BLOCK)
    qb = q_ref[bi, rows, :]
    kb = k_ref[bi, rows, :]
    vb = v_ref[bi, rows, :]
    bb = b_ref[bi, rows, :]
    end_row = 0 if reverse else GLA_BLOCK - 1
    b_end = bb[end_row:end_row + 1, :]
    st = st_ref[bi]

    o_inter = _dot_nt((qb * jnp.exp2(bb)).astype(BF16), st.astype(BF16))

    t_in = lax.broadcasted_iota(jnp.int32, (half, 1), 0)
    q_half = (qb[:half], qb[half:])
    b_half = (bb[:half], bb[half:])
    pieces, owners = [], []
    for s in range(GLA_BLOCK):
        hs = s // half
        k_s = kb[s:s + 1, :]
        b_s = bb[s:s + 1, :]
        for ht in range(2):
            if (ht > hs) if reverse else (ht < hs):
                continue
            decay = jnp.exp2(b_half[ht] - b_s)
            if ht == hs:
                seen = (t_in <= s - hs * half) if reverse else (t_in >= s - hs * half)
                decay = jnp.where(seen, decay, 0.0)
            pieces.append(q_half[ht] * k_s * decay)
            owners.append((ht, s))
    a_exp = _dot(jnp.concatenate(pieces, axis=0).astype(BF16), expand)
    o_intra = [jnp.zeros((half, GLA_WIDTH), F32), jnp.zeros((half, GLA_WIDTH), F32)]
    for i, (ht, s) in enumerate(owners):
        o_intra[ht] = o_intra[ht] + a_exp[i * half:(i + 1) * half, :] * vb[s:s + 1, :]
    o_ref[bi, rows, :] = o_inter + jnp.concatenate(o_intra, axis=0)

    k_hat = (kb * jnp.exp2(b_end - bb)).astype(BF16)
    upd = _dot_tn(vb.astype(BF16), k_hat)
    st_ref[bi] = st * jnp.exp2(b_end) + upd * head_mask


def _gla_kernel(qf_ref, kf_ref, vf_ref, laf_ref, qb_ref, kb_ref, vb_ref, lab_ref, s0f_ref, s0b_ref,
                trif_ref, trib_ref, e_ref, of_ref, ob_ref, sff_ref, sfb_ref,
                stf_ref, stb_ref, bf_ref, bb_ref, *, tm, nb):
    n = pl.program_id(1)
    n_blocks = tm // GLA_BLOCK

    @pl.when(n == 0)
    def _():
        stf_ref[...] = s0f_ref[...]
        stb_ref[...] = s0b_ref[...]

    for la_ref, tri_ref, b_ref in ((laf_ref, trif_ref, bf_ref), (lab_ref, trib_ref, bb_ref)):
        tri = tri_ref[...]
        for bi in range(nb):
            g_hi, g_mid, _ = _split3(la_ref[bi])
            sums = _dot(tri, jnp.concatenate([g_hi, g_mid], axis=-1))
            b_ref[bi] = sums[:, :GLA_KEY_WIDTH] + sums[:, GLA_KEY_WIDTH:]

    expand = e_ref[...]
    head_v = lax.broadcasted_iota(jnp.int32, (GLA_WIDTH, GLA_KEY_WIDTH), 0) // GLA_DV
    head_k = lax.broadcasted_iota(jnp.int32, (GLA_WIDTH, GLA_KEY_WIDTH), 1) // GLA_DK
    head_mask = jnp.where(head_v == head_k, 1.0, 0.0)

    def block(j, carry):
        r_f = pl.multiple_of(j * GLA_BLOCK, GLA_BLOCK)
        r_b = pl.multiple_of((n_blocks - 1 - j) * GLA_BLOCK, GLA_BLOCK)
        for bi in range(nb):
            _gla_block(qf_ref, kf_ref, vf_ref, bf_ref, stf_ref, of_ref, expand, head_mask, bi, r_f, reverse=False)
            _gla_block(qb_ref, kb_ref, vb_ref, bb_ref, stb_ref, ob_ref, expand, head_mask, bi, r_b, reverse=True)
        return carry

    lax.fori_loop(0, n_blocks, block, 0)

    @pl.when(n == pl.num_programs(1) - 1)
    def _():
        sff_ref[...] = stf_ref[...]
        sfb_ref[...] = stb_ref[...]


def _block_tri(tm, reverse):
    t = np.arange(tm)[:, None]
    s = np.arange(tm)[None, :]
    same = (t // GLA_BLOCK) == (s // GLA_BLOCK)
    return jnp.asarray(same & ((s >= t) if reverse else (s <= t)), BF16)


def _gla_scan(q, k, v, la, s0f_t, s0b_t, expand):
    B, L, _ = q.shape
    tm = min(GLA_TILE, L)
    n_tiles = L // tm
    nb = math.gcd(B, GLA_BATCH_PER_STEP)
    fwd = lambda b, n: (b, n, 0)
    bwd = lambda b, n: (b, n_tiles - 1 - n, 0)
    state = pl.BlockSpec((nb, GLA_WIDTH, GLA_KEY_WIDTH), lambda b, n: (b, 0, 0))
    const = lambda shape: pl.BlockSpec(shape, lambda b, n: (0, 0))

    def operands(tmap, la_col):
        return [pl.BlockSpec((nb, tm, GLA_KEY_WIDTH), tmap), pl.BlockSpec((nb, tm, GLA_KEY_WIDTH), tmap),
                pl.BlockSpec((nb, tm, GLA_WIDTH), tmap),
                pl.BlockSpec((nb, tm, GLA_KEY_WIDTH), lambda b, n: tmap(b, n)[:2] + (la_col,))]

    return pl.pallas_call(
        functools.partial(_gla_kernel, tm=tm, nb=nb),
        grid=(B // nb, n_tiles),
        in_specs=operands(fwd, 0) + operands(bwd, 1) + [state, state, const((tm, tm)), const((tm, tm)),
                                                        const((GLA_KEY_WIDTH, GLA_WIDTH))],
        out_specs=[pl.BlockSpec((nb, tm, GLA_WIDTH), fwd), pl.BlockSpec((nb, tm, GLA_WIDTH), bwd), state, state],
        out_shape=[jax.ShapeDtypeStruct((B, L, GLA_WIDTH), F32), jax.ShapeDtypeStruct((B, L, GLA_WIDTH), F32),
                   jax.ShapeDtypeStruct((B, GLA_WIDTH, GLA_KEY_WIDTH), F32),
                   jax.ShapeDtypeStruct((B, GLA_WIDTH, GLA_KEY_WIDTH), F32)],
        scratch_shapes=[pltpu.VMEM((nb, GLA_WIDTH, GLA_KEY_WIDTH), F32), pltpu.VMEM((nb, GLA_WIDTH, GLA_KEY_WIDTH), F32),
                        pltpu.VMEM((nb, tm, GLA_KEY_WIDTH), F32), pltpu.VMEM((nb, tm, GLA_KEY_WIDTH), F32)],
        compiler_params=pltpu.CompilerParams(vmem_limit_bytes=VMEM_LIMIT),
        name="gla",
    )(q, k, v, la, q, k, v, la, s0f_t, s0b_t, _block_tri(tm, False), _block_tri(tm, True), expand)


def _state_to_kernel(s):
    B = s.shape[0]
    same_head = np.eye(GLA_HEADS, dtype=bool)[None, :, None, :, None]
    st = jnp.where(same_head, s.transpose(0, 1, 3, 2)[:, :, :, None, :], 0.0)
    return st.reshape(B, GLA_WIDTH, GLA_KEY_WIDTH)


def _state_from_kernel(st):
    B = st.shape[0]
    blocks = st.reshape(B, GLA_HEADS, GLA_DV, GLA_HEADS, GLA_DK)
    diag = jnp.stack([blocks[:, h, :, h, :] for h in range(GLA_HEADS)], axis=1)
    return diag.transpose(0, 1, 3, 2)


def _post_kernel(x_ref, m_ref, yp_ref, yn_ref, of_ref, ob_ref, lg_ref, gg_ref, g2_ref,
                 wo_ref, wi_ref, wf_ref, o_ref, acc_ref, x1_ref, h2_ref):
    gate1 = m_ref[0, 2:3, :]
    shift2 = m_ref[0, 3:4, :]
    scale2 = m_ref[0, 4:5, :]
    gate2 = m_ref[0, 5:6, :]

    tm = x_ref.shape[1]
    sub = min(POST_SUB_TILE, tm)
    for r0 in range(0, tm, sub):
        rows = slice(r0, r0 + sub)
        o_la = _half_lane_rms(of_ref[0, rows, :] + ob_ref[0, rows, :], NORM_EPS) * gg_ref[...]
        y_la = (o_la * _silu(lg_ref[0, rows, :])).astype(BF16)
        mixed = (_dot(yp_ref[0, rows, :], wo_ref[0:POOL_WIDTH, :])
                 + _dot(yn_ref[0, rows, :], wo_ref[POOL_WIDTH:POOL_WIDTH + NA_WIDTH, :])
                 + _dot(y_la, wo_ref[POOL_WIDTH + NA_WIDTH:D_MODEL, :]))
        x1 = x_ref[0, rows, :] + gate1 * mixed
        x1_ref[rows, :] = x1
        h2_ref[rows, :] = (_row_rms(x1, NORM_EPS) * g2_ref[...] * (1.0 + scale2) + shift2).astype(BF16)

    for c0 in range(0, D_FF, FF_CHUNK):
        hg = _dot(h2_ref[...], wi_ref[:, c0:c0 + FF_CHUNK])
        hu = _dot(h2_ref[...], wi_ref[:, D_FF + c0:D_FF + c0 + FF_CHUNK])
        acc_ref[:, c0:c0 + FF_CHUNK] = (_silu(hg) * hu).astype(BF16)
    o_ref[0] = x1_ref[...] + gate2 * _dot(acc_ref[...], wf_ref[...])


def _post(x, mods_l, row_of_batch, y_pool, y_na, o_f, o_b, lg, lw):
    B, L, _ = x.shape
    tm = min(ROW_TILE, L)
    const = lambda b, i: (0, 0)
    tile = lambda b, i: (b, i, 0)

    def resident(shape):
        return pl.BlockSpec(shape, const, pipeline_mode=pl.Buffered(1))

    return pl.pallas_call(
        _post_kernel,
        grid=(B, L // tm),
        in_specs=[
            pl.BlockSpec((1, tm, D_MODEL), tile),
            pl.BlockSpec((1, N_MODS, D_MODEL), lambda b, i: (row_of_batch(b), 0, 0)),
            pl.BlockSpec((1, tm, POOL_WIDTH), tile),
            pl.BlockSpec((1, tm, NA_WIDTH), tile),
            pl.BlockSpec((1, tm, GLA_WIDTH), tile),
            pl.BlockSpec((1, tm, GLA_WIDTH), tile),
            pl.BlockSpec((1, tm, GLA_WIDTH), tile),
            pl.BlockSpec((1, GLA_WIDTH), const),
            pl.BlockSpec((1, D_MODEL), const),
            resident((D_MODEL, D_MODEL)),
            resident((D_MODEL, 2 * D_FF)),
            resident((D_FF, D_MODEL)),
        ],
        out_specs=pl.BlockSpec((1, tm, D_MODEL), tile),
        out_shape=jax.ShapeDtypeStruct((B, L, D_MODEL), F32),
        scratch_shapes=[pltpu.VMEM((tm, D_FF), BF16), pltpu.VMEM((tm, D_MODEL), F32), pltpu.VMEM((tm, D_MODEL), BF16)],
        compiler_params=pltpu.CompilerParams(vmem_limit_bytes=VMEM_LIMIT),
        name="post",
    )(x, mods_l, y_pool, y_na, o_f, o_b, lg, lw["gla_gain"], lw["norm2_gain"],
      lw["w_out"], lw["w_ffn_in"], lw["w_ffn_out"])


def _rope_tables(L):
    t = jnp.arange(L)
    row = (t // GRID_W).astype(F32)
    col = (t % GRID_W).astype(F32)
    half = GLA_DK // 2
    inv_freq = ROPE_THETA ** (-jnp.arange(0, half, 2, dtype=F32) / half)
    ang_r = row[:, None] * inv_freq
    ang_c = col[:, None] * inv_freq
    sign = jnp.concatenate([-jnp.ones((half // 2,), F32), jnp.ones((half // 2,), F32)])

    def lanes(fn, signed):
        per_axis = []
        for ang in (ang_r, ang_c):
            v = jnp.concatenate([fn(ang), fn(ang)], axis=-1)
            per_axis.append(v * sign if signed else v)
        return jnp.tile(jnp.concatenate(per_axis, axis=-1), (1, GLA_HEADS))

    return lanes(jnp.cos, False), lanes(jnp.sin, True)


def _layer_weights(l, w):
    tile_heads = lambda g, n: jnp.tile(g, n)[None, :]
    return {
        "norm1_gain": w["norm1_gain"][l][None, :],
        "norm2_gain": w["norm2_gain"][l][None, :],
        "w_in": w["w_in"][l].astype(BF16),
        "q_gain": tile_heads(w["q_norm_gain"][l], NA_HEADS),
        "k_gain": tile_heads(w["k_norm_gain"][l], NA_HEADS),
        "w_gate": jax.scipy.linalg.block_diag(w["w_gate_f"][l], w["w_gate_b"][l]).astype(BF16),
        "b_gate": jnp.concatenate([w["b_gate_f"][l], w["b_gate_b"][l]])[None, :],
        "w_pool_bd": jax.scipy.linalg.block_diag(*[w["w_pool"][l, g] for g in range(len(POOL_WINDOWS))]).astype(BF16),
        "pool_scale": w["pool_scale"][l][None, :],
        "gla_gain": tile_heads(w["gla_norm_gain"][l], GLA_HEADS),
        "w_out": w["w_out"][l].astype(BF16),
        "w_ffn_in": w["w_ffn_in"][l].astype(BF16),
        "w_ffn_out": w["w_ffn_out"][l].astype(BF16),
    }


def _trunk_layer(x, mods_l, row_of_batch, lw, expand, s0_f, s0_b, latent, kv_out=None):
    is_ctx = latent is None
    B, L, _ = x.shape
    per_token = (lambda a: a.reshape(1, B * L, a.shape[-1])) if is_ctx else (lambda a: a)
    per_seq = lambda a: a.reshape(B, L, a.shape[-1])
    outs = _in_proj(per_token(x), mods_l, row_of_batch, lw, None if is_ctx else latent[0],
                    kv_out + (L,) if is_ctx else None)
    u, q, k, v, lq, lk, lv, lg, la = [per_seq(a) for a in outs[:9]]
    y_pool = _pool(u, lw)
    if is_ctx:
        y_na = _ctx_attn(q, k, v)
    else:
        y_na = _lat_attn(q, k, v, latent[1], latent[2], latent[3])
    o_f, o_b, s_f, s_b = _gla_scan(lq, lk, lv, la, s0_f, s0_b, expand)
    x = per_seq(_post(per_token(x), mods_l, row_of_batch,
                      *[per_token(a) for a in (y_pool, y_na, o_f, o_b, lg)], lw))
    return x, s_f, s_b, tuple(outs[9:])


def kernel(x_prompt, x_sample, c, cache_na_k, cache_na_v, state_gla_fwd, state_gla_bwd, c_ctx, w_ada, b_ada, norm1_gain, norm2_gain, w_in, w_pool, pool_scale, q_norm_gain, k_norm_gain, rel_bias, w_gate_f, b_gate_f, w_gate_b, b_gate_b, gla_norm_gain, w_out, w_ffn_in, w_ffn_out):
    weights = dict(norm1_gain=norm1_gain, norm2_gain=norm2_gain, w_in=w_in, w_pool=w_pool, pool_scale=pool_scale,
                   q_norm_gain=q_norm_gain, k_norm_gain=k_norm_gain, w_gate_f=w_gate_f, b_gate_f=b_gate_f,
                   w_gate_b=w_gate_b, b_gate_b=b_gate_b, gla_norm_gain=gla_norm_gain, w_out=w_out,
                   w_ffn_in=w_ffn_in, w_ffn_out=w_ffn_out)
    B_ctx, L_ctx, _ = x_prompt.shape
    B_lat, L_lat, _ = x_sample.shape
    assert 1 + B_lat <= COND_ROWS

    conds = jnp.zeros((COND_ROWS, D_MODEL), F32).at[0].set(c_ctx).at[1:1 + B_lat].set(c)
    mods = _ada(conds, w_ada, b_ada).reshape(DEPTH, COND_ROWS, N_MODS, D_MODEL)
    lws = [_layer_weights(l, weights) for l in range(DEPTH)]
    head_of_k = np.arange(GLA_KEY_WIDTH) // GLA_DK
    head_of_v = np.arange(GLA_WIDTH) // GLA_DV
    expand = jnp.asarray(head_of_k[:, None] == head_of_v[None, :], BF16)

    xp = x_prompt
    zero_state = jnp.zeros((B_ctx, GLA_WIDTH, GLA_KEY_WIDTH), F32)
    new_k = jnp.zeros((B_ctx, DEPTH, NA_HEADS, L_ctx, NA_HEAD_DIM), F32)
    new_v = jnp.zeros((B_ctx, DEPTH, NA_HEADS, L_ctx, NA_HEAD_DIM), F32)
    sfs, sbs = [], []
    for l in range(DEPTH):
        xp, s_f, s_b, (new_k, new_v) = _trunk_layer(xp, mods[l], lambda b: 0, lws[l], expand,
                                                     zero_state, zero_state, None, (new_k, new_v, l))
        sfs.append(_state_from_kernel(s_f))
        sbs.append(_state_from_kernel(s_b))

    xs = x_sample
    rope_tabs = _rope_tables(L_lat)
    from_heads = lambda a: a.transpose(0, 2, 1, 3).reshape(B_lat, a.shape[2], NA_WIDTH).astype(BF16)
    for l in range(DEPTH):
        latent = (rope_tabs, from_heads(cache_na_k[:, l]), from_heads(cache_na_v[:, l]), _bias_tiles(rel_bias[l]))
        xs, _, _, _ = _trunk_layer(xs, mods[l], lambda b: b + 1, lws[l], expand,
                                   _state_to_kernel(state_gla_fwd[:, l]), _state_to_kernel(state_gla_bwd[:, l]),
                                   latent)

    return (xp, xs, new_k, new_v, jnp.stack(sfs, axis=1), jnp.stack(sbs, axis=1))
```

```python
import functools
import math

import numpy as np
import jax
import jax.numpy as jnp
from jax import lax
from jax.experimental import pallas as pl
from jax.experimental.pallas import tpu as pltpu

F32 = jnp.float32
BF16 = jnp.bfloat16

D_MODEL = 1024
DEPTH = 2
GRID_W = 64
POOL_WIDTH = 256
POOL_GROUP_DIM = 64
POOL_WINDOWS = (2, 4, 8, 16)
POOL_HALO = max(POOL_WINDOWS) // 2
POOL_PAD = 8 * len(POOL_WINDOWS)
NA_HEADS = 8
NA_HEAD_DIM = 64
NA_WIDTH = NA_HEADS * NA_HEAD_DIM
NA_WIN_ROWS = 8
NA_WIN_COLS = 16
NA_ROWS_PER_STEP = 8
GLA_HEADS = 4
GLA_DV = 64
GLA_DK = 32
GLA_WIDTH = GLA_HEADS * GLA_DV
GLA_KEY_WIDTH = GLA_HEADS * GLA_DK
GLA_GATE_RANK = 16
GLA_GATE_TAU = 16.0
GLA_BLOCK = 16
ROPE_THETA = 10000.0
D_FF = 2816
NORM_EPS = 1e-6
N_MODS = 6
COND_ROWS = 16
MASK_VALUE = -1e30
LOG2_E = 1.4426950408889634

OFF_POOL = 0
OFF_NA_Q = OFF_POOL + POOL_WIDTH
OFF_NA_K = OFF_NA_Q + NA_WIDTH
OFF_NA_V = OFF_NA_K + NA_WIDTH
OFF_LA_Q = OFF_NA_V + NA_WIDTH
OFF_LA_K = OFF_LA_Q + GLA_KEY_WIDTH
OFF_LA_V = OFF_LA_K + GLA_KEY_WIDTH
OFF_LA_G = OFF_LA_V + GLA_WIDTH
OFF_LR = OFF_LA_G + GLA_WIDTH
IN_WIDTH = OFF_LR + 2 * GLA_GATE_RANK

LANES = 128
HALF_LANES = LANES // 2
assert NA_HEAD_DIM == GLA_DV == HALF_LANES
POOL_TILE = 512
ROW_TILE = 512
IN_ROW_TILE = 1024
IN_SUB_TILE = 256
GLA_TILE = 256
GLA_BATCH_PER_STEP = 8
FF_CHUNK = 256
ADA_COL_TILE = 512
VMEM_LIMIT = 56 * 1024 * 1024


def _dot(a, b):
    return jnp.dot(a, b, preferred_element_type=F32)


def _dot_nt(a, b):
    return lax.dot_general(a, b, (((1,), (1,)), ((), ())), preferred_element_type=F32)


def _dot_tn(a, b):
    return lax.dot_general(a, b, (((0,), (0,)), ((), ())), preferred_element_type=F32)


def _silu(x):
    return x * (1.0 / (1.0 + jnp.exp(-x)))


def _split3(x):
    hi = x.astype(BF16)
    r1 = x - hi.astype(F32)
    mid = r1.astype(BF16)
    lo = (r1 - mid.astype(F32)).astype(BF16)
    return hi, mid, lo


def _half_lane_rms(x, eps):
    lane = lax.broadcasted_iota(jnp.int32, (1, LANES), 1)
    low = lane < HALF_LANES
    cols = []
    for j in range(x.shape[-1] // LANES):
        blk = x[:, j * LANES:(j + 1) * LANES]
        sq = blk * blk
        s_lo = jnp.sum(jnp.where(low, sq, 0.0), axis=-1, keepdims=True)
        s_hi = jnp.sum(jnp.where(low, 0.0, sq), axis=-1, keepdims=True)
        r_lo = lax.rsqrt(s_lo * (1.0 / HALF_LANES) + eps)
        r_hi = lax.rsqrt(s_hi * (1.0 / HALF_LANES) + eps)
        cols.append(blk * jnp.where(low, r_lo, r_hi))
    return jnp.concatenate(cols, axis=-1)


def _row_rms(x, eps):
    return x * lax.rsqrt(jnp.mean(x * x, axis=-1, keepdims=True) + eps)


def _ada_kernel(c_ref, w_ref, b_ref, o_ref):
    s_hi, s_mid, _ = _split3(_silu(c_ref[...]))
    w = w_ref[0]
    w_hi, w_mid, _ = _split3(w)
    acc = _dot(s_hi, w_hi) + _dot(s_mid, w_hi) + _dot(s_hi, w_mid)
    o_ref[0] = acc + b_ref[0]


def _ada(conds, w_ada, b_ada):
    n_out = w_ada.shape[-1]
    return pl.pallas_call(
        _ada_kernel,
        grid=(DEPTH, n_out // ADA_COL_TILE),
        in_specs=[
            pl.BlockSpec((COND_ROWS, D_MODEL), lambda l, j: (0, 0)),
            pl.BlockSpec((1, D_MODEL, ADA_COL_TILE), lambda l, j: (l, 0, j)),
            pl.BlockSpec((1, 1, ADA_COL_TILE), lambda l, j: (l, 0, j)),
        ],
        out_specs=pl.BlockSpec((1, COND_ROWS, ADA_COL_TILE), lambda l, j: (l, 0, j)),
        out_shape=jax.ShapeDtypeStruct((DEPTH, COND_ROWS, n_out), F32),
        name="ada",
    )(conds, w_ada, b_ada.reshape(DEPTH, 1, n_out))


def _inproj_kernel(*refs, rope, kv_seq):
    x_ref, m_ref, g1_ref, w_ref, qg_ref, kg_ref, wg_ref, bg_ref = refs[:8]
    pos = 8
    if rope:
        cos_ref, sin_ref = refs[pos:pos + 2]
        pos += 2
    if kv_seq:
        pos += 2
    u_ref, q_ref, k_ref, v_ref, lq_ref, lk_ref, lv_ref, lg_ref, la_ref = refs[pos:pos + 9]
    pos += 9
    if kv_seq:
        kf_ref, vf_ref = refs[pos:pos + 2]

    shift1 = m_ref[0, 0:1, :]
    scale1 = m_ref[0, 1:2, :]
    tm = x_ref.shape[1]
    sub = min(IN_SUB_TILE, tm)

    def normed(i):
        rows = slice(i * sub, (i + 1) * sub)
        h = _row_rms(x_ref[0, rows, :], NORM_EPS) * g1_ref[...] * (1.0 + scale1) + shift1
        return h.astype(BF16)

    def project(i, hb):
        rows = slice(i * sub, (i + 1) * sub)

        def proj(off, width):
            return _dot(hb, w_ref[:, off:off + width])

        lr = proj(OFF_LR, 2 * GLA_GATE_RANK).astype(BF16)
        u_ref[0, rows, :] = proj(OFF_POOL, POOL_WIDTH)

        qn = _half_lane_rms(proj(OFF_NA_Q, NA_WIDTH), NORM_EPS) * qg_ref[...]
        q_ref[0, rows, :] = (qn * (NA_HEAD_DIM ** -0.5 * LOG2_E)).astype(BF16)
        kn = _half_lane_rms(proj(OFF_NA_K, NA_WIDTH), NORM_EPS) * kg_ref[...]
        k_ref[0, rows, :] = kn.astype(BF16)
        vn = proj(OFF_NA_V, NA_WIDTH)
        v_ref[0, rows, :] = vn.astype(BF16)
        if kv_seq:
            piece = min(sub, kv_seq)
            for j in range(sub // piece):
                first = i * sub + j * piece
                seq, off = (first // kv_seq, first % kv_seq) if tm >= kv_seq else (0, first)
                for hh in range(NA_HEADS):
                    cols = slice(hh * NA_HEAD_DIM, (hh + 1) * NA_HEAD_DIM)
                    kf_ref[seq, 0, hh, off:off + piece, :] = kn[j * piece:(j + 1) * piece, cols]
                    vf_ref[seq, 0, hh, off:off + piece, :] = vn[j * piece:(j + 1) * piece, cols]

        z = _dot(lr, wg_ref[...]) + bg_ref[...]
        log_sig = jnp.minimum(z, 0.0) - jnp.log1p(jnp.exp(-jnp.abs(z)))
        la_ref[0, rows, :] = log_sig * (LOG2_E / GLA_GATE_TAU)

        lqk = proj(OFF_LA_Q, 2 * GLA_KEY_WIDTH)
        lq = lqk[:, :GLA_KEY_WIDTH] * (GLA_DK ** -0.5)
        lk = lqk[:, GLA_KEY_WIDTH:]
        if rope:
            lane = lax.broadcasted_iota(jnp.int32, (1, LANES), 1)
            first = (lane % 16) < 8
            cos = cos_ref[rows, :]
            sin = sin_ref[rows, :]

            def rot(t):
                partner = jnp.where(first, pltpu.roll(t, LANES - 8, axis=1), pltpu.roll(t, 8, axis=1))
                return t * cos + partner * sin

            lq = rot(lq)
            lk = rot(lk)
        lq_ref[0, rows, :] = lq
        lk_ref[0, rows, :] = lk
        lv_ref[0, rows, :] = proj(OFF_LA_V, GLA_WIDTH)
        lg_ref[0, rows, :] = proj(OFF_LA_G, GLA_WIDTH)

    hb_next = normed(0)
    for i in range(tm // sub):
        hb = hb_next
        hb_next = normed(i + 1) if (i + 1) * sub < tm else None
        project(i, hb)


def _in_proj(x, mods_l, row_of_batch, lw, rope_tabs, kv_out):
    B, L, _ = x.shape
    tm = min(IN_ROW_TILE, L)
    rope = rope_tabs is not None
    const = lambda b, i: (0, 0)
    tile = lambda b, i: (b, i, 0)

    in_specs = [
        pl.BlockSpec((1, tm, D_MODEL), tile),
        pl.BlockSpec((1, N_MODS, D_MODEL), lambda b, i: (row_of_batch(b), 0, 0)),
        pl.BlockSpec((1, D_MODEL), const),
        pl.BlockSpec((D_MODEL, IN_WIDTH), const),
        pl.BlockSpec((1, NA_WIDTH), const),
        pl.BlockSpec((1, NA_WIDTH), const),
        pl.BlockSpec((2 * GLA_GATE_RANK, 2 * GLA_KEY_WIDTH), const),
        pl.BlockSpec((1, 2 * GLA_KEY_WIDTH), const),
    ]
    args = [x, mods_l, lw["norm1_gain"], lw["w_in"], lw["q_gain"], lw["k_gain"], lw["w_gate"], lw["b_gate"]]
    if rope:
        in_specs += [pl.BlockSpec((tm, GLA_KEY_WIDTH), lambda b, i: (i, 0))] * 2
        args += list(rope_tabs)

    widths = [(POOL_WIDTH, F32), (NA_WIDTH, BF16), (NA_WIDTH, BF16), (NA_WIDTH, BF16),
              (GLA_KEY_WIDTH, F32), (GLA_KEY_WIDTH, F32), (GLA_WIDTH, F32), (GLA_WIDTH, F32),
              (2 * GLA_KEY_WIDTH, F32)]
    out_specs = [pl.BlockSpec((1, tm, w), tile) for w, _ in widths]
    out_shape = [jax.ShapeDtypeStruct((B, L, w), dt) for w, dt in widths]
    aliases = {}
    kv_seq = None
    if kv_out is not None:
        k_buf, v_buf, layer, kv_seq = kv_out
        assert B == 1 and (tm % kv_seq == 0 or kv_seq % tm == 0)
        if tm >= kv_seq:
            kv_spec = pl.BlockSpec((tm // kv_seq, 1, NA_HEADS, kv_seq, NA_HEAD_DIM), lambda b, i: (i, layer, 0, 0, 0))
        else:
            per_seq = kv_seq // tm
            kv_spec = pl.BlockSpec((1, 1, NA_HEADS, tm, NA_HEAD_DIM),
                                   lambda b, i: (i // per_seq, layer, 0, i % per_seq, 0))
        for buf in (k_buf, v_buf):
            aliases[len(args)] = len(out_specs)
            in_specs.append(pl.BlockSpec(memory_space=pl.ANY))
            args.append(buf)
            out_specs.append(kv_spec)
            out_shape.append(jax.ShapeDtypeStruct(buf.shape, buf.dtype))

    return pl.pallas_call(
        functools.partial(_inproj_kernel, rope=rope, kv_seq=kv_seq),
        grid=(B, L // tm),
        in_specs=in_specs,
        out_specs=out_specs,
        out_shape=out_shape,
        input_output_aliases=aliases,
        compiler_params=pltpu.CompilerParams(vmem_limit_bytes=VMEM_LIMIT),
        name="in_proj",
    )(*args)


def _pool_kernel(u_ref, w_ref, sc_ref, o_ref, pad_ref, s2_ref, s4_ref, s8_ref, *, L, tp):
    P = POOL_PAD
    zeros = jnp.zeros((P, POOL_WIDTH), F32)
    pad_ref[0:P, :] = zeros
    pad_ref[P + L:P + L + P, :] = zeros
    pad_ref[P:P + L, :] = u_ref[0]
    narrow = slice(0, LANES)
    wide = slice(LANES, 2 * LANES)
    lane = lax.broadcasted_iota(jnp.int32, (1, LANES), 1)
    low = lane < POOL_GROUP_DIM

    def chunks(level):
        lo, hi = 8 * level, L + 2 * P - 8 * level
        return [(a, min(a + tp, hi)) for a in range(lo, hi, tp)]

    for a, b in chunks(1):
        s2_ref[a:b, :] = pad_ref[a - 1:b - 1, wide] + pad_ref[a:b, wide]
    for a, b in chunks(2):
        s4_ref[a:b, :] = s2_ref[a - 1:b - 1, :] + s2_ref[a + 1:b + 1, :]
    for a, b in chunks(3):
        s8_ref[a:b, :] = s4_ref[a - 2:b - 2, :] + s4_ref[a + 2:b + 2, :]

    for base in range(0, L, tp):
        p0 = P + base

        def rows(ref, off, cols):
            return ref[p0 + off:p0 + off + tp, cols]

        u_a = rows(pad_ref, 0, narrow)
        w2 = rows(pad_ref, -1, narrow) + u_a
        w4 = w2 + rows(pad_ref, -2, narrow) + rows(pad_ref, 1, narrow)
        u_b = rows(pad_ref, 0, wide)
        w8 = rows(s8_ref, 0, slice(None))
        w16 = rows(s8_ref, -4, slice(None)) + rows(s8_ref, 4, slice(None))

        if base < POOL_HALO or base + tp > L - POOL_HALO:
            t = base + lax.broadcasted_iota(jnp.int32, (tp, LANES), 0)

            def mean(total, win):
                count = jnp.minimum(t + win // 2, L) - jnp.maximum(t - win // 2, 0)
                return total / count.astype(F32)
        else:
            def mean(total, win):
                return total * (1.0 / win)

        mean_a = jnp.where(low, mean(w2, 2), mean(w4, 4))
        mean_b = jnp.where(low, mean(w8, 8), mean(w16, 16))
        d = jnp.concatenate([mean_a - u_a, mean_b - u_b], axis=-1).astype(BF16)
        y = _dot(d, w_ref[...]) * sc_ref[...]
        o_ref[0, base:base + tp, :] = y.astype(BF16)


def _pool(u, lw):
    B, L, _ = u.shape
    tp = min(POOL_TILE, L)
    return pl.pallas_call(
        functools.partial(_pool_kernel, L=L, tp=tp),
        grid=(B,),
        in_specs=[
            pl.BlockSpec((1, L, POOL_WIDTH), lambda b: (b, 0, 0)),
            pl.BlockSpec((POOL_WIDTH, POOL_WIDTH), lambda b: (0, 0)),
            pl.BlockSpec((1, POOL_WIDTH), lambda b: (0, 0)),
        ],
        out_specs=pl.BlockSpec((1, L, POOL_WIDTH), lambda b: (b, 0, 0)),
        out_shape=jax.ShapeDtypeStruct((B, L, POOL_WIDTH), BF16),
        scratch_shapes=[pltpu.VMEM((L + 2 * POOL_PAD, POOL_WIDTH), F32)]
                       + [pltpu.VMEM((L + 2 * POOL_PAD, LANES), F32)] * 3,
        compiler_params=pltpu.CompilerParams(vmem_limit_bytes=VMEM_LIMIT),
        name="pool",
    )(u, lw["w_pool_bd"], lw["pool_scale"])


def _ctx_attn_kernel(q_ref, k_ref, v_ref, o_ref):
    L = q_ref.shape[1]
    lane = lax.broadcasted_iota(jnp.int32, (1, LANES), 1)
    even = lane < NA_HEAD_DIM
    zero = jnp.zeros((), BF16)

    def scores(j):
        cols = slice(j * LANES, (j + 1) * LANES)
        qp = q_ref[0, :, cols]
        q2 = jnp.concatenate([jnp.where(even, qp, zero), jnp.where(even, zero, qp)], axis=0)
        return _dot_nt(q2, k_ref[0, :, cols])

    def finish(j, s):
        cols = slice(j * LANES, (j + 1) * LANES)
        p = jnp.exp2(s - jnp.max(s, axis=-1, keepdims=True))
        denom = jnp.sum(p, axis=-1, keepdims=True)
        o2 = _dot(p.astype(BF16), v_ref[0, :, cols]) / denom
        o_ref[0, :, cols] = jnp.where(even, o2[:L], o2[L:]).astype(BF16)

    n_pairs = NA_HEADS // 2
    s_next = scores(0)
    for j in range(n_pairs):
        s_cur = s_next
        s_next = scores(j + 1) if j + 1 < n_pairs else None
        finish(j, s_cur)


def _ctx_attn(q, k, v):
    B, L, _ = q.shape
    spec = pl.BlockSpec((1, L, NA_WIDTH), lambda b: (b, 0, 0))
    return pl.pallas_call(
        _ctx_attn_kernel,
        grid=(B,),
        in_specs=[spec, spec, spec],
        out_specs=spec,
        out_shape=jax.ShapeDtypeStruct((B, L, NA_WIDTH), BF16),
        name="ctx_attn",
    )(q, k, v)


def _band_start(r, n_rows):
    return jnp.clip(r - NA_WIN_ROWS // 2, 0, n_rows - NA_WIN_ROWS)


def _lat_attn_kernel(q_ref, k_ref, v_ref, kc_ref, vc_ref, bias_ref, o_ref, *, n_rows):
    band = NA_WIN_ROWS * GRID_W
    lane = lax.broadcasted_iota(jnp.int32, (1, LANES), 1)
    even = lane < NA_HEAD_DIM
    zero = jnp.zeros((), BF16)

    def band_of(rr):
        r = pl.program_id(1) * NA_ROWS_PER_STEP + rr
        first = _band_start(r, n_rows)
        return pl.multiple_of(first * GRID_W, GRID_W), NA_WIN_ROWS - 1 - (r - first)

    def scores(rr, j):
        start, tile0 = band_of(rr)
        cols = slice(j * LANES, (j + 1) * LANES)
        qp = q_ref[0, rr * GRID_W:(rr + 1) * GRID_W, cols]
        q2 = jnp.concatenate([jnp.where(even, qp, zero), jnp.where(even, zero, qp)], axis=0)
        bias = jnp.concatenate(
            [jnp.concatenate([bias_ref[2 * j + hh, tile0 + 2 * ii] for ii in range(NA_WIN_ROWS // 2)], axis=-1)
             for hh in range(2)], axis=0)
        return _dot_nt(q2, k_ref[0, pl.ds(start, band), cols]) + bias, _dot_nt(q2, kc_ref[0, :, cols])

    def softmax(s_loc, s_ctx):
        m = jnp.maximum(jnp.max(s_loc, axis=-1, keepdims=True), jnp.max(s_ctx, axis=-1, keepdims=True))
        p_loc = jnp.exp2(s_loc - m)
        p_ctx = jnp.exp2(s_ctx - m)
        denom = jnp.sum(p_loc, axis=-1, keepdims=True) + jnp.sum(p_ctx, axis=-1, keepdims=True)
        return p_loc.astype(BF16), p_ctx.astype(BF16), denom

    def values(rr, j, p_loc, p_ctx, denom):
        start, _ = band_of(rr)
        cols = slice(j * LANES, (j + 1) * LANES)
        o2 = (_dot(p_loc, v_ref[0, pl.ds(start, band), cols]) + _dot(p_ctx, vc_ref[0, :, cols])) / denom
        o_ref[0, rr * GRID_W:(rr + 1) * GRID_W, cols] = jnp.where(even, o2[:GRID_W], o2[GRID_W:]).astype(BF16)

    chains = [(rr, j) for rr in range(NA_ROWS_PER_STEP) for j in range(NA_HEADS // 2)]
    s_next = scores(*chains[0])
    p_prev = None
    for i, chain in enumerate(chains):
        s_cur = s_next
        s_next = scores(*chains[i + 1]) if i + 1 < len(chains) else None
        p_cur = softmax(*s_cur)
        if p_prev is not None:
            values(*chains[i - 1], *p_prev)
        p_prev = p_cur
    values(*chains[-1], *p_prev)


def _lat_attn(q, k, v, kc, vc, bias_tiles):
    B, L, _ = q.shape
    n_rows = L // GRID_W
    assert n_rows >= NA_WIN_ROWS and n_rows % NA_ROWS_PER_STEP == 0
    past = kc.shape[1]
    whole = pl.BlockSpec((1, L, NA_WIDTH), lambda b, r: (b, 0, 0))
    rows = pl.BlockSpec((1, NA_ROWS_PER_STEP * GRID_W, NA_WIDTH), lambda b, r: (b, r, 0))
    ctx = pl.BlockSpec((1, past, NA_WIDTH), lambda b, r: (b, 0, 0))
    bias = pl.BlockSpec(bias_tiles.shape, lambda b, r: (0, 0, 0, 0))
    return pl.pallas_call(
        functools.partial(_lat_attn_kernel, n_rows=n_rows),
        grid=(B, n_rows // NA_ROWS_PER_STEP),
        in_specs=[rows, whole, whole, ctx, ctx, bias],
        out_specs=rows,
        out_shape=jax.ShapeDtypeStruct((B, L, NA_WIDTH), BF16),
        compiler_params=pltpu.CompilerParams(vmem_limit_bytes=VMEM_LIMIT),
        name="lat_attn",
    )(q, k, v, kc, vc, bias_tiles)


def _bias_tile_constants():
    n_dc = 2 * NA_WIN_COLS
    w = np.arange(GRID_W)[:, None]
    cc = np.arange(2 * GRID_W)[None, :]
    c = cc % GRID_W
    cs = np.clip(w - NA_WIN_COLS // 2, 0, GRID_W - NA_WIN_COLS)
    valid = (c >= cs) & (c < cs + NA_WIN_COLS)
    k_idx = (cc // GRID_W) * n_dc + (c - w + NA_WIN_COLS - 1)
    onehot = (np.arange(2 * n_dc)[:, None, None] == k_idx[None]) & valid[None]
    onehot = onehot.reshape(2 * n_dc, GRID_W * 2 * GRID_W)
    mask = np.where(valid, 0.0, MASK_VALUE).reshape(1, GRID_W * 2 * GRID_W)
    return jnp.asarray(onehot, BF16), jnp.asarray(mask, F32)


def _bias_tiles_kernel(rb_ref, oh_ref, mask_ref, o_ref):
    hi, mid, lo = _split3(rb_ref[...])
    oh = oh_ref[...]
    o_ref[...] = (_dot(hi, oh) + _dot(mid, oh) + _dot(lo, oh)) * LOG2_E + mask_ref[...]


def _bias_tiles(rel_bias_l):
    n_dr = 2 * NA_WIN_ROWS - 1
    padded = jnp.pad(rel_bias_l, ((0, 0), (0, 0), (0, 1)))
    pairs = jnp.concatenate([padded[:, :-1], padded[:, 1:]], axis=-1)
    pairs = pairs.reshape(NA_HEADS * (n_dr - 1), 4 * NA_WIN_COLS)
    onehot, mask = _bias_tile_constants()
    full = lambda shape: pl.BlockSpec(shape, lambda: (0,) * len(shape))
    out = pl.pallas_call(
        _bias_tiles_kernel,
        in_specs=[full(pairs.shape), full(onehot.shape), full(mask.shape)],
        out_specs=full((pairs.shape[0], onehot.shape[1])),
        out_shape=jax.ShapeDtypeStruct((pairs.shape[0], onehot.shape[1]), F32),
        name="bias_tiles",
    )(pairs, onehot, mask)
    return out.reshape(NA_HEADS, n_dr - 1, GRID_W, 2 * GRID_W)


def _gla_block(q_ref, k_ref, v_ref, b_ref, st_ref, o_ref, expand, head_mask, bi, r0, reverse):
    half = GLA_BLOCK // 2
    rows = pl.ds(r0, GLA_BLOCK)
    qb = q_ref[bi, rows, :]
    kb = k_ref[bi, rows, :]
    vb = v_ref[bi, rows, :]
    bb = b_ref[bi, rows, :]
    end_row = 0 if reverse else GLA_BLOCK - 1
    b_end = bb[end_row:end_row + 1, :]
    st = st_ref[bi]

    o_inter = _dot_nt((qb * jnp.exp2(bb)).astype(BF16), st.astype(BF16))

    t_in = lax.broadcasted_iota(jnp.int32, (half, 1), 0)
    q_half = (qb[:half], qb[half:])
    b_half = (bb[:half], bb[half:])
    pieces, owners = [], []
    for s in range(GLA_BLOCK):
        hs = s // half
        k_s = kb[s:s + 1, :]
        b_s = bb[s:s + 1, :]
        for ht in range(2):
            if (ht > hs) if reverse else (ht < hs):
                continue
            decay = jnp.exp2(b_half[ht] - b_s)
            if ht == hs:
                seen = (t_in <= s - hs * half) if reverse else (t_in >= s - hs * half)
                decay = jnp.where(seen, decay, 0.0)
            pieces.append(q_half[ht] * k_s * decay)
            owners.append((ht, s))
    a_exp = _dot(jnp.concatenate(pieces, axis=0).astype(BF16), expand)
    o_intra = [jnp.zeros((half, GLA_WIDTH), F32), jnp.zeros((half, GLA_WIDTH), F32)]
    for i, (ht, s) in enumerate(owners):
        o_intra[ht] = o_intra[ht] + a_exp[i * half:(i + 1) * half, :] * vb[s:s + 1, :]
    o_ref[bi, rows, :] = o_inter + jnp.concatenate(o_intra, axis=0)

    k_hat = (kb * jnp.exp2(b_end - bb)).astype(BF16)
    upd = _dot_tn(vb.astype(BF16), k_hat)
    st_ref[bi] = st * jnp.exp2(b_end) + upd * head_mask


def _gla_kernel(qf_ref, kf_ref, vf_ref, laf_ref, qb_ref, kb_ref, vb_ref, lab_ref, s0f_ref, s0b_ref,
                trif_ref, trib_ref, e_ref, of_ref, ob_ref, sff_ref, sfb_ref,
                stf_ref, stb_ref, bf_ref, bb_ref, *, tm, nb):
    n = pl.program_id(1)
    n_blocks = tm // GLA_BLOCK

    @pl.when(n == 0)
    def _():
        stf_ref[...] = s0f_ref[...]
        stb_ref[...] = s0b_ref[...]

    for la_ref, tri_ref, b_ref in ((laf_ref, trif_ref, bf_ref), (lab_ref, trib_ref, bb_ref)):
        tri = tri_ref[...]
        for bi in range(nb):
            g_hi, g_mid, _ = _split3(la_ref[bi])
            sums = _dot(tri, jnp.concatenate([g_hi, g_mid], axis=-1))
            b_ref[bi] = sums[:, :GLA_KEY_WIDTH] + sums[:, GLA_KEY_WIDTH:]

    expand = e_ref[...]
    head_v = lax.broadcasted_iota(jnp.int32, (GLA_WIDTH, GLA_KEY_WIDTH), 0) // GLA_DV
    head_k = lax.broadcasted_iota(jnp.int32, (GLA_WIDTH, GLA_KEY_WIDTH), 1) // GLA_DK
    head_mask = jnp.where(head_v == head_k, 1.0, 0.0)

    def block(j, carry):
        r_f = pl.multiple_of(j * GLA_BLOCK, GLA_BLOCK)
        r_b = pl.multiple_of((n_blocks - 1 - j) * GLA_BLOCK, GLA_BLOCK)
        for bi in range(nb):
            _gla_block(qf_ref, kf_ref, vf_ref, bf_ref, stf_ref, of_ref, expand, head_mask, bi, r_f, reverse=False)
            _gla_block(qb_ref, kb_ref, vb_ref, bb_ref, stb_ref, ob_ref, expand, head_mask, bi, r_b, reverse=True)
        return carry

    lax.fori_loop(0, n_blocks, block, 0)

    @pl.when(n == pl.num_programs(1) - 1)
    def _():
        sff_ref[...] = stf_ref[...]
        sfb_ref[...] = stb_ref[...]


def _block_tri(tm, reverse):
    t = np.arange(tm)[:, None]
    s = np.arange(tm)[None, :]
    same = (t // GLA_BLOCK) == (s // GLA_BLOCK)
    return jnp.asarray(same & ((s >= t) if reverse else (s <= t)), BF16)


def _gla_scan(q, k, v, la, s0f_t, s0b_t, expand):
    B, L, _ = q.shape
    tm = min(GLA_TILE, L)
    n_tiles = L // tm
    nb = math.gcd(B, GLA_BATCH_PER_STEP)
    fwd = lambda b, n: (b, n, 0)
    bwd = lambda b, n: (b, n_tiles - 1 - n, 0)
    state = pl.BlockSpec((nb, GLA_WIDTH, GLA_KEY_WIDTH), lambda b, n: (b, 0, 0))
    const = lambda shape: pl.BlockSpec(shape, lambda b, n: (0, 0))

    def operands(tmap, la_col):
        return [pl.BlockSpec((nb, tm, GLA_KEY_WIDTH), tmap), pl.BlockSpec((nb, tm, GLA_KEY_WIDTH), tmap),
                pl.BlockSpec((nb, tm, GLA_WIDTH), tmap),
                pl.BlockSpec((nb, tm, GLA_KEY_WIDTH), lambda b, n: tmap(b, n)[:2] + (la_col,))]

    return pl.pallas_call(
        functools.partial(_gla_kernel, tm=tm, nb=nb),
        grid=(B // nb, n_tiles),
        in_specs=operands(fwd, 0) + operands(bwd, 1) + [state, state, const((tm, tm)), const((tm, tm)),
                                                        const((GLA_KEY_WIDTH, GLA_WIDTH))],
        out_specs=[pl.BlockSpec((nb, tm, GLA_WIDTH), fwd), pl.BlockSpec((nb, tm, GLA_WIDTH), bwd), state, state],
        out_shape=[jax.ShapeDtypeStruct((B, L, GLA_WIDTH), F32), jax.ShapeDtypeStruct((B, L, GLA_WIDTH), F32),
                   jax.ShapeDtypeStruct((B, GLA_WIDTH, GLA_KEY_WIDTH), F32),
                   jax.ShapeDtypeStruct((B, GLA_WIDTH, GLA_KEY_WIDTH), F32)],
        scratch_shapes=[pltpu.VMEM((nb, GLA_WIDTH, GLA_KEY_WIDTH), F32), pltpu.VMEM((nb, GLA_WIDTH, GLA_KEY_WIDTH), F32),
                        pltpu.VMEM((nb, tm, GLA_KEY_WIDTH), F32), pltpu.VMEM((nb, tm, GLA_KEY_WIDTH), F32)],
        compiler_params=pltpu.CompilerParams(vmem_limit_bytes=VMEM_LIMIT),
        name="gla",
    )(q, k, v, la, q, k, v, la, s0f_t, s0b_t, _block_tri(tm, False), _block_tri(tm, True), expand)


def _state_to_kernel(s):
    B = s.shape[0]
    same_head = np.eye(GLA_HEADS, dtype=bool)[None, :, None, :, None]
    st = jnp.where(same_head, s.transpose(0, 1, 3, 2)[:, :, :, None, :], 0.0)
    return st.reshape(B, GLA_WIDTH, GLA_KEY_WIDTH)


def _state_from_kernel(st):
    B = st.shape[0]
    blocks = st.reshape(B, GLA_HEADS, GLA_DV, GLA_HEADS, GLA_DK)
    diag = jnp.stack([blocks[:, h, :, h, :] for h in range(GLA_HEADS)], axis=1)
    return diag.transpose(0, 1, 3, 2)


def _post_kernel(x_ref, m_ref, yp_ref, yn_ref, of_ref, ob_ref, lg_ref, gg_ref, g2_ref,
                 wo_ref, wi_ref, wf_ref, o_ref, acc_ref):
    gate1 = m_ref[0, 2:3, :]
    shift2 = m_ref[0, 3:4, :]
    scale2 = m_ref[0, 4:5, :]
    gate2 = m_ref[0, 5:6, :]

    o_la = _half_lane_rms(of_ref[0] + ob_ref[0], NORM_EPS) * gg_ref[...]
    y_la = (o_la * _silu(lg_ref[0])).astype(BF16)
    mixed = (_dot(yp_ref[0], wo_ref[0:POOL_WIDTH, :])
             + _dot(yn_ref[0], wo_ref[POOL_WIDTH:POOL_WIDTH + NA_WIDTH, :])
             + _dot(y_la, wo_ref[POOL_WIDTH + NA_WIDTH:D_MODEL, :]))
    x1 = x_ref[0] + gate1 * mixed

    h2 = (_row_rms(x1, NORM_EPS) * g2_ref[...] * (1.0 + scale2) + shift2).astype(BF16)
    for c0 in range(0, D_FF, FF_CHUNK):
        hg = _dot(h2, wi_ref[:, c0:c0 + FF_CHUNK])
        hu = _dot(h2, wi_ref[:, D_FF + c0:D_FF + c0 + FF_CHUNK])
        acc_ref[:, c0:c0 + FF_CHUNK] = (_silu(hg) * hu).astype(BF16)
    o_ref[0] = x1 + gate2 * _dot(acc_ref[...], wf_ref[...])


def _post(x, mods_l, row_of_batch, y_pool, y_na, o_f, o_b, lg, lw):
    B, L, _ = x.shape
    tm = min(ROW_TILE, L)
    const = lambda b, i: (0, 0)
    tile = lambda b, i: (b, i, 0)

    def resident(shape):
        return pl.BlockSpec(shape, const, pipeline_mode=pl.Buffered(1))

    return pl.pallas_call(
        _post_kernel,
        grid=(B, L // tm),
        in_specs=[
            pl.BlockSpec((1, tm, D_MODEL), tile),
            pl.BlockSpec((1, N_MODS, D_MODEL), lambda b, i: (row_of_batch(b), 0, 0)),
            pl.BlockSpec((1, tm, POOL_WIDTH), tile),
            pl.BlockSpec((1, tm, NA_WIDTH), tile),
            pl.BlockSpec((1, tm, GLA_WIDTH), tile),
            pl.BlockSpec((1, tm, GLA_WIDTH), tile),
            pl.BlockSpec((1, tm, GLA_WIDTH), tile),
            pl.BlockSpec((1, GLA_WIDTH), const),
            pl.BlockSpec((1, D_MODEL), const),
            resident((D_MODEL, D_MODEL)),
            resident((D_MODEL, 2 * D_FF)),
            resident((D_FF, D_MODEL)),
        ],
        out_specs=pl.BlockSpec((1, tm, D_MODEL), tile),
        out_shape=jax.ShapeDtypeStruct((B, L, D_MODEL), F32),
        scratch_shapes=[pltpu.VMEM((tm, D_FF), BF16)],
        compiler_params=pltpu.CompilerParams(vmem_limit_bytes=VMEM_LIMIT),
        name="post",
    )(x, mods_l, y_pool, y_na, o_f, o_b, lg, lw["gla_gain"], lw["norm2_gain"],
      lw["w_out"], lw["w_ffn_in"], lw["w_ffn_out"])


def _rope_tables(L):
    t = jnp.arange(L)
    row = (t // GRID_W).astype(F32)
    col = (t % GRID_W).astype(F32)
    half = GLA_DK // 2
    inv_freq = ROPE_THETA ** (-jnp.arange(0, half, 2, dtype=F32) / half)
    ang_r = row[:, None] * inv_freq
    ang_c = col[:, None] * inv_freq
    sign = jnp.concatenate([-jnp.ones((half // 2,), F32), jnp.ones((half // 2,), F32)])

    def lanes(fn, signed):
        per_axis = []
        for ang in (ang_r, ang_c):
            v = jnp.concatenate([fn(ang), fn(ang)], axis=-1)
            per_axis.append(v * sign if signed else v)
        return jnp.tile(jnp.concatenate(per_axis, axis=-1), (1, GLA_HEADS))

    return lanes(jnp.cos, False), lanes(jnp.sin, True)


def _layer_weights(l, w):
    tile_heads = lambda g, n: jnp.tile(g, n)[None, :]
    return {
        "norm1_gain": w["norm1_gain"][l][None, :],
        "norm2_gain": w["norm2_gain"][l][None, :],
        "w_in": w["w_in"][l].astype(BF16),
        "q_gain": tile_heads(w["q_norm_gain"][l], NA_HEADS),
        "k_gain": tile_heads(w["k_norm_gain"][l], NA_HEADS),
        "w_gate": jax.scipy.linalg.block_diag(w["w_gate_f"][l], w["w_gate_b"][l]).astype(BF16),
        "b_gate": jnp.concatenate([w["b_gate_f"][l], w["b_gate_b"][l]])[None, :],
        "w_pool_bd": jax.scipy.linalg.block_diag(*[w["w_pool"][l, g] for g in range(len(POOL_WINDOWS))]).astype(BF16),
        "pool_scale": w["pool_scale"][l][None, :],
        "gla_gain": tile_heads(w["gla_norm_gain"][l], GLA_HEADS),
        "w_out": w["w_out"][l].astype(BF16),
        "w_ffn_in": w["w_ffn_in"][l].astype(BF16),
        "w_ffn_out": w["w_ffn_out"][l].astype(BF16),
    }


def _trunk_layer(x, mods_l, row_of_batch, lw, expand, s0_f, s0_b, latent, kv_out=None):
    is_ctx = latent is None
    B, L, _ = x.shape
    per_token = (lambda a: a.reshape(1, B * L, a.shape[-1])) if is_ctx else (lambda a: a)
    per_seq = lambda a: a.reshape(B, L, a.shape[-1])
    outs = _in_proj(per_token(x), mods_l, row_of_batch, lw, None if is_ctx else latent[0],
                    kv_out + (L,) if is_ctx else None)
    u, q, k, v, lq, lk, lv, lg, la = [per_seq(a) for a in outs[:9]]
    y_pool = _pool(u, lw)
    if is_ctx:
        y_na = _ctx_attn(q, k, v)
    else:
        y_na = _lat_attn(q, k, v, latent[1], latent[2], latent[3])
    o_f, o_b, s_f, s_b = _gla_scan(lq, lk, lv, la, s0_f, s0_b, expand)
    x = per_seq(_post(per_token(x), mods_l, row_of_batch,
                      *[per_token(a) for a in (y_pool, y_na, o_f, o_b, lg)], lw))
    return x, s_f, s_b, tuple(outs[9:])


def kernel(x_prompt, x_sample, c, cache_na_k, cache_na_v, state_gla_fwd, state_gla_bwd, c_ctx, w_ada, b_ada, norm1_gain, norm2_gain, w_in, w_pool, pool_scale, q_norm_gain, k_norm_gain, rel_bias, w_gate_f, b_gate_f, w_gate_b, b_gate_b, gla_norm_gain, w_out, w_ffn_in, w_ffn_out):
    weights = dict(norm1_gain=norm1_gain, norm2_gain=norm2_gain, w_in=w_in, w_pool=w_pool, pool_scale=pool_scale,
                   q_norm_gain=q_norm_gain, k_norm_gain=k_norm_gain, w_gate_f=w_gate_f, b_gate_f=b_gate_f,
                   w_gate_b=w_gate_b, b_gate_b=b_gate_b, gla_norm_gain=gla_norm_gain, w_out=w_out,
                   w_ffn_in=w_ffn_in, w_ffn_out=w_ffn_out)
    B_ctx, L_ctx, _ = x_prompt.shape
    B_lat, L_lat, _ = x_sample.shape
    assert 1 + B_lat <= COND_ROWS

    conds = jnp.zeros((COND_ROWS, D_MODEL), F32).at[0].set(c_ctx).at[1:1 + B_lat].set(c)
    mods = _ada(conds, w_ada, b_ada).reshape(DEPTH, COND_ROWS, N_MODS, D_MODEL)
    lws = [_layer_weights(l, weights) for l in range(DEPTH)]
    head_of_k = np.arange(GLA_KEY_WIDTH) // GLA_DK
    head_of_v = np.arange(GLA_WIDTH) // GLA_DV
    expand = jnp.asarray(head_of_k[:, None] == head_of_v[None, :], BF16)

    xp = x_prompt
    zero_state = jnp.zeros((B_ctx, GLA_WIDTH, GLA_KEY_WIDTH), F32)
    new_k = jnp.zeros((B_ctx, DEPTH, NA_HEADS, L_ctx, NA_HEAD_DIM), F32)
    new_v = jnp.zeros((B_ctx, DEPTH, NA_HEADS, L_ctx, NA_HEAD_DIM), F32)
    sfs, sbs = [], []
    for l in range(DEPTH):
        xp, s_f, s_b, (new_k, new_v) = _trunk_layer(xp, mods[l], lambda b: 0, lws[l], expand,
                                                     zero_state, zero_state, None, (new_k, new_v, l))
        sfs.append(_state_from_kernel(s_f))
        sbs.append(_state_from_kernel(s_b))

    xs = x_sample
    rope_tabs = _rope_tables(L_lat)
    from_heads = lambda a: a.transpose(0, 2, 1, 3).reshape(B_lat, a.shape[2], NA_WIDTH).astype(BF16)
    for l in range(DEPTH):
        latent = (rope_tabs, from_heads(cache_na_k[:, l]), from_heads(cache_na_v[:, l]), _bias_tiles(rel_bias[l]))
        xs, _, _, _ = _trunk_layer(xs, mods[l], lambda b: b + 1, lws[l], expand,
                                   _state_to_kernel(state_gla_fwd[:, l]), _state_to_kernel(state_gla_bwd[:, l]),
                                   latent)

    return (xp, xs, new_k, new_v, jnp.stack(sfs, axis=1), jnp.stack(sbs, axis=1))
```

```python
import functools
import math

import numpy as np
import jax
import jax.numpy as jnp
from jax import lax
from jax.experimental import pallas as pl
from jax.experimental.pallas import tpu as pltpu

F32 = jnp.float32
BF16 = jnp.bfloat16

D_MODEL = 1024
DEPTH = 2
GRID_W = 64
POOL_WIDTH = 256
POOL_GROUP_DIM = 64
POOL_WINDOWS = (2, 4, 8, 16)
POOL_HALO = max(POOL_WINDOWS) // 2
POOL_PAD = 8 * len(POOL_WINDOWS)
NA_HEADS = 8
NA_HEAD_DIM = 64
NA_WIDTH = NA_HEADS * NA_HEAD_DIM
NA_WIN_ROWS = 8
NA_WIN_COLS = 16
NA_ROWS_PER_STEP = 8
GLA_HEADS = 4
GLA_DV = 64
GLA_DK = 32
GLA_WIDTH = GLA_HEADS * GLA_DV
GLA_KEY_WIDTH = GLA_HEADS * GLA_DK
GLA_GATE_RANK = 16
GLA_GATE_TAU = 16.0
GLA_BLOCK = 16
ROPE_THETA = 10000.0
D_FF = 2816
NORM_EPS = 1e-6
N_MODS = 6
COND_ROWS = 16
MASK_VALUE = -1e30
LOG2_E = 1.4426950408889634

OFF_POOL = 0
OFF_NA_Q = OFF_POOL + POOL_WIDTH
OFF_NA_K = OFF_NA_Q + NA_WIDTH
OFF_NA_V = OFF_NA_K + NA_WIDTH
OFF_LA_Q = OFF_NA_V + NA_WIDTH
OFF_LA_K = OFF_LA_Q + GLA_KEY_WIDTH
OFF_LA_V = OFF_LA_K + GLA_KEY_WIDTH
OFF_LA_G = OFF_LA_V + GLA_WIDTH
OFF_LR = OFF_LA_G + GLA_WIDTH
IN_WIDTH = OFF_LR + 2 * GLA_GATE_RANK

LANES = 128
HALF_LANES = LANES // 2
assert NA_HEAD_DIM == GLA_DV == HALF_LANES
POOL_TILE = 512
ROW_TILE = 512
IN_ROW_TILE = 1024
IN_SUB_TILE = 256
GLA_TILE = 256
GLA_BATCH_PER_STEP = 8
FF_CHUNK = 256
ADA_COL_TILE = 1024
VMEM_LIMIT = 56 * 1024 * 1024


def _dot(a, b):
    return jnp.dot(a, b, preferred_element_type=F32)


def _dot_nt(a, b):
    return lax.dot_general(a, b, (((1,), (1,)), ((), ())), preferred_element_type=F32)


def _dot_tn(a, b):
    return lax.dot_general(a, b, (((0,), (0,)), ((), ())), preferred_element_type=F32)


def _silu(x):
    return x * (1.0 / (1.0 + jnp.exp(-x)))


def _split3(x):
    hi = x.astype(BF16)
    r1 = x - hi.astype(F32)
    mid = r1.astype(BF16)
    lo = (r1 - mid.astype(F32)).astype(BF16)
    return hi, mid, lo


def _half_lane_rms(x, eps):
    lane = lax.broadcasted_iota(jnp.int32, (1, LANES), 1)
    low = lane < HALF_LANES
    cols = []
    for j in range(x.shape[-1] // LANES):
        blk = x[:, j * LANES:(j + 1) * LANES]
        sq = blk * blk
        s_lo = jnp.sum(jnp.where(low, sq, 0.0), axis=-1, keepdims=True)
        s_hi = jnp.sum(jnp.where(low, 0.0, sq), axis=-1, keepdims=True)
        r_lo = lax.rsqrt(s_lo * (1.0 / HALF_LANES) + eps)
        r_hi = lax.rsqrt(s_hi * (1.0 / HALF_LANES) + eps)
        cols.append(blk * jnp.where(low, r_lo, r_hi))
    return jnp.concatenate(cols, axis=-1)


def _row_rms(x, eps):
    return x * lax.rsqrt(jnp.mean(x * x, axis=-1, keepdims=True) + eps)


def _ada_kernel(c_ref, w_ref, b_ref, o_ref):
    s_hi, s_mid, _ = _split3(_silu(c_ref[...]))
    w = w_ref[0]
    w_hi, w_mid, _ = _split3(w)
    acc = _dot(s_hi, w_hi) + _dot(s_mid, w_hi) + _dot(s_hi, w_mid)
    o_ref[0] = acc + b_ref[0]


def _ada(conds, w_ada, b_ada):
    n_out = w_ada.shape[-1]
    return pl.pallas_call(
        _ada_kernel,
        grid=(DEPTH, n_out // ADA_COL_TILE),
        in_specs=[
            pl.BlockSpec((COND_ROWS, D_MODEL), lambda l, j: (0, 0)),
            pl.BlockSpec((1, D_MODEL, ADA_COL_TILE), lambda l, j: (l, 0, j)),
            pl.BlockSpec((1, 1, ADA_COL_TILE), lambda l, j: (l, 0, j)),
        ],
        out_specs=pl.BlockSpec((1, COND_ROWS, ADA_COL_TILE), lambda l, j: (l, 0, j)),
        out_shape=jax.ShapeDtypeStruct((DEPTH, COND_ROWS, n_out), F32),
        name="ada",
    )(conds, w_ada, b_ada.reshape(DEPTH, 1, n_out))


def _inproj_kernel(*refs, rope, kv_seq):
    x_ref, m_ref, g1_ref, w_ref, qg_ref, kg_ref, wg_ref, bg_ref = refs[:8]
    pos = 8
    if rope:
        cos_ref, sin_ref = refs[pos:pos + 2]
        pos += 2
    if kv_seq:
        pos += 2
    u_ref, q_ref, k_ref, v_ref, lq_ref, lk_ref, lv_ref, lg_ref, la_ref = refs[pos:pos + 9]
    pos += 9
    if kv_seq:
        kf_ref, vf_ref = refs[pos:pos + 2]

    shift1 = m_ref[0, 0:1, :]
    scale1 = m_ref[0, 1:2, :]
    tm = x_ref.shape[1]
    sub = min(IN_SUB_TILE, tm)

    def normed(i):
        rows = slice(i * sub, (i + 1) * sub)
        h = _row_rms(x_ref[0, rows, :], NORM_EPS) * g1_ref[...] * (1.0 + scale1) + shift1
        return h.astype(BF16)

    def project(i, hb):
        rows = slice(i * sub, (i + 1) * sub)

        def proj(off, width):
            return _dot(hb, w_ref[:, off:off + width])

        lr = proj(OFF_LR, 2 * GLA_GATE_RANK).astype(BF16)

        qn = _half_lane_rms(proj(OFF_NA_Q, NA_WIDTH), NORM_EPS) * qg_ref[...]
        q_ref[0, rows, :] = (qn * (NA_HEAD_DIM ** -0.5 * LOG2_E)).astype(BF16)
        kn = _half_lane_rms(proj(OFF_NA_K, NA_WIDTH), NORM_EPS) * kg_ref[...]
        k_ref[0, rows, :] = kn.astype(BF16)
        vn = proj(OFF_NA_V, NA_WIDTH)
        v_ref[0, rows, :] = vn.astype(BF16)
        if kv_seq:
            piece = min(sub, kv_seq)
            for j in range(sub // piece):
                first = i * sub + j * piece
                seq, off = (first // kv_seq, first % kv_seq) if tm >= kv_seq else (0, first)
                for hh in range(NA_HEADS):
                    cols = slice(hh * NA_HEAD_DIM, (hh + 1) * NA_HEAD_DIM)
                    kf_ref[seq, 0, hh, off:off + piece, :] = kn[j * piece:(j + 1) * piece, cols]
                    vf_ref[seq, 0, hh, off:off + piece, :] = vn[j * piece:(j + 1) * piece, cols]

        z = _dot(lr, wg_ref[...]) + bg_ref[...]
        log_sig = jnp.minimum(z, 0.0) - jnp.log1p(jnp.exp(-jnp.abs(z)))
        la_ref[0, rows, :] = log_sig * (LOG2_E / GLA_GATE_TAU)

        lqk = proj(OFF_LA_Q, 2 * GLA_KEY_WIDTH)
        lq = lqk[:, :GLA_KEY_WIDTH] * (GLA_DK ** -0.5)
        lk = lqk[:, GLA_KEY_WIDTH:]
        if rope:
            lane = lax.broadcasted_iota(jnp.int32, (1, LANES), 1)
            first = (lane % 16) < 8
            cos = cos_ref[rows, :]
            sin = sin_ref[rows, :]

            def rot(t):
                partner = jnp.where(first, pltpu.roll(t, LANES - 8, axis=1), pltpu.roll(t, 8, axis=1))
                return t * cos + partner * sin

            lq = rot(lq)
            lk = rot(lk)
        lq_ref[0, rows, :] = lq
        lk_ref[0, rows, :] = lk
        u_ref[0, rows, :] = proj(OFF_POOL, POOL_WIDTH)
        lv_ref[0, rows, :] = proj(OFF_LA_V, GLA_WIDTH)
        lg_ref[0, rows, :] = proj(OFF_LA_G, GLA_WIDTH)

    hb_next = normed(0)
    for i in range(tm // sub):
        hb = hb_next
        hb_next = normed(i + 1) if (i + 1) * sub < tm else None
        project(i, hb)


def _in_proj(x, mods_l, row_of_batch, lw, rope_tabs, kv_out):
    B, L, _ = x.shape
    tm = min(IN_ROW_TILE, L)
    rope = rope_tabs is not None
    const = lambda b, i: (0, 0)
    tile = lambda b, i: (b, i, 0)

    in_specs = [
        pl.BlockSpec((1, tm, D_MODEL), tile),
        pl.BlockSpec((1, N_MODS, D_MODEL), lambda b, i: (row_of_batch(b), 0, 0)),
        pl.BlockSpec((1, D_MODEL), const),
        pl.BlockSpec((D_MODEL, IN_WIDTH), const),
        pl.BlockSpec((1, NA_WIDTH), const),
        pl.BlockSpec((1, NA_WIDTH), const),
        pl.BlockSpec((2 * GLA_GATE_RANK, 2 * GLA_KEY_WIDTH), const),
        pl.BlockSpec((1, 2 * GLA_KEY_WIDTH), const),
    ]
    args = [x, mods_l, lw["norm1_gain"], lw["w_in"], lw["q_gain"], lw["k_gain"], lw["w_gate"], lw["b_gate"]]
    if rope:
        in_specs += [pl.BlockSpec((tm, GLA_KEY_WIDTH), lambda b, i: (i, 0))] * 2
        args += list(rope_tabs)

    widths = [(POOL_WIDTH, F32), (NA_WIDTH, BF16), (NA_WIDTH, BF16), (NA_WIDTH, BF16),
              (GLA_KEY_WIDTH, F32), (GLA_KEY_WIDTH, F32), (GLA_WIDTH, F32), (GLA_WIDTH, F32),
              (2 * GLA_KEY_WIDTH, F32)]
    out_specs = [pl.BlockSpec((1, tm, w), tile) for w, _ in widths]
    out_shape = [jax.ShapeDtypeStruct((B, L, w), dt) for w, dt in widths]
    aliases = {}
    kv_seq = None
    if kv_out is not None:
        k_buf, v_buf, layer, kv_seq = kv_out
        assert B == 1 and (tm % kv_seq == 0 or kv_seq % tm == 0)
        if tm >= kv_seq:
            kv_spec = pl.BlockSpec((tm // kv_seq, 1, NA_HEADS, kv_seq, NA_HEAD_DIM), lambda b, i: (i, layer, 0, 0, 0))
        else:
            per_seq = kv_seq // tm
            kv_spec = pl.BlockSpec((1, 1, NA_HEADS, tm, NA_HEAD_DIM),
                                   lambda b, i: (i // per_seq, layer, 0, i % per_seq, 0))
        for buf in (k_buf, v_buf):
            aliases[len(args)] = len(out_specs)
            in_specs.append(pl.BlockSpec(memory_space=pl.ANY))
            args.append(buf)
            out_specs.append(kv_spec)
            out_shape.append(jax.ShapeDtypeStruct(buf.shape, buf.dtype))

    return pl.pallas_call(
        functools.partial(_inproj_kernel, rope=rope, kv_seq=kv_seq),
        grid=(B, L // tm),
        in_specs=in_specs,
        out_specs=out_specs,
        out_shape=out_shape,
        input_output_aliases=aliases,
        compiler_params=pltpu.CompilerParams(vmem_limit_bytes=VMEM_LIMIT),
        name="in_proj",
    )(*args)


def _pool_kernel(u_ref, w_ref, sc_ref, o_ref, pad_ref, s2_ref, s4_ref, s8_ref, *, L, tp):
    P = POOL_PAD
    zeros = jnp.zeros((P, POOL_WIDTH), F32)
    pad_ref[0:P, :] = zeros
    pad_ref[P + L:P + L + P, :] = zeros
    pad_ref[P:P + L, :] = u_ref[0]
    narrow = slice(0, LANES)
    wide = slice(LANES, 2 * LANES)
    lane = lax.broadcasted_iota(jnp.int32, (1, LANES), 1)
    low = lane < POOL_GROUP_DIM

    def chunks(level):
        lo, hi = 8 * level, L + 2 * P - 8 * level
        return [(a, min(a + tp, hi)) for a in range(lo, hi, tp)]

    for a, b in chunks(1):
        s2_ref[a:b, :] = pad_ref[a - 1:b - 1, wide] + pad_ref[a:b, wide]
    for a, b in chunks(2):
        s4_ref[a:b, :] = s2_ref[a - 1:b - 1, :] + s2_ref[a + 1:b + 1, :]
    for a, b in chunks(3):
        s8_ref[a:b, :] = s4_ref[a - 2:b - 2, :] + s4_ref[a + 2:b + 2, :]

    for base in range(0, L, tp):
        p0 = P + base

        def rows(ref, off, cols):
            return ref[p0 + off:p0 + off + tp, cols]

        u_a = rows(pad_ref, 0, narrow)
        w2 = rows(pad_ref, -1, narrow) + u_a
        w4 = w2 + rows(pad_ref, -2, narrow) + rows(pad_ref, 1, narrow)
        u_b = rows(pad_ref, 0, wide)
        w8 = rows(s8_ref, 0, slice(None))
        w16 = rows(s8_ref, -4, slice(None)) + rows(s8_ref, 4, slice(None))

        if base < POOL_HALO or base + tp > L - POOL_HALO:
            t = base + lax.broadcasted_iota(jnp.int32, (tp, LANES), 0)

            def mean(total, win):
                count = jnp.minimum(t + win // 2, L) - jnp.maximum(t - win // 2, 0)
                return total / count.astype(F32)
        else:
            def mean(total, win):
                return total * (1.0 / win)

        mean_a = jnp.where(low, mean(w2, 2), mean(w4, 4))
        mean_b = jnp.where(low, mean(w8, 8), mean(w16, 16))
        d = jnp.concatenate([mean_a - u_a, mean_b - u_b], axis=-1).astype(BF16)
        y = _dot(d, w_ref[...]) * sc_ref[...]
        o_ref[0, base:base + tp, :] = y.astype(BF16)


def _pool(u, lw):
    B, L, _ = u.shape
    tp = min(POOL_TILE, L)
    return pl.pallas_call(
        functools.partial(_pool_kernel, L=L, tp=tp),
        grid=(B,),
        in_specs=[
            pl.BlockSpec((1, L, POOL_WIDTH), lambda b: (b, 0, 0)),
            pl.BlockSpec((POOL_WIDTH, POOL_WIDTH), lambda b: (0, 0)),
            pl.BlockSpec((1, POOL_WIDTH), lambda b: (0, 0)),
        ],
        out_specs=pl.BlockSpec((1, L, POOL_WIDTH), lambda b: (b, 0, 0)),
        out_shape=jax.ShapeDtypeStruct((B, L, POOL_WIDTH), BF16),
        scratch_shapes=[pltpu.VMEM((L + 2 * POOL_PAD, POOL_WIDTH), F32)]
                       + [pltpu.VMEM((L + 2 * POOL_PAD, LANES), F32)] * 3,
        compiler_params=pltpu.CompilerParams(vmem_limit_bytes=VMEM_LIMIT),
        name="pool",
    )(u, lw["w_pool_bd"], lw["pool_scale"])


def _ctx_attn_kernel(q_ref, k_ref, v_ref, o_ref):
    L = q_ref.shape[1]
    lane = lax.broadcasted_iota(jnp.int32, (1, LANES), 1)
    even = lane < NA_HEAD_DIM
    zero = jnp.zeros((), BF16)

    def scores(j):
        cols = slice(j * LANES, (j + 1) * LANES)
        qp = q_ref[0, :, cols]
        q2 = jnp.concatenate([jnp.where(even, qp, zero), jnp.where(even, zero, qp)], axis=0)
        return _dot_nt(q2, k_ref[0, :, cols])

    def finish(j, s):
        cols = slice(j * LANES, (j + 1) * LANES)
        p = jnp.exp2(s - jnp.max(s, axis=-1, keepdims=True))
        denom = jnp.sum(p, axis=-1, keepdims=True)
        o2 = _dot(p.astype(BF16), v_ref[0, :, cols]) / denom
        o_ref[0, :, cols] = jnp.where(even, o2[:L], o2[L:]).astype(BF16)

    n_pairs = NA_HEADS // 2
    s_next = scores(0)
    for j in range(n_pairs):
        s_cur = s_next
        s_next = scores(j + 1) if j + 1 < n_pairs else None
        finish(j, s_cur)


def _ctx_attn(q, k, v):
    B, L, _ = q.shape
    spec = pl.BlockSpec((1, L, NA_WIDTH), lambda b: (b, 0, 0))
    return pl.pallas_call(
        _ctx_attn_kernel,
        grid=(B,),
        in_specs=[spec, spec, spec],
        out_specs=spec,
        out_shape=jax.ShapeDtypeStruct((B, L, NA_WIDTH), BF16),
        name="ctx_attn",
    )(q, k, v)


def _band_start(r, n_rows):
    return jnp.clip(r - NA_WIN_ROWS // 2, 0, n_rows - NA_WIN_ROWS)


def _lat_attn_kernel(q_ref, k_ref, v_ref, kc_ref, vc_ref, bias_ref, o_ref, *, n_rows):
    band = NA_WIN_ROWS * GRID_W
    lane = lax.broadcasted_iota(jnp.int32, (1, LANES), 1)
    even = lane < NA_HEAD_DIM
    zero = jnp.zeros((), BF16)

    def band_of(rr):
        r = pl.program_id(1) * NA_ROWS_PER_STEP + rr
        first = _band_start(r, n_rows)
        return pl.multiple_of(first * GRID_W, GRID_W), NA_WIN_ROWS - 1 - (r - first)

    def scores(rr, j):
        start, tile0 = band_of(rr)
        cols = slice(j * LANES, (j + 1) * LANES)
        qp = q_ref[0, rr * GRID_W:(rr + 1) * GRID_W, cols]
        q2 = jnp.concatenate([jnp.where(even, qp, zero), jnp.where(even, zero, qp)], axis=0)
        bias = jnp.concatenate(
            [jnp.concatenate([bias_ref[2 * j + hh, tile0 + 2 * ii] for ii in range(NA_WIN_ROWS // 2)], axis=-1)
             for hh in range(2)], axis=0)
        return _dot_nt(q2, k_ref[0, pl.ds(start, band), cols]) + bias, _dot_nt(q2, kc_ref[0, :, cols])

    def softmax(s_loc, s_ctx):
        m = jnp.maximum(jnp.max(s_loc, axis=-1, keepdims=True), jnp.max(s_ctx, axis=-1, keepdims=True))
        p_loc = jnp.exp2(s_loc - m)
        p_ctx = jnp.exp2(s_ctx - m)
        denom = jnp.sum(p_loc, axis=-1, keepdims=True) + jnp.sum(p_ctx, axis=-1, keepdims=True)
        return p_loc.astype(BF16), p_ctx.astype(BF16), denom

    def values(rr, j, p_loc, p_ctx, denom):
        start, _ = band_of(rr)
        cols = slice(j * LANES, (j + 1) * LANES)
        o2 = (_dot(p_loc, v_ref[0, pl.ds(start, band), cols]) + _dot(p_ctx, vc_ref[0, :, cols])) / denom
        o_ref[0, rr * GRID_W:(rr + 1) * GRID_W, cols] = jnp.where(even, o2[:GRID_W], o2[GRID_W:]).astype(BF16)

    chains = [(rr, j) for rr in range(NA_ROWS_PER_STEP) for j in range(NA_HEADS // 2)]
    s_next = scores(*chains[0])
    p_prev = None
    for i, chain in enumerate(chains):
        s_cur = s_next
        s_next = scores(*chains[i + 1]) if i + 1 < len(chains) else None
        p_cur = softmax(*s_cur)
        if p_prev is not None:
            values(*chains[i - 1], *p_prev)
        p_prev = p_cur
    values(*chains[-1], *p_prev)


def _lat_attn(q, k, v, kc, vc, bias_tiles):
    B, L, _ = q.shape
    n_rows = L // GRID_W
    assert n_rows >= NA_WIN_ROWS and n_rows % NA_ROWS_PER_STEP == 0
    past = kc.shape[1]
    whole = pl.BlockSpec((1, L, NA_WIDTH), lambda b, r: (b, 0, 0))
    rows = pl.BlockSpec((1, NA_ROWS_PER_STEP * GRID_W, NA_WIDTH), lambda b, r: (b, r, 0))
    ctx = pl.BlockSpec((1, past, NA_WIDTH), lambda b, r: (b, 0, 0))
    bias = pl.BlockSpec(bias_tiles.shape, lambda b, r: (0, 0, 0, 0))
    return pl.pallas_call(
        functools.partial(_lat_attn_kernel, n_rows=n_rows),
        grid=(B, n_rows // NA_ROWS_PER_STEP),
        in_specs=[rows, whole, whole, ctx, ctx, bias],
        out_specs=rows,
        out_shape=jax.ShapeDtypeStruct((B, L, NA_WIDTH), BF16),
        compiler_params=pltpu.CompilerParams(vmem_limit_bytes=VMEM_LIMIT),
        name="lat_attn",
    )(q, k, v, kc, vc, bias_tiles)


def _bias_tile_constants():
    n_dc = 2 * NA_WIN_COLS
    w = np.arange(GRID_W)[:, None]
    cc = np.arange(2 * GRID_W)[None, :]
    c = cc % GRID_W
    cs = np.clip(w - NA_WIN_COLS // 2, 0, GRID_W - NA_WIN_COLS)
    valid = (c >= cs) & (c < cs + NA_WIN_COLS)
    k_idx = (cc // GRID_W) * n_dc + (c - w + NA_WIN_COLS - 1)
    onehot = (np.arange(2 * n_dc)[:, None, None] == k_idx[None]) & valid[None]
    onehot = onehot.reshape(2 * n_dc, GRID_W * 2 * GRID_W)
    mask = np.where(valid, 0.0, MASK_VALUE).reshape(1, GRID_W * 2 * GRID_W)
    return jnp.asarray(onehot, BF16), jnp.asarray(mask, F32)


def _bias_tiles_kernel(rb_ref, oh_ref, mask_ref, o_ref):
    hi, mid, lo = _split3(rb_ref[...])
    oh = oh_ref[...]
    o_ref[...] = (_dot(hi, oh) + _dot(mid, oh) + _dot(lo, oh)) * LOG2_E + mask_ref[...]


def _bias_tiles(rel_bias_l):
    n_dr = 2 * NA_WIN_ROWS - 1
    padded = jnp.pad(rel_bias_l, ((0, 0), (0, 0), (0, 1)))
    pairs = jnp.concatenate([padded[:, :-1], padded[:, 1:]], axis=-1)
    pairs = pairs.reshape(NA_HEADS * (n_dr - 1), 4 * NA_WIN_COLS)
    onehot, mask = _bias_tile_constants()
    full = lambda shape: pl.BlockSpec(shape, lambda: (0,) * len(shape))
    out = pl.pallas_call(
        _bias_tiles_kernel,
        in_specs=[full(pairs.shape), full(onehot.shape), full(mask.shape)],
        out_specs=full((pairs.shape[0], onehot.shape[1])),
        out_shape=jax.ShapeDtypeStruct((pairs.shape[0], onehot.shape[1]), F32),
        name="bias_tiles",
    )(pairs, onehot, mask)
    return out.reshape(NA_HEADS, n_dr - 1, GRID_W, 2 * GRID_W)


def _gla_block(q_ref, k_ref, v_ref, b_ref, st_ref, o_ref, expand, head_mask, bi, r0, reverse):
    half = GLA_BLOCK // 2
    rows = pl.ds(r0, GLA_BLOCK)
    qb = q_ref[bi, rows, :]
    kb = k_ref[bi, rows, :]
    vb = v_ref[bi, rows, :]
    bb = b_ref[bi, rows, :]
    end_row = 0 if reverse else GLA_BLOCK - 1
    b_end = bb[end_row:end_row + 1, :]
    st = st_ref[bi]

    o_inter = _dot_nt((qb * jnp.exp2(bb)).astype(BF16), st.astype(BF16))

    t_in = lax.broadcasted_iota(jnp.int32, (half, 1), 0)
    q_half = (qb[:half], qb[half:])
    b_half = (bb[:half], bb[half:])
    pieces, owners = [], []
    for s in range(GLA_BLOCK):
        hs = s // half
        k_s = kb[s:s + 1, :]
        b_s = bb[s:s + 1, :]
        for ht in range(2):
            if (ht > hs) if reverse else (ht < hs):
                continue
            decay = jnp.exp2(b_half[ht] - b_s)
            if ht == hs:
                seen = (t_in <= s - hs * half) if reverse else (t_in >= s - hs * half)
                decay = jnp.where(seen, decay, 0.0)
            pieces.append(q_half[ht] * k_s * decay)
            owners.append((ht, s))
    a_exp = _dot(jnp.concatenate(pieces, axis=0).astype(BF16), expand)
    o_intra = [jnp.zeros((half, GLA_WIDTH), F32), jnp.zeros((half, GLA_WIDTH), F32)]
    for i, (ht, s) in enumerate(owners):
        o_intra[ht] = o_intra[ht] + a_exp[i * half:(i + 1) * half, :] * vb[s:s + 1, :]
    o_ref[bi, rows, :] = o_inter + jnp.concatenate(o_intra, axis=0)

    k_hat = (kb * jnp.exp2(b_end - bb)).astype(BF16)
    upd = _dot_tn(vb.astype(BF16), k_hat)
    st_ref[bi] = st * jnp.exp2(b_end) + upd * head_mask


def _gla_kernel(qf_ref, kf_ref, vf_ref, laf_ref, qb_ref, kb_ref, vb_ref, lab_ref, s0f_ref, s0b_ref,
                trif_ref, trib_ref, e_ref, of_ref, ob_ref, sff_ref, sfb_ref,
                stf_ref, stb_ref, bf_ref, bb_ref, *, tm, nb):
    n = pl.program_id(1)
    n_blocks = tm // GLA_BLOCK

    @pl.when(n == 0)
    def _():
        stf_ref[...] = s0f_ref[...]
        stb_ref[...] = s0b_ref[...]

    for la_ref, tri_ref, b_ref in ((laf_ref, trif_ref, bf_ref), (lab_ref, trib_ref, bb_ref)):
        tri = tri_ref[...]
        for bi in range(nb):
            g_hi, g_mid, _ = _split3(la_ref[bi])
            sums = _dot(tri, jnp.concatenate([g_hi, g_mid], axis=-1))
            b_ref[bi] = sums[:, :GLA_KEY_WIDTH] + sums[:, GLA_KEY_WIDTH:]

    expand = e_ref[...]
    head_v = lax.broadcasted_iota(jnp.int32, (GLA_WIDTH, GLA_KEY_WIDTH), 0) // GLA_DV
    head_k = lax.broadcasted_iota(jnp.int32, (GLA_WIDTH, GLA_KEY_WIDTH), 1) // GLA_DK
    head_mask = jnp.where(head_v == head_k, 1.0, 0.0)

    def block(j, carry):
        r_f = pl.multiple_of(j * GLA_BLOCK, GLA_BLOCK)
        r_b = pl.multiple_of((n_blocks - 1 - j) * GLA_BLOCK, GLA_BLOCK)
        for bi in range(nb):
            _gla_block(qf_ref, kf_ref, vf_ref, bf_ref, stf_ref, of_ref, expand, head_mask, bi, r_f, reverse=False)
            _gla_block(qb_ref, kb_ref, vb_ref, bb_ref, stb_ref, ob_ref, expand, head_mask, bi, r_b, reverse=True)
        return carry

    lax.fori_loop(0, n_blocks, block, 0)

    @pl.when(n == pl.num_programs(1) - 1)
    def _():
        sff_ref[...] = stf_ref[...]
        sfb_ref[...] = stb_ref[...]


def _block_tri(tm, reverse):
    t = np.arange(tm)[:, None]
    s = np.arange(tm)[None, :]
    same = (t // GLA_BLOCK) == (s // GLA_BLOCK)
    return jnp.asarray(same & ((s >= t) if reverse else (s <= t)), BF16)


def _gla_scan(q, k, v, la, s0f_t, s0b_t, expand):
    B, L, _ = q.shape
    tm = min(GLA_TILE, L)
    n_tiles = L // tm
    nb = math.gcd(B, GLA_BATCH_PER_STEP)
    fwd = lambda b, n: (b, n, 0)
    bwd = lambda b, n: (b, n_tiles - 1 - n, 0)
    state = pl.BlockSpec((nb, GLA_WIDTH, GLA_KEY_WIDTH), lambda b, n: (b, 0, 0))
    const = lambda shape: pl.BlockSpec(shape, lambda b, n: (0, 0))

    def operands(tmap, la_col):
        return [pl.BlockSpec((nb, tm, GLA_KEY_WIDTH), tmap), pl.BlockSpec((nb, tm, GLA_KEY_WIDTH), tmap),
                pl.BlockSpec((nb, tm, GLA_WIDTH), tmap),
                pl.BlockSpec((nb, tm, GLA_KEY_WIDTH), lambda b, n: tmap(b, n)[:2] + (la_col,))]

    return pl.pallas_call(
        functools.partial(_gla_kernel, tm=tm, nb=nb),
        grid=(B // nb, n_tiles),
        in_specs=operands(fwd, 0) + operands(bwd, 1) + [state, state, const((tm, tm)), const((tm, tm)),
                                                        const((GLA_KEY_WIDTH, GLA_WIDTH))],
        out_specs=[pl.BlockSpec((nb, tm, GLA_WIDTH), fwd), pl.BlockSpec((nb, tm, GLA_WIDTH), bwd), state, state],
        out_shape=[jax.ShapeDtypeStruct((B, L, GLA_WIDTH), F32), jax.ShapeDtypeStruct((B, L, GLA_WIDTH), F32),
                   jax.ShapeDtypeStruct((B, GLA_WIDTH, GLA_KEY_WIDTH), F32),
                   jax.ShapeDtypeStruct((B, GLA_WIDTH, GLA_KEY_WIDTH), F32)],
        scratch_shapes=[pltpu.VMEM((nb, GLA_WIDTH, GLA_KEY_WIDTH), F32), pltpu.VMEM((nb, GLA_WIDTH, GLA_KEY_WIDTH), F32),
                        pltpu.VMEM((nb, tm, GLA_KEY_WIDTH), F32), pltpu.VMEM((nb, tm, GLA_KEY_WIDTH), F32)],
        compiler_params=pltpu.CompilerParams(vmem_limit_bytes=VMEM_LIMIT),
        name="gla",
    )(q, k, v, la, q, k, v, la, s0f_t, s0b_t, _block_tri(tm, False), _block_tri(tm, True), expand)


def _state_to_kernel(s):
    B = s.shape[0]
    same_head = np.eye(GLA_HEADS, dtype=bool)[None, :, None, :, None]
    st = jnp.where(same_head, s.transpose(0, 1, 3, 2)[:, :, :, None, :], 0.0)
    return st.reshape(B, GLA_WIDTH, GLA_KEY_WIDTH)


def _state_from_kernel(st):
    B = st.shape[0]
    blocks = st.reshape(B, GLA_HEADS, GLA_DV, GLA_HEADS, GLA_DK)
    diag = jnp.stack([blocks[:, h, :, h, :] for h in range(GLA_HEADS)], axis=1)
    return diag.transpose(0, 1, 3, 2)


def _post_kernel(x_ref, m_ref, yp_ref, yn_ref, of_ref, ob_ref, lg_ref, gg_ref, g2_ref,
                 wo_ref, wi_ref, wf_ref, o_ref, acc_ref):
    gate1 = m_ref[0, 2:3, :]
    shift2 = m_ref[0, 3:4, :]
    scale2 = m_ref[0, 4:5, :]
    gate2 = m_ref[0, 5:6, :]

    o_la = _half_lane_rms(of_ref[0] + ob_ref[0], NORM_EPS) * gg_ref[...]
    y_la = (o_la * _silu(lg_ref[0])).astype(BF16)
    mixed = (_dot(yp_ref[0], wo_ref[0:POOL_WIDTH, :])
             + _dot(yn_ref[0], wo_ref[POOL_WIDTH:POOL_WIDTH + NA_WIDTH, :])
             + _dot(y_la, wo_ref[POOL_WIDTH + NA_WIDTH:D_MODEL, :]))
    x1 = x_ref[0] + gate1 * mixed

    h2 = (_row_rms(x1, NORM_EPS) * g2_ref[...] * (1.0 + scale2) + shift2).astype(BF16)
    for c0 in range(0, D_FF, FF_CHUNK):
        hg = _dot(h2, wi_ref[:, c0:c0 + FF_CHUNK])
        hu = _dot(h2, wi_ref[:, D_FF + c0:D_FF + c0 + FF_CHUNK])
        acc_ref[:, c0:c0 + FF_CHUNK] = (_silu(hg) * hu).astype(BF16)
    o_ref[0] = x1 + gate2 * _dot(acc_ref[...], wf_ref[...])


def _post(x, mods_l, row_of_batch, y_pool, y_na, o_f, o_b, lg, lw):
    B, L, _ = x.shape
    tm = min(ROW_TILE, L)
    const = lambda b, i: (0, 0)
    tile = lambda b, i: (b, i, 0)

    def resident(shape):
        return pl.BlockSpec(shape, const, pipeline_mode=pl.Buffered(1))

    return pl.pallas_call(
        _post_kernel,
        grid=(B, L // tm),
        in_specs=[
            pl.BlockSpec((1, tm, D_MODEL), tile),
            pl.BlockSpec((1, N_MODS, D_MODEL), lambda b, i: (row_of_batch(b), 0, 0)),
            pl.BlockSpec((1, tm, POOL_WIDTH), tile),
            pl.BlockSpec((1, tm, NA_WIDTH), tile),
            pl.BlockSpec((1, tm, GLA_WIDTH), tile),
            pl.BlockSpec((1, tm, GLA_WIDTH), tile),
            pl.BlockSpec((1, tm, GLA_WIDTH), tile),
            pl.BlockSpec((1, GLA_WIDTH), const),
            pl.BlockSpec((1, D_MODEL), const),
            resident((D_MODEL, D_MODEL)),
            resident((D_MODEL, 2 * D_FF)),
            resident((D_FF, D_MODEL)),
        ],
        out_specs=pl.BlockSpec((1, tm, D_MODEL), tile),
        out_shape=jax.ShapeDtypeStruct((B, L, D_MODEL), F32),
        scratch_shapes=[pltpu.VMEM((tm, D_FF), BF16)],
        compiler_params=pltpu.CompilerParams(vmem_limit_bytes=VMEM_LIMIT),
        name="post",
    )(x, mods_l, y_pool, y_na, o_f, o_b, lg, lw["gla_gain"], lw["norm2_gain"],
      lw["w_out"], lw["w_ffn_in"], lw["w_ffn_out"])


def _rope_tables(L):
    t = jnp.arange(L)
    row = (t // GRID_W).astype(F32)
    col = (t % GRID_W).astype(F32)
    half = GLA_DK // 2
    inv_freq = ROPE_THETA ** (-jnp.arange(0, half, 2, dtype=F32) / half)
    ang_r = row[:, None] * inv_freq
    ang_c = col[:, None] * inv_freq
    sign = jnp.concatenate([-jnp.ones((half // 2,), F32), jnp.ones((half // 2,), F32)])

    def lanes(fn, signed):
        per_axis = []
        for ang in (ang_r, ang_c):
            v = jnp.concatenate([fn(ang), fn(ang)], axis=-1)
            per_axis.append(v * sign if signed else v)
        return jnp.tile(jnp.concatenate(per_axis, axis=-1), (1, GLA_HEADS))

    return lanes(jnp.cos, False), lanes(jnp.sin, True)


def _layer_weights(l, w):
    tile_heads = lambda g, n: jnp.tile(g, n)[None, :]
    return {
        "norm1_gain": w["norm1_gain"][l][None, :],
        "norm2_gain": w["norm2_gain"][l][None, :],
        "w_in": w["w_in"][l].astype(BF16),
        "q_gain": tile_heads(w["q_norm_gain"][l], NA_HEADS),
        "k_gain": tile_heads(w["k_norm_gain"][l], NA_HEADS),
        "w_gate": jax.scipy.linalg.block_diag(w["w_gate_f"][l], w["w_gate_b"][l]).astype(BF16),
        "b_gate": jnp.concatenate([w["b_gate_f"][l], w["b_gate_b"][l]])[None, :],
        "w_pool_bd": jax.scipy.linalg.block_diag(*[w["w_pool"][l, g] for g in range(len(POOL_WINDOWS))]).astype(BF16),
        "pool_scale": w["pool_scale"][l][None, :],
        "gla_gain": tile_heads(w["gla_norm_gain"][l], GLA_HEADS),
        "w_out": w["w_out"][l].astype(BF16),
        "w_ffn_in": w["w_ffn_in"][l].astype(BF16),
        "w_ffn_out": w["w_ffn_out"][l].astype(BF16),
    }


def _trunk_layer(x, mods_l, row_of_batch, lw, expand, s0_f, s0_b, latent, kv_out=None):
    is_ctx = latent is None
    B, L, _ = x.shape
    per_token = (lambda a: a.reshape(1, B * L, a.shape[-1])) if is_ctx else (lambda a: a)
    per_seq = lambda a: a.reshape(B, L, a.shape[-1])
    outs = _in_proj(per_token(x), mods_l, row_of_batch, lw, None if is_ctx else latent[0],
                    kv_out + (L,) if is_ctx else None)
    u, q, k, v, lq, lk, lv, lg, la = [per_seq(a) for a in outs[:9]]
    y_pool = _pool(u, lw)
    if is_ctx:
        y_na = _ctx_attn(q, k, v)
    else:
        y_na = _lat_attn(q, k, v, latent[1], latent[2], latent[3])
    o_f, o_b, s_f, s_b = _gla_scan(lq, lk, lv, la, s0_f, s0_b, expand)
    x = per_seq(_post(per_token(x), mods_l, row_of_batch,
                      *[per_token(a) for a in (y_pool, y_na, o_f, o_b, lg)], lw))
    return x, s_f, s_b, tuple(outs[9:])


def kernel(x_prompt, x_sample, c, cache_na_k, cache_na_v, state_gla_fwd, state_gla_bwd, c_ctx, w_ada, b_ada, norm1_gain, norm2_gain, w_in, w_pool, pool_scale, q_norm_gain, k_norm_gain, rel_bias, w_gate_f, b_gate_f, w_gate_b, b_gate_b, gla_norm_gain, w_out, w_ffn_in, w_ffn_out):
    weights = dict(norm1_gain=norm1_gain, norm2_gain=norm2_gain, w_in=w_in, w_pool=w_pool, pool_scale=pool_scale,
                   q_norm_gain=q_norm_gain, k_norm_gain=k_norm_gain, w_gate_f=w_gate_f, b_gate_f=b_gate_f,
                   w_gate_b=w_gate_b, b_gate_b=b_gate_b, gla_norm_gain=gla_norm_gain, w_out=w_out,
                   w_ffn_in=w_ffn_in, w_ffn_out=w_ffn_out)
    B_ctx, L_ctx, _ = x_prompt.shape
    B_lat, L_lat, _ = x_sample.shape
    assert 1 + B_lat <= COND_ROWS

    conds = jnp.zeros((COND_ROWS, D_MODEL), F32).at[0].set(c_ctx).at[1:1 + B_lat].set(c)
    mods = _ada(conds, w_ada, b_ada).reshape(DEPTH, COND_ROWS, N_MODS, D_MODEL)
    lws = [_layer_weights(l, weights) for l in range(DEPTH)]
    head_of_k = np.arange(GLA_KEY_WIDTH) // GLA_DK
    head_of_v = np.arange(GLA_WIDTH) // GLA_DV
    expand = jnp.asarray(head_of_k[:, None] == head_of_v[None, :], BF16)

    xp = x_prompt
    zero_state = jnp.zeros((B_ctx, GLA_WIDTH, GLA_KEY_WIDTH), F32)
    new_k = jnp.zeros((B_ctx, DEPTH, NA_HEADS, L_ctx, NA_HEAD_DIM), F32)
    new_v = jnp.zeros((B_ctx, DEPTH, NA_HEADS, L_ctx, NA_HEAD_DIM), F32)
    sfs, sbs = [], []
    for l in range(DEPTH):
        xp, s_f, s_b, (new_k, new_v) = _trunk_layer(xp, mods[l], lambda b: 0, lws[l], expand,
                                                     zero_state, zero_state, None, (new_k, new_v, l))
        sfs.append(_state_from_kernel(s_f))
        sbs.append(_state_from_kernel(s_b))

    xs = x_sample
    rope_tabs = _rope_tables(L_lat)
    from_heads = lambda a: a.transpose(0, 2, 1, 3).reshape(B_lat, a.shape[2], NA_WIDTH).astype(BF16)
    for l in range(DEPTH):
        latent = (rope_tabs, from_heads(cache_na_k[:, l]), from_heads(cache_na_v[:, l]), _bias_tiles(rel_bias[l]))
        xs, _, _, _ = _trunk_layer(xs, mods[l], lambda b: b + 1, lws[l], expand,
                                   _state_to_kernel(state_gla_fwd[:, l]), _state_to_kernel(state_gla_bwd[:, l]),
                                   latent)

    return (xp, xs, new_k, new_v, jnp.stack(sfs, axis=1), jnp.stack(sbs, axis=1))
```

```python
import functools
import math

import numpy as np
import jax
import jax.numpy as jnp
from jax import lax
from jax.experimental import pallas as pl
from jax.experimental.pallas import tpu as pltpu

F32 = jnp.float32
BF16 = jnp.bfloat16

D_MODEL = 1024
DEPTH = 2
GRID_W = 64
POOL_WIDTH = 256
POOL_GROUP_DIM = 64
POOL_WINDOWS = (2, 4, 8, 16)
POOL_HALO = max(POOL_WINDOWS) // 2
POOL_PAD = 8 * len(POOL_WINDOWS)
NA_HEADS = 8
NA_HEAD_DIM = 64
NA_WIDTH = NA_HEADS * NA_HEAD_DIM
NA_WIN_ROWS = 8
NA_WIN_COLS = 16
NA_ROWS_PER_STEP = 16
GLA_HEADS = 4
GLA_DV = 64
GLA_DK = 32
GLA_WIDTH = GLA_HEADS * GLA_DV
GLA_KEY_WIDTH = GLA_HEADS * GLA_DK
GLA_GATE_RANK = 16
GLA_GATE_TAU = 16.0
GLA_BLOCK = 16
ROPE_THETA = 10000.0
D_FF = 2816
NORM_EPS = 1e-6
N_MODS = 6
COND_ROWS = 16
MASK_VALUE = -1e30
LOG2_E = 1.4426950408889634

OFF_POOL = 0
OFF_NA_Q = OFF_POOL + POOL_WIDTH
OFF_NA_K = OFF_NA_Q + NA_WIDTH
OFF_NA_V = OFF_NA_K + NA_WIDTH
OFF_LA_Q = OFF_NA_V + NA_WIDTH
OFF_LA_K = OFF_LA_Q + GLA_KEY_WIDTH
OFF_LA_V = OFF_LA_K + GLA_KEY_WIDTH
OFF_LA_G = OFF_LA_V + GLA_WIDTH
OFF_LR = OFF_LA_G + GLA_WIDTH
IN_WIDTH = OFF_LR + 2 * GLA_GATE_RANK

LANES = 128
HALF_LANES = LANES // 2
assert NA_HEAD_DIM == GLA_DV == HALF_LANES
POOL_TILE = 512
ROW_TILE = 512
IN_ROW_TILE = 1024
IN_SUB_TILE = 256
GLA_TILE = 256
GLA_BATCH_PER_STEP = 8
FF_CHUNK = 256
ADA_COL_TILE = 1024
VMEM_LIMIT = 56 * 1024 * 1024


def _dot(a, b):
    return jnp.dot(a, b, preferred_element_type=F32)


def _dot_nt(a, b):
    return lax.dot_general(a, b, (((1,), (1,)), ((), ())), preferred_element_type=F32)


def _dot_tn(a, b):
    return lax.dot_general(a, b, (((0,), (0,)), ((), ())), preferred_element_type=F32)


def _silu(x):
    return x * (1.0 / (1.0 + jnp.exp(-x)))


def _split3(x):
    hi = x.astype(BF16)
    r1 = x - hi.astype(F32)
    mid = r1.astype(BF16)
    lo = (r1 - mid.astype(F32)).astype(BF16)
    return hi, mid, lo


def _half_lane_rms(x, eps):
    lane = lax.broadcasted_iota(jnp.int32, (1, LANES), 1)
    low = lane < HALF_LANES
    cols = []
    for j in range(x.shape[-1] // LANES):
        blk = x[:, j * LANES:(j + 1) * LANES]
        sq = blk * blk
        s_lo = jnp.sum(jnp.where(low, sq, 0.0), axis=-1, keepdims=True)
        s_hi = jnp.sum(jnp.where(low, 0.0, sq), axis=-1, keepdims=True)
        r_lo = lax.rsqrt(s_lo * (1.0 / HALF_LANES) + eps)
        r_hi = lax.rsqrt(s_hi * (1.0 / HALF_LANES) + eps)
        cols.append(blk * jnp.where(low, r_lo, r_hi))
    return jnp.concatenate(cols, axis=-1)


def _row_rms(x, eps):
    return x * lax.rsqrt(jnp.mean(x * x, axis=-1, keepdims=True) + eps)


def _ada_kernel(c_ref, w_ref, b_ref, o_ref):
    s_hi, s_mid, _ = _split3(_silu(c_ref[...]))
    w = w_ref[0]
    w_hi, w_mid, _ = _split3(w)
    acc = _dot(s_hi, w_hi) + _dot(s_mid, w_hi) + _dot(s_hi, w_mid)
    o_ref[0] = acc + b_ref[0]


def _ada(conds, w_ada, b_ada):
    n_out = w_ada.shape[-1]
    return pl.pallas_call(
        _ada_kernel,
        grid=(DEPTH, n_out // ADA_COL_TILE),
        in_specs=[
            pl.BlockSpec((COND_ROWS, D_MODEL), lambda l, j: (0, 0)),
            pl.BlockSpec((1, D_MODEL, ADA_COL_TILE), lambda l, j: (l, 0, j)),
            pl.BlockSpec((1, 1, ADA_COL_TILE), lambda l, j: (l, 0, j)),
        ],
        out_specs=pl.BlockSpec((1, COND_ROWS, ADA_COL_TILE), lambda l, j: (l, 0, j)),
        out_shape=jax.ShapeDtypeStruct((DEPTH, COND_ROWS, n_out), F32),
        name="ada",
    )(conds, w_ada, b_ada.reshape(DEPTH, 1, n_out))


def _inproj_kernel(*refs, rope, kv_seq):
    x_ref, m_ref, g1_ref, w_ref, qg_ref, kg_ref, wg_ref, bg_ref = refs[:8]
    pos = 8
    if rope:
        cos_ref, sin_ref = refs[pos:pos + 2]
        pos += 2
    if kv_seq:
        pos += 2
    u_ref, q_ref, k_ref, v_ref, lq_ref, lk_ref, lv_ref, lg_ref, la_ref = refs[pos:pos + 9]
    pos += 9
    if kv_seq:
        kf_ref, vf_ref = refs[pos:pos + 2]

    shift1 = m_ref[0, 0:1, :]
    scale1 = m_ref[0, 1:2, :]
    tm = x_ref.shape[1]
    sub = min(IN_SUB_TILE, tm)

    def normed(i):
        rows = slice(i * sub, (i + 1) * sub)
        h = _row_rms(x_ref[0, rows, :], NORM_EPS) * g1_ref[...] * (1.0 + scale1) + shift1
        return h.astype(BF16)

    def project(i, hb):
        rows = slice(i * sub, (i + 1) * sub)

        def proj(off, width):
            return _dot(hb, w_ref[:, off:off + width])

        lr = proj(OFF_LR, 2 * GLA_GATE_RANK).astype(BF16)

        qn = _half_lane_rms(proj(OFF_NA_Q, NA_WIDTH), NORM_EPS) * qg_ref[...]
        q_ref[0, rows, :] = (qn * (NA_HEAD_DIM ** -0.5 * LOG2_E)).astype(BF16)
        kn = _half_lane_rms(proj(OFF_NA_K, NA_WIDTH), NORM_EPS) * kg_ref[...]
        k_ref[0, rows, :] = kn.astype(BF16)
        vn = proj(OFF_NA_V, NA_WIDTH)
        v_ref[0, rows, :] = vn.astype(BF16)
        if kv_seq:
            piece = min(sub, kv_seq)
            for j in range(sub // piece):
                first = i * sub + j * piece
                seq, off = (first // kv_seq, first % kv_seq) if tm >= kv_seq else (0, first)
                for hh in range(NA_HEADS):
                    cols = slice(hh * NA_HEAD_DIM, (hh + 1) * NA_HEAD_DIM)
                    kf_ref[seq, 0, hh, off:off + piece, :] = kn[j * piece:(j + 1) * piece, cols]
                    vf_ref[seq, 0, hh, off:off + piece, :] = vn[j * piece:(j + 1) * piece, cols]

        z = _dot(lr, wg_ref[...]) + bg_ref[...]
        log_sig = jnp.minimum(z, 0.0) - jnp.log1p(jnp.exp(-jnp.abs(z)))
        la_ref[0, rows, :] = log_sig * (LOG2_E / GLA_GATE_TAU)

        lqk = proj(OFF_LA_Q, 2 * GLA_KEY_WIDTH)
        lq = lqk[:, :GLA_KEY_WIDTH] * (GLA_DK ** -0.5)
        lk = lqk[:, GLA_KEY_WIDTH:]
        if rope:
            lane = lax.broadcasted_iota(jnp.int32, (1, LANES), 1)
            first = (lane % 16) < 8
            cos = cos_ref[rows, :]
            sin = sin_ref[rows, :]

            def rot(t):
                partner = jnp.where(first, pltpu.roll(t, LANES - 8, axis=1), pltpu.roll(t, 8, axis=1))
                return t * cos + partner * sin

            lq = rot(lq)
            lk = rot(lk)
        lq_ref[0, rows, :] = lq
        lk_ref[0, rows, :] = lk
        u_ref[0, rows, :] = proj(OFF_POOL, POOL_WIDTH)
        lv_ref[0, rows, :] = proj(OFF_LA_V, GLA_WIDTH)
        lg_ref[0, rows, :] = proj(OFF_LA_G, GLA_WIDTH)

    hb_next = normed(0)
    for i in range(tm // sub):
        hb = hb_next
        hb_next = normed(i + 1) if (i + 1) * sub < tm else None
        project(i, hb)


def _in_proj(x, mods_l, row_of_batch, lw, rope_tabs, kv_out):
    B, L, _ = x.shape
    tm = min(IN_ROW_TILE, L)
    rope = rope_tabs is not None
    const = lambda b, i: (0, 0)
    tile = lambda b, i: (b, i, 0)

    in_specs = [
        pl.BlockSpec((1, tm, D_MODEL), tile),
        pl.BlockSpec((1, N_MODS, D_MODEL), lambda b, i: (row_of_batch(b), 0, 0)),
        pl.BlockSpec((1, D_MODEL), const),
        pl.BlockSpec((D_MODEL, IN_WIDTH), const),
        pl.BlockSpec((1, NA_WIDTH), const),
        pl.BlockSpec((1, NA_WIDTH), const),
        pl.BlockSpec((2 * GLA_GATE_RANK, 2 * GLA_KEY_WIDTH), const),
        pl.BlockSpec((1, 2 * GLA_KEY_WIDTH), const),
    ]
    args = [x, mods_l, lw["norm1_gain"], lw["w_in"], lw["q_gain"], lw["k_gain"], lw["w_gate"], lw["b_gate"]]
    if rope:
        in_specs += [pl.BlockSpec((tm, GLA_KEY_WIDTH), lambda b, i: (i, 0))] * 2
        args += list(rope_tabs)

    widths = [(POOL_WIDTH, F32), (NA_WIDTH, BF16), (NA_WIDTH, BF16), (NA_WIDTH, BF16),
              (GLA_KEY_WIDTH, F32), (GLA_KEY_WIDTH, F32), (GLA_WIDTH, F32), (GLA_WIDTH, F32),
              (2 * GLA_KEY_WIDTH, F32)]
    out_specs = [pl.BlockSpec((1, tm, w), tile) for w, _ in widths]
    out_shape = [jax.ShapeDtypeStruct((B, L, w), dt) for w, dt in widths]
    aliases = {}
    kv_seq = None
    if kv_out is not None:
        k_buf, v_buf, layer, kv_seq = kv_out
        assert B == 1 and (tm % kv_seq == 0 or kv_seq % tm == 0)
        if tm >= kv_seq:
            kv_spec = pl.BlockSpec((tm // kv_seq, 1, NA_HEADS, kv_seq, NA_HEAD_DIM), lambda b, i: (i, layer, 0, 0, 0))
        else:
            per_seq = kv_seq // tm
            kv_spec = pl.BlockSpec((1, 1, NA_HEADS, tm, NA_HEAD_DIM),
                                   lambda b, i: (i // per_seq, layer, 0, i % per_seq, 0))
        for buf in (k_buf, v_buf):
            aliases[len(args)] = len(out_specs)
            in_specs.append(pl.BlockSpec(memory_space=pl.ANY))
            args.append(buf)
            out_specs.append(kv_spec)
            out_shape.append(jax.ShapeDtypeStruct(buf.shape, buf.dtype))

    return pl.pallas_call(
        functools.partial(_inproj_kernel, rope=rope, kv_seq=kv_seq),
        grid=(B, L // tm),
        in_specs=in_specs,
        out_specs=out_specs,
        out_shape=out_shape,
        input_output_aliases=aliases,
        compiler_params=pltpu.CompilerParams(vmem_limit_bytes=VMEM_LIMIT),
        name="in_proj",
    )(*args)


def _pool_kernel(u_ref, w_ref, sc_ref, o_ref, pad_ref, s2_ref, s4_ref, s8_ref, *, L, tp):
    P = POOL_PAD
    zeros = jnp.zeros((P, POOL_WIDTH), F32)
    pad_ref[0:P, :] = zeros
    pad_ref[P + L:P + L + P, :] = zeros
    pad_ref[P:P + L, :] = u_ref[0]
    narrow = slice(0, LANES)
    wide = slice(LANES, 2 * LANES)
    lane = lax.broadcasted_iota(jnp.int32, (1, LANES), 1)
    low = lane < POOL_GROUP_DIM

    def chunks(level):
        lo, hi = 8 * level, L + 2 * P - 8 * level
        return [(a, min(a + tp, hi)) for a in range(lo, hi, tp)]

    for a, b in chunks(1):
        s2_ref[a:b, :] = pad_ref[a - 1:b - 1, wide] + pad_ref[a:b, wide]
    for a, b in chunks(2):
        s4_ref[a:b, :] = s2_ref[a - 1:b - 1, :] + s2_ref[a + 1:b + 1, :]
    for a, b in chunks(3):
        s8_ref[a:b, :] = s4_ref[a - 2:b - 2, :] + s4_ref[a + 2:b + 2, :]

    for base in range(0, L, tp):
        p0 = P + base

        def rows(ref, off, cols):
            return ref[p0 + off:p0 + off + tp, cols]

        u_a = rows(pad_ref, 0, narrow)
        w2 = rows(pad_ref, -1, narrow) + u_a
        w4 = w2 + rows(pad_ref, -2, narrow) + rows(pad_ref, 1, narrow)
        u_b = rows(pad_ref, 0, wide)
        w8 = rows(s8_ref, 0, slice(None))
        w16 = rows(s8_ref, -4, slice(None)) + rows(s8_ref, 4, slice(None))

        if base < POOL_HALO or base + tp > L - POOL_HALO:
            t = base + lax.broadcasted_iota(jnp.int32, (tp, LANES), 0)

            def mean(total, win):
                count = jnp.minimum(t + win // 2, L) - jnp.maximum(t - win // 2, 0)
                return total / count.astype(F32)
        else:
            def mean(total, win):
                return total * (1.0 / win)

        mean_a = jnp.where(low, mean(w2, 2), mean(w4, 4))
        mean_b = jnp.where(low, mean(w8, 8), mean(w16, 16))
        d = jnp.concatenate([mean_a - u_a, mean_b - u_b], axis=-1).astype(BF16)
        y = _dot(d, w_ref[...]) * sc_ref[...]
        o_ref[0, base:base + tp, :] = y.astype(BF16)


def _pool(u, lw):
    B, L, _ = u.shape
    tp = min(POOL_TILE, L)
    return pl.pallas_call(
        functools.partial(_pool_kernel, L=L, tp=tp),
        grid=(B,),
        in_specs=[
            pl.BlockSpec((1, L, POOL_WIDTH), lambda b: (b, 0, 0)),
            pl.BlockSpec((POOL_WIDTH, POOL_WIDTH), lambda b: (0, 0)),
            pl.BlockSpec((1, POOL_WIDTH), lambda b: (0, 0)),
        ],
        out_specs=pl.BlockSpec((1, L, POOL_WIDTH), lambda b: (b, 0, 0)),
        out_shape=jax.ShapeDtypeStruct((B, L, POOL_WIDTH), BF16),
        scratch_shapes=[pltpu.VMEM((L + 2 * POOL_PAD, POOL_WIDTH), F32)]
                       + [pltpu.VMEM((L + 2 * POOL_PAD, LANES), F32)] * 3,
        compiler_params=pltpu.CompilerParams(vmem_limit_bytes=VMEM_LIMIT),
        name="pool",
    )(u, lw["w_pool_bd"], lw["pool_scale"])


def _ctx_attn_kernel(q_ref, k_ref, v_ref, o_ref):
    L = q_ref.shape[1]
    lane = lax.broadcasted_iota(jnp.int32, (1, LANES), 1)
    even = lane < NA_HEAD_DIM
    zero = jnp.zeros((), BF16)

    def scores(j):
        cols = slice(j * LANES, (j + 1) * LANES)
        qp = q_ref[0, :, cols]
        q2 = jnp.concatenate([jnp.where(even, qp, zero), jnp.where(even, zero, qp)], axis=0)
        return _dot_nt(q2, k_ref[0, :, cols])

    def finish(j, s):
        cols = slice(j * LANES, (j + 1) * LANES)
        p = jnp.exp2(s - jnp.max(s, axis=-1, keepdims=True))
        denom = jnp.sum(p, axis=-1, keepdims=True)
        o2 = _dot(p.astype(BF16), v_ref[0, :, cols]) / denom
        o_ref[0, :, cols] = jnp.where(even, o2[:L], o2[L:]).astype(BF16)

    n_pairs = NA_HEADS // 2
    s_next = scores(0)
    for j in range(n_pairs):
        s_cur = s_next
        s_next = scores(j + 1) if j + 1 < n_pairs else None
        finish(j, s_cur)


def _ctx_attn(q, k, v):
    B, L, _ = q.shape
    spec = pl.BlockSpec((1, L, NA_WIDTH), lambda b: (b, 0, 0))
    return pl.pallas_call(
        _ctx_attn_kernel,
        grid=(B,),
        in_specs=[spec, spec, spec],
        out_specs=spec,
        out_shape=jax.ShapeDtypeStruct((B, L, NA_WIDTH), BF16),
        name="ctx_attn",
    )(q, k, v)


def _band_start(r, n_rows):
    return jnp.clip(r - NA_WIN_ROWS // 2, 0, n_rows - NA_WIN_ROWS)


def _lat_attn_kernel(q_ref, k_ref, v_ref, kc_ref, vc_ref, bias_ref, o_ref, *, n_rows):
    band = NA_WIN_ROWS * GRID_W
    lane = lax.broadcasted_iota(jnp.int32, (1, LANES), 1)
    even = lane < NA_HEAD_DIM
    zero = jnp.zeros((), BF16)

    def band_of(rr):
        r = pl.program_id(1) * NA_ROWS_PER_STEP + rr
        first = _band_start(r, n_rows)
        return pl.multiple_of(first * GRID_W, GRID_W), NA_WIN_ROWS - 1 - (r - first)

    def scores(rr, j):
        start, tile0 = band_of(rr)
        cols = slice(j * LANES, (j + 1) * LANES)
        qp = q_ref[0, rr * GRID_W:(rr + 1) * GRID_W, cols]
        q2 = jnp.concatenate([jnp.where(even, qp, zero), jnp.where(even, zero, qp)], axis=0)
        bias = jnp.concatenate(
            [jnp.concatenate([bias_ref[2 * j + hh, tile0 + 2 * ii] for ii in range(NA_WIN_ROWS // 2)], axis=-1)
             for hh in range(2)], axis=0)
        return _dot_nt(q2, k_ref[0, pl.ds(start, band), cols]) + bias, _dot_nt(q2, kc_ref[0, :, cols])

    def softmax(s_loc, s_ctx):
        m = jnp.maximum(jnp.max(s_loc, axis=-1, keepdims=True), jnp.max(s_ctx, axis=-1, keepdims=True))
        p_loc = jnp.exp2(s_loc - m)
        p_ctx = jnp.exp2(s_ctx - m)
        denom = jnp.sum(p_loc, axis=-1, keepdims=True) + jnp.sum(p_ctx, axis=-1, keepdims=True)
        return p_loc.astype(BF16), p_ctx.astype(BF16), denom

    def values(rr, j, p_loc, p_ctx, denom):
        start, _ = band_of(rr)
        cols = slice(j * LANES, (j + 1) * LANES)
        o2 = (_dot(p_loc, v_ref[0, pl.ds(start, band), cols]) + _dot(p_ctx, vc_ref[0, :, cols])) / denom
        o_ref[0, rr * GRID_W:(rr + 1) * GRID_W, cols] = jnp.where(even, o2[:GRID_W], o2[GRID_W:]).astype(BF16)

    chains = [(rr, j) for rr in range(NA_ROWS_PER_STEP) for j in range(NA_HEADS // 2)]
    s_next = scores(*chains[0])
    p_prev = None
    for i, chain in enumerate(chains):
        s_cur = s_next
        s_next = scores(*chains[i + 1]) if i + 1 < len(chains) else None
        p_cur = softmax(*s_cur)
        if p_prev is not None:
            values(*chains[i - 1], *p_prev)
        p_prev = p_cur
    values(*chains[-1], *p_prev)


def _lat_attn(q, k, v, kc, vc, bias_tiles):
    B, L, _ = q.shape
    n_rows = L // GRID_W
    assert n_rows >= NA_WIN_ROWS and n_rows % NA_ROWS_PER_STEP == 0
    past = kc.shape[1]
    whole = pl.BlockSpec((1, L, NA_WIDTH), lambda b, r: (b, 0, 0))
    rows = pl.BlockSpec((1, NA_ROWS_PER_STEP * GRID_W, NA_WIDTH), lambda b, r: (b, r, 0))
    ctx = pl.BlockSpec((1, past, NA_WIDTH), lambda b, r: (b, 0, 0))
    bias = pl.BlockSpec(bias_tiles.shape, lambda b, r: (0, 0, 0, 0))
    return pl.pallas_call(
        functools.partial(_lat_attn_kernel, n_rows=n_rows),
        grid=(B, n_rows // NA_ROWS_PER_STEP),
        in_specs=[rows, whole, whole, ctx, ctx, bias],
        out_specs=rows,
        out_shape=jax.ShapeDtypeStruct((B, L, NA_WIDTH), BF16),
        compiler_params=pltpu.CompilerParams(vmem_limit_bytes=VMEM_LIMIT),
        name="lat_attn",
    )(q, k, v, kc, vc, bias_tiles)


def _bias_tile_constants():
    n_dc = 2 * NA_WIN_COLS
    w = np.arange(GRID_W)[:, None]
    cc = np.arange(2 * GRID_W)[None, :]
    c = cc % GRID_W
    cs = np.clip(w - NA_WIN_COLS // 2, 0, GRID_W - NA_WIN_COLS)
    valid = (c >= cs) & (c < cs + NA_WIN_COLS)
    k_idx = (cc // GRID_W) * n_dc + (c - w + NA_WIN_COLS - 1)
    onehot = (np.arange(2 * n_dc)[:, None, None] == k_idx[None]) & valid[None]
    onehot = onehot.reshape(2 * n_dc, GRID_W * 2 * GRID_W)
    mask = np.where(valid, 0.0, MASK_VALUE).reshape(1, GRID_W * 2 * GRID_W)
    return jnp.asarray(onehot, BF16), jnp.asarray(mask, F32)


def _bias_tiles_kernel(rb_ref, oh_ref, mask_ref, o_ref):
    hi, mid, lo = _split3(rb_ref[...])
    oh = oh_ref[...]
    o_ref[...] = (_dot(hi, oh) + _dot(mid, oh) + _dot(lo, oh)) * LOG2_E + mask_ref[...]


def _bias_tiles(rel_bias_l):
    n_dr = 2 * NA_WIN_ROWS - 1
    padded = jnp.pad(rel_bias_l, ((0, 0), (0, 0), (0, 1)))
    pairs = jnp.concatenate([padded[:, :-1], padded[:, 1:]], axis=-1)
    pairs = pairs.reshape(NA_HEADS * (n_dr - 1), 4 * NA_WIN_COLS)
    onehot, mask = _bias_tile_constants()
    full = lambda shape: pl.BlockSpec(shape, lambda: (0,) * len(shape))
    out = pl.pallas_call(
        _bias_tiles_kernel,
        in_specs=[full(pairs.shape), full(onehot.shape), full(mask.shape)],
        out_specs=full((pairs.shape[0], onehot.shape[1])),
        out_shape=jax.ShapeDtypeStruct((pairs.shape[0], onehot.shape[1]), F32),
        name="bias_tiles",
    )(pairs, onehot, mask)
    return out.reshape(NA_HEADS, n_dr - 1, GRID_W, 2 * GRID_W)


def _gla_block(q_ref, k_ref, v_ref, b_ref, st_ref, o_ref, expand, head_mask, bi, r0, reverse):
    half = GLA_BLOCK // 2
    rows = pl.ds(r0, GLA_BLOCK)
    qb = q_ref[bi, rows, :]
    kb = k_ref[bi, rows, :]
    vb = v_ref[bi, rows, :]
    bb = b_ref[bi, rows, :]
    end_row = 0 if reverse else GLA_BLOCK - 1
    b_end = bb[end_row:end_row + 1, :]
    st = st_ref[bi]

    o_inter = _dot_nt((qb * jnp.exp2(bb)).astype(BF16), st.astype(BF16))

    t_in = lax.broadcasted_iota(jnp.int32, (half, 1), 0)
    q_half = (qb[:half], qb[half:])
    b_half = (bb[:half], bb[half:])
    pieces, owners = [], []
    for s in range(GLA_BLOCK):
        hs = s // half
        k_s = kb[s:s + 1, :]
        b_s = bb[s:s + 1, :]
        for ht in range(2):
            if (ht > hs) if reverse else (ht < hs):
                continue
            decay = jnp.exp2(b_half[ht] - b_s)
            if ht == hs:
                seen = (t_in <= s - hs * half) if reverse else (t_in >= s - hs * half)
                decay = jnp.where(seen, decay, 0.0)
            pieces.append(q_half[ht] * k_s * decay)
            owners.append((ht, s))
    a_exp = _dot(jnp.concatenate(pieces, axis=0).astype(BF16), expand)
    o_intra = [jnp.zeros((half, GLA_WIDTH), F32), jnp.zeros((half, GLA_WIDTH), F32)]
    for i, (ht, s) in enumerate(owners):
        o_intra[ht] = o_intra[ht] + a_exp[i * half:(i + 1) * half, :] * vb[s:s + 1, :]
    o_ref[bi, rows, :] = o_inter + jnp.concatenate(o_intra, axis=0)

    k_hat = (kb * jnp.exp2(b_end - bb)).astype(BF16)
    upd = _dot_tn(vb.astype(BF16), k_hat)
    st_ref[bi] = st * jnp.exp2(b_end) + upd * head_mask


def _gla_kernel(qf_ref, kf_ref, vf_ref, laf_ref, qb_ref, kb_ref, vb_ref, lab_ref, s0f_ref, s0b_ref,
                trif_ref, trib_ref, e_ref, of_ref, ob_ref, sff_ref, sfb_ref,
                stf_ref, stb_ref, bf_ref, bb_ref, *, tm, nb):
    n = pl.program_id(1)
    n_blocks = tm // GLA_BLOCK

    @pl.when(n == 0)
    def _():
        stf_ref[...] = s0f_ref[...]
        stb_ref[...] = s0b_ref[...]

    for la_ref, tri_ref, b_ref in ((laf_ref, trif_ref, bf_ref), (lab_ref, trib_ref, bb_ref)):
        tri = tri_ref[...]
        for bi in range(nb):
            g_hi, g_mid, _ = _split3(la_ref[bi])
            sums = _dot(tri, jnp.concatenate([g_hi, g_mid], axis=-1))
            b_ref[bi] = sums[:, :GLA_KEY_WIDTH] + sums[:, GLA_KEY_WIDTH:]

    expand = e_ref[...]
    head_v = lax.broadcasted_iota(jnp.int32, (GLA_WIDTH, GLA_KEY_WIDTH), 0) // GLA_DV
    head_k = lax.broadcasted_iota(jnp.int32, (GLA_WIDTH, GLA_KEY_WIDTH), 1) // GLA_DK
    head_mask = jnp.where(head_v == head_k, 1.0, 0.0)

    def block(j, carry):
        r_f = pl.multiple_of(j * GLA_BLOCK, GLA_BLOCK)
        r_b = pl.multiple_of((n_blocks - 1 - j) * GLA_BLOCK, GLA_BLOCK)
        for bi in range(nb):
            _gla_block(qf_ref, kf_ref, vf_ref, bf_ref, stf_ref, of_ref, expand, head_mask, bi, r_f, reverse=False)
            _gla_block(qb_ref, kb_ref, vb_ref, bb_ref, stb_ref, ob_ref, expand, head_mask, bi, r_b, reverse=True)
        return carry

    lax.fori_loop(0, n_blocks, block, 0)

    @pl.when(n == pl.num_programs(1) - 1)
    def _():
        sff_ref[...] = stf_ref[...]
        sfb_ref[...] = stb_ref[...]


def _block_tri(tm, reverse):
    t = np.arange(tm)[:, None]
    s = np.arange(tm)[None, :]
    same = (t // GLA_BLOCK) == (s // GLA_BLOCK)
    return jnp.asarray(same & ((s >= t) if reverse else (s <= t)), BF16)


def _gla_scan(q, k, v, la, s0f_t, s0b_t, expand):
    B, L, _ = q.shape
    tm = min(GLA_TILE, L)
    n_tiles = L // tm
    nb = math.gcd(B, GLA_BATCH_PER_STEP)
    fwd = lambda b, n: (b, n, 0)
    bwd = lambda b, n: (b, n_tiles - 1 - n, 0)
    state = pl.BlockSpec((nb, GLA_WIDTH, GLA_KEY_WIDTH), lambda b, n: (b, 0, 0))
    const = lambda shape: pl.BlockSpec(shape, lambda b, n: (0, 0))

    def operands(tmap, la_col):
        return [pl.BlockSpec((nb, tm, GLA_KEY_WIDTH), tmap), pl.BlockSpec((nb, tm, GLA_KEY_WIDTH), tmap),
                pl.BlockSpec((nb, tm, GLA_WIDTH), tmap),
                pl.BlockSpec((nb, tm, GLA_KEY_WIDTH), lambda b, n: tmap(b, n)[:2] + (la_col,))]

    return pl.pallas_call(
        functools.partial(_gla_kernel, tm=tm, nb=nb),
        grid=(B // nb, n_tiles),
        in_specs=operands(fwd, 0) + operands(bwd, 1) + [state, state, const((tm, tm)), const((tm, tm)),
                                                        const((GLA_KEY_WIDTH, GLA_WIDTH))],
        out_specs=[pl.BlockSpec((nb, tm, GLA_WIDTH), fwd), pl.BlockSpec((nb, tm, GLA_WIDTH), bwd), state, state],
        out_shape=[jax.ShapeDtypeStruct((B, L, GLA_WIDTH), F32), jax.ShapeDtypeStruct((B, L, GLA_WIDTH), F32),
                   jax.ShapeDtypeStruct((B, GLA_WIDTH, GLA_KEY_WIDTH), F32),
                   jax.ShapeDtypeStruct((B, GLA_WIDTH, GLA_KEY_WIDTH), F32)],
        scratch_shapes=[pltpu.VMEM((nb, GLA_WIDTH, GLA_KEY_WIDTH), F32), pltpu.VMEM((nb, GLA_WIDTH, GLA_KEY_WIDTH), F32),
                        pltpu.VMEM((nb, tm, GLA_KEY_WIDTH), F32), pltpu.VMEM((nb, tm, GLA_KEY_WIDTH), F32)],
        compiler_params=pltpu.CompilerParams(vmem_limit_bytes=VMEM_LIMIT),
        name="gla",
    )(q, k, v, la, q, k, v, la, s0f_t, s0b_t, _block_tri(tm, False), _block_tri(tm, True), expand)


def _state_to_kernel(s):
    B = s.shape[0]
    same_head = np.eye(GLA_HEADS, dtype=bool)[None, :, None, :, None]
    st = jnp.where(same_head, s.transpose(0, 1, 3, 2)[:, :, :, None, :], 0.0)
    return st.reshape(B, GLA_WIDTH, GLA_KEY_WIDTH)


def _state_from_kernel(st):
    B = st.shape[0]
    blocks = st.reshape(B, GLA_HEADS, GLA_DV, GLA_HEADS, GLA_DK)
    diag = jnp.stack([blocks[:, h, :, h, :] for h in range(GLA_HEADS)], axis=1)
    return diag.transpose(0, 1, 3, 2)


def _post_kernel(x_ref, m_ref, yp_ref, yn_ref, of_ref, ob_ref, lg_ref, gg_ref, g2_ref,
                 wo_ref, wi_ref, wf_ref, o_ref, acc_ref):
    gate1 = m_ref[0, 2:3, :]
    shift2 = m_ref[0, 3:4, :]
    scale2 = m_ref[0, 4:5, :]
    gate2 = m_ref[0, 5:6, :]

    o_la = _half_lane_rms(of_ref[0] + ob_ref[0], NORM_EPS) * gg_ref[...]
    y_la = (o_la * _silu(lg_ref[0])).astype(BF16)
    mixed = (_dot(yp_ref[0], wo_ref[0:POOL_WIDTH, :])
             + _dot(yn_ref[0], wo_ref[POOL_WIDTH:POOL_WIDTH + NA_WIDTH, :])
             + _dot(y_la, wo_ref[POOL_WIDTH + NA_WIDTH:D_MODEL, :]))
    x1 = x_ref[0] + gate1 * mixed

    h2 = (_row_rms(x1, NORM_EPS) * g2_ref[...] * (1.0 + scale2) + shift2).astype(BF16)
    for c0 in range(0, D_FF, FF_CHUNK):
        hg = _dot(h2, wi_ref[:, c0:c0 + FF_CHUNK])
        hu = _dot(h2, wi_ref[:, D_FF + c0:D_FF + c0 + FF_CHUNK])
        acc_ref[:, c0:c0 + FF_CHUNK] = (_silu(hg) * hu).astype(BF16)
    o_ref[0] = x1 + gate2 * _dot(acc_ref[...], wf_ref[...])


def _post(x, mods_l, row_of_batch, y_pool, y_na, o_f, o_b, lg, lw):
    B, L, _ = x.shape
    tm = min(ROW_TILE, L)
    const = lambda b, i: (0, 0)
    tile = lambda b, i: (b, i, 0)

    def resident(shape):
        return pl.BlockSpec(shape, const, pipeline_mode=pl.Buffered(1))

    return pl.pallas_call(
        _post_kernel,
        grid=(B, L // tm),
        in_specs=[
            pl.BlockSpec((1, tm, D_MODEL), tile),
            pl.BlockSpec((1, N_MODS, D_MODEL), lambda b, i: (row_of_batch(b), 0, 0)),
            pl.BlockSpec((1, tm, POOL_WIDTH), tile),
            pl.BlockSpec((1, tm, NA_WIDTH), tile),
            pl.BlockSpec((1, tm, GLA_WIDTH), tile),
            pl.BlockSpec((1, tm, GLA_WIDTH), tile),
            pl.BlockSpec((1, tm, GLA_WIDTH), tile),
            pl.BlockSpec((1, GLA_WIDTH), const),
            pl.BlockSpec((1, D_MODEL), const),
            resident((D_MODEL, D_MODEL)),
            resident((D_MODEL, 2 * D_FF)),
            resident((D_FF, D_MODEL)),
        ],
        out_specs=pl.BlockSpec((1, tm, D_MODEL), tile),
        out_shape=jax.ShapeDtypeStruct((B, L, D_MODEL), F32),
        scratch_shapes=[pltpu.VMEM((tm, D_FF), BF16)],
        compiler_params=pltpu.CompilerParams(vmem_limit_bytes=VMEM_LIMIT),
        name="post",
    )(x, mods_l, y_pool, y_na, o_f, o_b, lg, lw["gla_gain"], lw["norm2_gain"],
      lw["w_out"], lw["w_ffn_in"], lw["w_ffn_out"])


def _rope_tables(L):
    t = jnp.arange(L)
    row = (t // GRID_W).astype(F32)
    col = (t % GRID_W).astype(F32)
    half = GLA_DK // 2
    inv_freq = ROPE_THETA ** (-jnp.arange(0, half, 2, dtype=F32) / half)
    ang_r = row[:, None] * inv_freq
    ang_c = col[:, None] * inv_freq
    sign = jnp.concatenate([-jnp.ones((half // 2,), F32), jnp.ones((half // 2,), F32)])

    def lanes(fn, signed):
        per_axis = []
        for ang in (ang_r, ang_c):
            v = jnp.concatenate([fn(ang), fn(ang)], axis=-1)
            per_axis.append(v * sign if signed else v)
        return jnp.tile(jnp.concatenate(per_axis, axis=-1), (1, GLA_HEADS))

    return lanes(jnp.cos, False), lanes(jnp.sin, True)


def _layer_weights(l, w):
    tile_heads = lambda g, n: jnp.tile(g, n)[None, :]
    return {
        "norm1_gain": w["norm1_gain"][l][None, :],
        "norm2_gain": w["norm2_gain"][l][None, :],
        "w_in": w["w_in"][l].astype(BF16),
        "q_gain": tile_heads(w["q_norm_gain"][l], NA_HEADS),
        "k_gain": tile_heads(w["k_norm_gain"][l], NA_HEADS),
        "w_gate": jax.scipy.linalg.block_diag(w["w_gate_f"][l], w["w_gate_b"][l]).astype(BF16),
        "b_gate": jnp.concatenate([w["b_gate_f"][l], w["b_gate_b"][l]])[None, :],
        "w_pool_bd": jax.scipy.linalg.block_diag(*[w["w_pool"][l, g] for g in range(len(POOL_WINDOWS))]).astype(BF16),
        "pool_scale": w["pool_scale"][l][None, :],
        "gla_gain": tile_heads(w["gla_norm_gain"][l], GLA_HEADS),
        "w_out": w["w_out"][l].astype(BF16),
        "w_ffn_in": w["w_ffn_in"][l].astype(BF16),
        "w_ffn_out": w["w_ffn_out"][l].astype(BF16),
    }


def _trunk_layer(x, mods_l, row_of_batch, lw, expand, s0_f, s0_b, latent, kv_out=None):
    is_ctx = latent is None
    B, L, _ = x.shape
    per_token = (lambda a: a.reshape(1, B * L, a.shape[-1])) if is_ctx else (lambda a: a)
    per_seq = lambda a: a.reshape(B, L, a.shape[-1])
    outs = _in_proj(per_token(x), mods_l, row_of_batch, lw, None if is_ctx else latent[0],
                    kv_out + (L,) if is_ctx else None)
    u, q, k, v, lq, lk, lv, lg, la = [per_seq(a) for a in outs[:9]]
    y_pool = _pool(u, lw)
    if is_ctx:
        y_na = _ctx_attn(q, k, v)
    else:
        y_na = _lat_attn(q, k, v, latent[1], latent[2], latent[3])
    o_f, o_b, s_f, s_b = _gla_scan(lq, lk, lv, la, s0_f, s0_b, expand)
    x = per_seq(_post(per_token(x), mods_l, row_of_batch,
                      *[per_token(a) for a in (y_pool, y_na, o_f, o_b, lg)], lw))
    return x, s_f, s_b, tuple(outs[9:])


def kernel(x_prompt, x_sample, c, cache_na_k, cache_na_v, state_gla_fwd, state_gla_bwd, c_ctx, w_ada, b_ada, norm1_gain, norm2_gain, w_in, w_pool, pool_scale, q_norm_gain, k_norm_gain, rel_bias, w_gate_f, b_gate_f, w_gate_b, b_gate_b, gla_norm_gain, w_out, w_ffn_in, w_ffn_out):
    weights = dict(norm1_gain=norm1_gain, norm2_gain=norm2_gain, w_in=w_in, w_pool=w_pool, pool_scale=pool_scale,
                   q_norm_gain=q_norm_gain, k_norm_gain=k_norm_gain, w_gate_f=w_gate_f, b_gate_f=b_gate_f,
                   w_gate_b=w_gate_b, b_gate_b=b_gate_b, gla_norm_gain=gla_norm_gain, w_out=w_out,
                   w_ffn_in=w_ffn_in, w_ffn_out=w_ffn_out)
    B_ctx, L_ctx, _ = x_prompt.shape
    B_lat, L_lat, _ = x_sample.shape
    assert 1 + B_lat <= COND_ROWS

    conds = jnp.zeros((COND_ROWS, D_MODEL), F32).at[0].set(c_ctx).at[1:1 + B_lat].set(c)
    mods = _ada(conds, w_ada, b_ada).reshape(DEPTH, COND_ROWS, N_MODS, D_MODEL)
    lws = [_layer_weights(l, weights) for l in range(DEPTH)]
    head_of_k = np.arange(GLA_KEY_WIDTH) // GLA_DK
    head_of_v = np.arange(GLA_WIDTH) // GLA_DV
    expand = jnp.asarray(head_of_k[:, None] == head_of_v[None, :], BF16)

    xp = x_prompt
    zero_state = jnp.zeros((B_ctx, GLA_WIDTH, GLA_KEY_WIDTH), F32)
    new_k = jnp.zeros((B_ctx, DEPTH, NA_HEADS, L_ctx, NA_HEAD_DIM), F32)
    new_v = jnp.zeros((B_ctx, DEPTH, NA_HEADS, L_ctx, NA_HEAD_DIM), F32)
    sfs, sbs = [], []
    for l in range(DEPTH):
        xp, s_f, s_b, (new_k, new_v) = _trunk_layer(xp, mods[l], lambda b: 0, lws[l], expand,
                                                     zero_state, zero_state, None, (new_k, new_v, l))
        sfs.append(_state_from_kernel(s_f))
        sbs.append(_state_from_kernel(s_b))

    xs = x_sample
    rope_tabs = _rope_tables(L_lat)
    from_heads = lambda a: a.transpose(0, 2, 1, 3).reshape(B_lat, a.shape[2], NA_WIDTH).astype(BF16)
    for l in range(DEPTH):
        latent = (rope_tabs, from_heads(cache_na_k[:, l]), from_heads(cache_na_v[:, l]), _bias_tiles(rel_bias[l]))
        xs, _, _, _ = _trunk_layer(xs, mods[l], lambda b: b + 1, lws[l], expand,
                                   _state_to_kernel(state_gla_fwd[:, l]), _state_to_kernel(state_gla_bwd[:, l]),
                                   latent)

    return (xp, xs, new_k, new_v, jnp.stack(sfs, axis=1), jnp.stack(sbs, axis=1))
```
